```python
import math
import jax
import jax.numpy as jnp
from jax import lax
import numpy as np

D_MODEL = 2048
BATCH = 16
SEQ = 256
DEPTH = 2
DEC_BATCH = 2
DEC_SEQ = 1024
PAST_LEN = 256

GRID_W = 64
HEAD_DIM = 128
N_MIX_HEADS = D_MODEL // HEAD_DIM
HGRN_HEADS = N_MIX_HEADS // 4
HGRN_KDIM = 128
HGRN_VDIM = HEAD_DIM
GQA_Q_HEADS = (N_MIX_HEADS - HGRN_HEADS) // 2
GQA_KV_HEADS = GQA_Q_HEADS // 3
DIFF_HEADS = N_MIX_HEADS - HGRN_HEADS - GQA_Q_HEADS
DIFF_QK_DIM = HEAD_DIM // 2
DIFF_V_DIM = HEAD_DIM
HGRN_KW = HGRN_HEADS * HGRN_KDIM
HGRN_WIDTH = HGRN_HEADS * HGRN_VDIM
GQA_WIDTH = GQA_Q_HEADS * HEAD_DIM
DIFF_WIDTH = DIFF_HEADS * DIFF_V_DIM
MIX_WIDTH = HGRN_WIDTH + GQA_WIDTH + DIFF_WIDTH
SPLIT_SIZES = (HGRN_KW, HGRN_WIDTH, HGRN_WIDTH, HGRN_KW, HGRN_KW,
               GQA_Q_HEADS * HEAD_DIM, GQA_KV_HEADS * HEAD_DIM, GQA_KV_HEADS * HEAD_DIM,
               DIFF_HEADS * 2 * DIFF_QK_DIM, DIFF_HEADS * 2 * DIFF_QK_DIM, DIFF_HEADS * DIFF_V_DIM)
SPLIT_POINTS = tuple(sum(SPLIT_SIZES[:i + 1]) for i in range(len(SPLIT_SIZES) - 1))
IN_WIDTH = sum(SPLIT_SIZES)
FFN_DIM = ((8 * D_MODEL // 3 + 255) // 256) * 256
N_MOD = 9
Q_BLOCK = 128
HGRN_CHUNK = 64
ROPE_BASE = 10000.0
EPS = 1e-6

kernel_name = 'hybrid_hgrn2_gqa_diffattn_dit_step'


def rms_norm(x, g):
    xf = x.astype(jnp.float32)
    y = xf * lax.rsqrt(jnp.mean(xf * xf, axis=-1, keepdims=True) + EPS)
    return (y * g.astype(jnp.float32)).astype(x.dtype)


def axial_rope_tables(n_rows, dim):
    quarter = dim // 4
    rows = jnp.repeat(jnp.arange(n_rows), GRID_W)
    cols = jnp.tile(jnp.arange(GRID_W), n_rows)
    pos = jnp.stack([rows, cols], axis=-1).astype(jnp.float32)
    inv = ROPE_BASE ** (-jnp.arange(quarter, dtype=jnp.float32) / quarter)
    ang = pos[:, :, None] * inv
    return jnp.cos(ang), jnp.sin(ang)


def apply_axial_rope(x, cos, sin):
    dim = x.shape[-1]
    q = dim // 4
    xs = x.reshape(x.shape[:-1] + (2, 2, q))
    x1, x2 = xs[..., 0, :], xs[..., 1, :]
    bshape = (1, x.shape[1]) + (1,) * (x.ndim - 3) + (2, q)
    c = cos.reshape(bshape).astype(x.dtype)
    s = sin.reshape(bshape).astype(x.dtype)
    return jnp.stack([x1 * c - x2 * s, x2 * c + x1 * s], axis=-2).reshape(x.shape)


def sweep_query_blocks(block_fn, q):
    B, T = q.shape[0], q.shape[1]
    nb = T // Q_BLOCK
    qb = jnp.moveaxis(q.reshape((B, nb, Q_BLOCK) + q.shape[2:]), 1, 0)
    ob = lax.map(block_fn, qb)
    return jnp.moveaxis(ob, 0, 1).reshape((B, T) + ob.shape[3:])


def gqa_attend(q, k, v):
    B, _, Hq, Dh = q.shape
    Hkv = k.shape[2]
    G = Hq // Hkv
    scale = Dh ** -0.5

    def block(qb):
        qb = qb.reshape(B, Q_BLOCK, Hkv, G, Dh)
        s = jnp.einsum('bqhgd,bkhd->bhgqk', qb, k).astype(jnp.float32) * scale
        p = jax.nn.softmax(s, axis=-1).astype(v.dtype)
        return jnp.einsum('bhgqk,bkhd->bqhgd', p, v).reshape(B, Q_BLOCK, Hq * Dh)

    return sweep_query_blocks(block, q)


def diff_attend(q, k, v, lam, lam_init, sub_g):
    B, _, H, _, d = q.shape
    Dv = v.shape[-1]
    scale = d ** -0.5

    def block(qb):
        s = jnp.einsum('bqhcd,bkhcd->bhcqk', qb, k).astype(jnp.float32) * scale
        p = jax.nn.softmax(s, axis=-1)
        a = (p[:, :, 0] - lam * p[:, :, 1]).astype(v.dtype)
        o = jnp.einsum('bhqk,bkhe->bqhe', a, v)
        o = rms_norm(o, sub_g) * (1.0 - lam_init)
        return o.reshape(B, Q_BLOCK, H * Dv)

    return sweep_query_blocks(block, q)


def hgrn_lower_bounds(raw):
    p = jax.nn.softmax(raw.astype(jnp.float32), axis=0)
    cum = jnp.cumsum(p, axis=0)
    return cum - cum[0:1]


def hgrn_log_forget(z, lb):
    lbf = lb.astype(jnp.float32)
    return jnp.logaddexp(jnp.log(lbf), jnp.log1p(-lbf) + jax.nn.log_sigmoid(z.astype(jnp.float32)))


def hgrn_chunk_scan(q, k, v, logf, S0):
    B, T, H, _ = q.shape
    n = T // HGRN_CHUNK
    tri = jnp.tril(jnp.ones((HGRN_CHUNK, HGRN_CHUNK), dtype=bool))

    def to_chunks(a):
        return jnp.moveaxis(a.reshape(B, n, HGRN_CHUNK, H, a.shape[-1]), 1, 0)

    def step(S, inp):
        qc, kc, vc, lf = inp
        b = jnp.cumsum(lf, axis=1)
        b_last = b[:, -1]
        o_inter = jnp.einsum('bchk,bhkv->bchv', qc * jnp.exp(b), S)
        diff = b[:, :, None] - b[:, None, :]
        decay = jnp.exp(jnp.where(tri[None, :, :, None, None], diff, -jnp.inf))
        A = jnp.einsum('bthk,bshk,btshk->bhts', qc, kc, decay)
        o_intra = jnp.einsum('bhts,bshv->bthv', A, vc)
        S_new = jnp.exp(b_last)[..., None] * S + jnp.einsum(
            'bshk,bshv->bhkv', kc * jnp.exp(b_last[:, None] - b), vc)
        return S_new, o_inter + o_intra

    S_T, o = lax.scan(step, S0, (to_chunks(q), to_chunks(k), to_chunks(v), to_chunks(logf)))
    return jnp.moveaxis(o, 0, 1).reshape(B, T, H, v.shape[-1]), S_T


def hgrn_mixer(hq, hi, hg, hf_f, hf_b, lb, onorm_g, S0):
    B, T, _ = hq.shape
    f32 = jnp.float32
    q = jax.nn.silu(hq).reshape(B, T, HGRN_HEADS, HGRN_KDIM).astype(f32)
    v = hi.reshape(B, T, HGRN_HEADS, HGRN_VDIM).astype(f32)
    logf_f = hgrn_log_forget(hf_f, lb[0]).reshape(B, T, HGRN_HEADS, HGRN_KDIM)
    logf_b = hgrn_log_forget(hf_b, lb[1]).reshape(B, T, HGRN_HEADS, HGRN_KDIM)
    k_f = -jnp.expm1(logf_f)
    k_b = -jnp.expm1(logf_b)
    if S0 is None:
        S0_f = jnp.zeros((B, HGRN_HEADS, HGRN_KDIM, HGRN_VDIM), f32)
        S0_b = S0_f
    else:
        S0_f = S0[:, 0].astype(f32)
        S0_b = S0[:, 1].astype(f32)
    o_f, S_f = hgrn_chunk_scan(q, k_f, v, logf_f, S0_f)
    flip = lambda a: jnp.flip(a, axis=1)
    o_b, S_b = hgrn_chunk_scan(flip(q), flip(k_b), flip(v), flip(logf_b), S0_b)
    o = (o_f + flip(o_b)).astype(hq.dtype)
    o = rms_norm(o, onorm_g) * jax.nn.silu(hg).reshape(B, T, HGRN_HEADS, HGRN_VDIM)
    return o.reshape(B, T, HGRN_WIDTH), jnp.stack([S_f, S_b], axis=1).astype(hq.dtype)


def token_mixers(h, p, layer_idx, ctx, rope):
    B, T, _ = h.shape
    (hq, hi, hg, hff, hfb, gq, gk, gv, dq, dk, dv) = jnp.split(h @ p['w_in'], SPLIT_POINTS, axis=-1)
    o_h, S = hgrn_mixer(hq, hi, hg, hff, hfb, p['lb'], p['hgrn_onorm_g'],
                        None if ctx is None else ctx[4])
    q = rms_norm(gq.reshape(B, T, GQA_Q_HEADS, HEAD_DIM), p['gqa_qnorm_g'])
    k = rms_norm(gk.reshape(B, T, GQA_KV_HEADS, HEAD_DIM), p['gqa_knorm_g'])
    v = gv.reshape(B, T, GQA_KV_HEADS, HEAD_DIM)
    q2 = rms_norm(dq.reshape(B, T, DIFF_HEADS, 2, DIFF_QK_DIM), p['diff_qnorm_g'])
    k2 = rms_norm(dk.reshape(B, T, DIFF_HEADS, 2, DIFF_QK_DIM), p['diff_knorm_g'])
    v2 = dv.reshape(B, T, DIFF_HEADS, DIFF_V_DIM)
    lam_init = 0.8 - 0.6 * math.exp(-0.3 * layer_idx)
    lv = p['diff_lambda'].astype(jnp.float32)
    lam = jnp.exp(jnp.sum(lv[0] * lv[1])) - jnp.exp(jnp.sum(lv[2] * lv[3])) + lam_init
    if ctx is None:
        o_g = gqa_attend(q, k, v)
        o_d = diff_attend(q2, k2, v2, lam, lam_init, p['diff_subln_g'])
        new_ctx = (k, v, k2, v2, S)
    else:
        cos_g, sin_g, cos_d, sin_d = rope
        q = apply_axial_rope(q, cos_g, sin_g)
        k = apply_axial_rope(k, cos_g, sin_g)
        q2 = apply_axial_rope(q2, cos_d, sin_d)
        k2 = apply_axial_rope(k2, cos_d, sin_d)
        kc_g, vc_g, kc_d, vc_d = ctx[0], ctx[1], ctx[2], ctx[3]
        o_g = gqa_attend(q, jnp.concatenate([kc_g.astype(k.dtype), k], axis=1),
                         jnp.concatenate([vc_g.astype(v.dtype), v], axis=1))
        o_d = diff_attend(q2, jnp.concatenate([kc_d.astype(k2.dtype), k2], axis=1),
                          jnp.concatenate([vc_d.astype(v2.dtype), v2], axis=1),
                          lam, lam_init, p['diff_subln_g'])
        new_ctx = None
    out = jnp.concatenate([o_h, o_g, o_d], axis=-1) @ p['w_out']
    return out, new_ctx


def swiglu(h, wg, wu, wd):
    return (jax.nn.silu(h @ wg) * (h @ wu)) @ wd


def layer(x, cond, p, layer_idx, ctx, rope):
    mod = (jax.nn.silu(cond) @ p['w_mod'] + p['b_mod']).reshape(cond.shape[0], 1, N_MOD, D_MODEL)

    def mod_norm(x, j):
        shift, scale = mod[:, :, 3 * j], mod[:, :, 3 * j + 1]
        return rms_norm(x, p['norm_g'][j]) * (1.0 + scale) + shift

    x = x + mod[:, :, 2] * (0.5 * swiglu(mod_norm(x, 0), p['ffn_w_gate'][0], p['ffn_w_up'][0], p['ffn_w_down'][0]))
    m, new_ctx = token_mixers(mod_norm(x, 1), p, layer_idx, ctx, rope)
    x = x + mod[:, :, 5] * m
    x = x + mod[:, :, 8] * (0.5 * swiglu(mod_norm(x, 2), p['ffn_w_gate'][1], p['ffn_w_up'][1], p['ffn_w_down'][1]))
    return x, new_ctx


def setup_inputs(seed: int = 0) -> dict:
    key = jax.random.key(seed)
    ks = iter(jax.random.split(key, 32))
    f32 = jnp.float32
    nrm = lambda shape, s=1.0: jax.random.normal(next(ks), shape, f32) * s
    gain = lambda shape: 1.0 + 0.02 * jax.random.normal(next(ks), shape, f32)
    return {
        'x_prompt': nrm((BATCH, SEQ, D_MODEL)),
        'x_sample': nrm((DEC_BATCH, DEC_SEQ, D_MODEL)),
        'c': nrm((DEC_BATCH, D_MODEL)),
        'cache_gqa_k': nrm((DEC_BATCH, DEPTH, PAST_LEN, GQA_KV_HEADS, HEAD_DIM)),
        'cache_gqa_v': nrm((DEC_BATCH, DEPTH, PAST_LEN, GQA_KV_HEADS, HEAD_DIM)),
        'cache_diff_k': nrm((DEC_BATCH, DEPTH, PAST_LEN, DIFF_HEADS, 2, DIFF_QK_DIM)),
        'cache_diff_v': nrm((DEC_BATCH, DEPTH, PAST_LEN, DIFF_HEADS, DIFF_V_DIM)),
        'state_hgrn': nrm((DEC_BATCH, DEPTH, 2, HGRN_HEADS, HGRN_KDIM, HGRN_VDIM), 0.5),
        'c_ctx': nrm((D_MODEL,)),
        'w_mod': nrm((DEPTH, D_MODEL, N_MOD * D_MODEL), 0.5 * D_MODEL ** -0.5),
        'b_mod': nrm((DEPTH, N_MOD * D_MODEL), 0.02),
        'norm_g': gain((DEPTH, 3, D_MODEL)),
        'ffn_w_gate': nrm((DEPTH, 2, D_MODEL, FFN_DIM), D_MODEL ** -0.5),
        'ffn_w_up': nrm((DEPTH, 2, D_MODEL, FFN_DIM), D_MODEL ** -0.5),
        'ffn_w_down': nrm((DEPTH, 2, FFN_DIM, D_MODEL), FFN_DIM ** -0.5),
        'w_in': nrm((DEPTH, D_MODEL, IN_WIDTH), D_MODEL ** -0.5),
        'w_out': nrm((DEPTH, MIX_WIDTH, D_MODEL), MIX_WIDTH ** -0.5),
        'hgrn_lb_raw': nrm((DEPTH, 2, HGRN_KW), 0.5),
        'hgrn_onorm_g': gain((DEPTH, HGRN_VDIM)),
        'gqa_qnorm_g': gain((DEPTH, HEAD_DIM)),
        'gqa_knorm_g': gain((DEPTH, HEAD_DIM)),
        'diff_qnorm_g': gain((DEPTH, DIFF_QK_DIM)),
        'diff_knorm_g': gain((DEPTH, DIFF_QK_DIM)),
        'diff_lambda': nrm((DEPTH, 4, DIFF_QK_DIM), 0.1),
        'diff_subln_g': gain((DEPTH, DIFF_V_DIM)),
    }


def reference(x_prompt, x_sample, c, cache_gqa_k, cache_gqa_v, cache_diff_k, cache_diff_v,
              state_hgrn, c_ctx, w_mod, b_mod, norm_g, ffn_w_gate, ffn_w_up, ffn_w_down,
              w_in, w_out, hgrn_lb_raw, hgrn_onorm_g, gqa_qnorm_g, gqa_knorm_g,
              diff_qnorm_g, diff_knorm_g, diff_lambda, diff_subln_g):
    lb_all = hgrn_lower_bounds(hgrn_lb_raw)
    grid_rows = x_sample.shape[1] // GRID_W
    rope = axial_rope_tables(grid_rows, HEAD_DIM) + axial_rope_tables(grid_rows, DIFF_QK_DIM)
    cond_ctx = c_ctx[None, :]
    xp, xs = x_prompt, x_sample
    kg_l, vg_l, kd_l, vd_l, s_l = [], [], [], [], []
    for l in range(DEPTH):
        p = {
            'w_mod': w_mod[l], 'b_mod': b_mod[l], 'norm_g': norm_g[l],
            'ffn_w_gate': ffn_w_gate[l], 'ffn_w_up': ffn_w_up[l], 'ffn_w_down': ffn_w_down[l],
            'w_in': w_in[l], 'w_out': w_out[l], 'lb': lb_all[l],
            'hgrn_onorm_g': hgrn_onorm_g[l], 'gqa_qnorm_g': gqa_qnorm_g[l],
            'gqa_knorm_g': gqa_knorm_g[l], 'diff_qnorm_g': diff_qnorm_g[l],
            'diff_knorm_g': diff_knorm_g[l], 'diff_lambda': diff_lambda[l],
            'diff_subln_g': diff_subln_g[l],
        }
        xp, (kg, vg, kd, vd, s) = layer(xp, cond_ctx, p, l, None, None)
        kg_l.append(kg); vg_l.append(vg); kd_l.append(kd); vd_l.append(vd); s_l.append(s)
        ctx = (cache_gqa_k[:, l], cache_gqa_v[:, l], cache_diff_k[:, l], cache_diff_v[:, l], state_hgrn[:, l])
        xs, _ = layer(xs, c, p, l, ctx, rope)
    return (xp, xs, jnp.stack(kg_l, axis=1), jnp.stack(vg_l, axis=1), jnp.stack(kd_l, axis=1),
            jnp.stack(vd_l, axis=1), jnp.stack(s_l, axis=1))
```

```python
import functools
import math

import jax
import jax.numpy as jnp
from jax import lax
from jax.experimental import pallas as pl
from jax.experimental.pallas import tpu as pltpu

F32 = jnp.float32
BF16 = jnp.bfloat16

D_MODEL = 2048
BATCH = 16
SEQ = 256
DEPTH = 2
DEC_BATCH = 2
DEC_SEQ = 1024
PAST_LEN = 256
GRID_W = 64
HEAD_DIM = 128
HGRN_HEADS = 4
GQA_Q_HEADS = 6
GQA_KV_HEADS = 2
GQA_GROUP = GQA_Q_HEADS // GQA_KV_HEADS
DIFF_HEADS = 6
DIFF_QK_DIM = 64
FFN_DIM = 5632
N_MOD = 9
IN_WIDTH = 6144
ROPE_BASE = 10000.0
EPS = 1e-6

N_CTX = BATCH * SEQ
N_LAT = DEC_BATCH * DEC_SEQ
N_ROWS = N_CTX + N_LAT
N_COND = 1 + DEC_BATCH
COND_PAD = 8

COL_HQ, COL_HI, COL_HG, COL_HFF, COL_HFB = 0, 4, 8, 12, 16
COL_GQ, COL_GK, COL_GV = 20, 26, 28
COL_DQ, COL_DK, COL_DV = 30, 36, 42

HGRN_W = HGRN_HEADS * HEAD_DIM
GQA_W = GQA_Q_HEADS * HEAD_DIM
DIFF_W = DIFF_HEADS * HEAD_DIM

VMEM_LIMIT = 60 * 1024 * 1024

TM = 1024
ROW_CHUNK = 128
FFN_TF = 256
FFN_NB = 512
IN_TN = 512
OUT_TN = 512
MOD_TN = 1024
HGRN_CHUNK = 64
HGRN_SUB = 16
Q_BLOCK = 256


def _cond_of_tile(i):
    n_ctx_tiles = N_CTX // TM
    tiles_per_latent = DEC_SEQ // TM
    return jnp.where(i < n_ctx_tiles, 0, 1 + (i - n_ctx_tiles) // tiles_per_latent)


def _silu(x):
    return x * jax.nn.sigmoid(x)


def _dot(a, b):
    return jnp.dot(a, b, preferred_element_type=F32)


def _dot_nt(a, b):
    return lax.dot_general(a, b, (((1,), (1,)), ((), ())), preferred_element_type=F32)


def _dot_tn(a, b):
    return lax.dot_general(a, b, (((0,), (0,)), ((), ())), preferred_element_type=F32)


def _rms(x, g):
    ms = jnp.mean(x * x, axis=-1, keepdims=True)
    return (x * lax.rsqrt(ms + EPS)) * g


def _mod_kernel(cond_ref, w_ref, b_ref, o_ref):
    a = _silu(cond_ref[...]).astype(BF16)
    o_ref[...] = _dot(a, w_ref[...].astype(BF16)) + b_ref[...]


def _modulation(cond, w_mod, b_mod):
    width = N_MOD * D_MODEL
    out = pl.pallas_call(
        _mod_kernel,
        out_shape=jax.ShapeDtypeStruct((DEPTH, COND_PAD, width), F32),
        grid=(DEPTH, width // MOD_TN),
        in_specs=[
            pl.BlockSpec((COND_PAD, D_MODEL), lambda l, j: (0, 0)),
            pl.BlockSpec((None, D_MODEL, MOD_TN), lambda l, j: (l, 0, j)),
            pl.BlockSpec((None, 1, MOD_TN), lambda l, j: (l, 0, j)),
        ],
        out_specs=pl.BlockSpec((None, COND_PAD, MOD_TN), lambda l, j: (l, 0, j)),
        compiler_params=pltpu.CompilerParams(
            dimension_semantics=("parallel", "parallel"), vmem_limit_bytes=VMEM_LIMIT),
        name="modulation",
    )(cond, w_mod, b_mod.reshape(DEPTH, 1, width))
    return out[:, :N_COND].reshape(DEPTH, N_COND, N_MOD, D_MODEL)


def _mod_norm_into(x_ref, mod_ref, g_ref, h_ref, sub):
    shift = mod_ref[3 * sub:3 * sub + 1, :]
    scale1 = 1.0 + mod_ref[3 * sub + 1:3 * sub + 2, :]
    g = g_ref[sub:sub + 1, :]

    def body(r, carry):
        rows = pl.ds(pl.multiple_of(r * ROW_CHUNK, ROW_CHUNK), ROW_CHUNK)
        h_ref[rows, :] = (_rms(x_ref[rows, :], g) * scale1 + shift).astype(BF16)
        return carry

    lax.fori_loop(0, TM // ROW_CHUNK, body, 0)


def _ffn_kernel(x_ref, mod_ref, g_ref, wg_ref, wu_ref, wd_ref, o_ref, h_ref, *, sub):
    j = pl.program_id(1)

    @pl.when(j == 0)
    def _():
        _mod_norm_into(x_ref, mod_ref, g_ref, h_ref, sub)
        o_ref[...] = jnp.zeros_like(o_ref)

    h = h_ref[...]
    gate_act = _dot(h, wg_ref[...].astype(BF16))
    up = _dot(h, wu_ref[...].astype(BF16))
    a = (_silu(gate_act) * up).astype(BF16)
    for n in range(0, D_MODEL, FFN_NB):
        o_ref[:, n:n + FFN_NB] += _dot(a, wd_ref[:, n:n + FFN_NB].astype(BF16))

    @pl.when(j == pl.num_programs(1) - 1)
    def _():
        gate = mod_ref[3 * sub + 2:3 * sub + 3, :]

        def body(r, carry):
            rows = pl.ds(pl.multiple_of(r * ROW_CHUNK, ROW_CHUNK), ROW_CHUNK)
            o_ref[rows, :] = x_ref[rows, :] + gate * (0.5 * o_ref[rows, :])
            return carry

        lax.fori_loop(0, TM // ROW_CHUNK, body, 0)


def _ffn(x, mod_l, norm_g_l, w_gate, w_up, w_down, layer, which):
    sub = 2 * which
    return pl.pallas_call(
        functools.partial(_ffn_kernel, sub=sub),
        out_shape=jax.ShapeDtypeStruct((N_ROWS, D_MODEL), F32),
        grid=(N_ROWS // TM, FFN_DIM // FFN_TF),
        in_specs=[
            pl.BlockSpec((TM, D_MODEL), lambda i, j: (i, 0)),
            pl.BlockSpec((None, N_MOD, D_MODEL), lambda i, j: (_cond_of_tile(i), 0, 0)),
            pl.BlockSpec((3, D_MODEL), lambda i, j: (0, 0)),
            pl.BlockSpec((None, None, D_MODEL, FFN_TF), lambda i, j: (layer, which, 0, j)),
            pl.BlockSpec((None, None, D_MODEL, FFN_TF), lambda i, j: (layer, which, 0, j)),
            pl.BlockSpec((None, None, FFN_TF, D_MODEL), lambda i, j: (layer, which, j, 0)),
        ],
        out_specs=pl.BlockSpec((TM, D_MODEL), lambda i, j: (i, 0)),
        scratch_shapes=[pltpu.VMEM((TM, D_MODEL), BF16)],
        compiler_params=pltpu.CompilerParams(
            dimension_semantics=("parallel", "arbitrary"), vmem_limit_bytes=VMEM_LIMIT),
        name=f"ffn_l{layer}_h{which}",
    )(x, mod_l, norm_g_l, w_gate, w_up, w_down)


def _proj_in_kernel(x_ref, mod_ref, g_ref, w_ref, o_ref, h_ref):
    @pl.when(pl.program_id(1) == 0)
    def _():
        _mod_norm_into(x_ref, mod_ref, g_ref, h_ref, 1)

    o_ref[...] = _dot(h_ref[...], w_ref[...].astype(BF16))


def _proj_in(x, mod_l, norm_g_l, w_in, layer):
    return pl.pallas_call(
        _proj_in_kernel,
        out_shape=jax.ShapeDtypeStruct((N_ROWS, IN_WIDTH), F32),
        grid=(N_ROWS // TM, IN_WIDTH // IN_TN),
        in_specs=[
            pl.BlockSpec((TM, D_MODEL), lambda i, j: (i, 0)),
            pl.BlockSpec((None, N_MOD, D_MODEL), lambda i, j: (_cond_of_tile(i), 0, 0)),
            pl.BlockSpec((3, D_MODEL), lambda i, j: (0, 0)),
            pl.BlockSpec((None, D_MODEL, IN_TN), lambda i, j: (layer, 0, j)),
        ],
        out_specs=pl.BlockSpec((TM, IN_TN), lambda i, j: (i, j)),
        scratch_shapes=[pltpu.VMEM((TM, D_MODEL), BF16)],
        compiler_params=pltpu.CompilerParams(
            dimension_semantics=("parallel", "arbitrary"), vmem_limit_bytes=VMEM_LIMIT),
        name=f"proj_in_l{layer}",
    )(x, mod_l, norm_g_l, w_in)


def _proj_out_kernel(x_ref, mod_ref, oh_ref, og_ref, od_ref, w_ref, o_ref):
    m = _dot(oh_ref[...], w_ref[0:HGRN_W, :].astype(BF16))
    m += _dot(og_ref[...], w_ref[HGRN_W:HGRN_W + GQA_W, :].astype(BF16))
    m += _dot(od_ref[...], w_ref[HGRN_W + GQA_W:, :].astype(BF16))
    o_ref[...] = x_ref[...] + mod_ref[5:6, :] * m


def _proj_out(x, mod_l, o_h, o_g, o_d, w_out, layer):
    return pl.pallas_call(
        _proj_out_kernel,
        out_shape=jax.ShapeDtypeStruct((N_ROWS, D_MODEL), F32),
        grid=(N_ROWS // TM, D_MODEL // OUT_TN),
        in_specs=[
            pl.BlockSpec((TM, OUT_TN), lambda i, j: (i, j)),
            pl.BlockSpec((None, N_MOD, OUT_TN), lambda i, j: (_cond_of_tile(i), 0, j)),
            pl.BlockSpec((TM, HGRN_W), lambda i, j: (i, 0)),
            pl.BlockSpec((TM, GQA_W), lambda i, j: (i, 0)),
            pl.BlockSpec((TM, DIFF_W), lambda i, j: (i, 0)),
            pl.BlockSpec((None, D_MODEL, OUT_TN), lambda i, j: (layer, 0, j)),
        ],
        out_specs=pl.BlockSpec((TM, OUT_TN), lambda i, j: (i, j)),
        compiler_params=pltpu.CompilerParams(
            dimension_semantics=("parallel", "parallel"), vmem_limit_bytes=VMEM_LIMIT),
        name=f"proj_out_l{layer}",
    )(x, mod_l, o_h, o_g, o_d, w_out)


def _log_sigmoid(z):
    return jnp.minimum(z, 0.0) - jnp.log1p(jnp.exp(-jnp.abs(z)))


def _log_forget(z, lb):
    a = jnp.log(lb)
    c = jnp.log1p(-lb) + _log_sigmoid(z)
    return jnp.maximum(a, c) + jnp.log1p(jnp.exp(-jnp.abs(a - c)))


def _hgrn_chunk(qs, v, z, lb, st, reverse):
    C, SUB = HGRN_CHUNK, HGRN_SUB
    lf = _log_forget(z, lb)
    kk = (1.0 - lb) * jax.nn.sigmoid(-z)
    r_i = lax.broadcasted_iota(jnp.int32, (C, C), 0)
    c_i = lax.broadcasted_iota(jnp.int32, (C, C), 1)
    tri = jnp.where((r_i <= c_i) if reverse else (r_i >= c_i), 1.0, 0.0).astype(F32)
    b = jnp.dot(tri, lf, precision=lax.Precision.HIGHEST, preferred_element_type=F32)
    b_tot = b[0:1, :] if reverse else b[C - 1:C, :]

    o = _dot_nt((qs * jnp.exp(b)).astype(BF16), st.astype(BF16))

    row_c = lax.broadcasted_iota(jnp.int32, (C, 1), 0)
    row_s = lax.broadcasted_iota(jnp.int32, (SUB, 1), 0)
    lane_c = lax.broadcasted_iota(jnp.int32, (SUB, C), 1)
    blocks = []
    for i in range(C // SUB):
        lo, hi = i * SUB, (i + 1) * SUB
        qi, bi, ki = qs[lo:hi, :], b[lo:hi, :], kk[lo:hi, :]
        acc = jnp.zeros((SUB, C), F32)
        for s in range(SUB):
            keep = (row_s <= s) if reverse else (row_s >= s)
            e = jnp.exp(jnp.where(keep, bi - bi[s:s + 1, :], -jnp.inf))
            col = jnp.sum(qi * e * ki[s:s + 1, :], axis=-1, keepdims=True)
            acc = jnp.where(lane_c == lo + s, col, acc)
        has_off = (i < C // SUB - 1) if reverse else (i > 0)
        if has_off:
            ref = b[hi:hi + 1, :] if reverse else b[lo - 1:lo, :]
            qt = (qi * jnp.exp(bi - ref)).astype(BF16)
            keys = (row_c >= hi) if reverse else (row_c < lo)
            kt = (kk * jnp.exp(jnp.where(keys, ref - b, -jnp.inf))).astype(BF16)
            acc = acc + _dot_nt(qt, kt)
        blocks.append(acc)
    a = jnp.concatenate(blocks, axis=0)
    o = o + _dot(a.astype(BF16), v.astype(BF16))

    k_dec = (kk * jnp.exp(b_tot - b)).astype(BF16)
    st_new = jnp.exp(b_tot) * st + _dot_tn(v.astype(BF16), k_dec)
    return o, st_new


def _hgrn_kernel(*refs, layer, seq, has_state):
    if has_state:
        (raw_ref, q_ref, v_ref, g_ref, ff_ref, fb_ref, ong_ref, s0_ref, _alias,
         o_ref, s_ref, of_ref, ob_ref, st_ref) = refs
    else:
        (raw_ref, q_ref, v_ref, g_ref, ff_ref, fb_ref, ong_ref,
         o_ref, s_ref, of_ref, ob_ref, st_ref) = refs
    C = HGRN_CHUNK
    n_chunks = seq // C

    def lower_bound(d):
        rows = [raw_ref[2 * l + d:2 * l + d + 1, :] for l in range(DEPTH)]
        m = functools.reduce(jnp.maximum, rows)
        e = [jnp.exp(r - m) for r in rows]
        tot = functools.reduce(lambda x, y: x + y, e)
        lb = jnp.zeros_like(m)
        for l in range(1, layer + 1):
            lb = lb + e[l] / tot
        return lb

    lb_f, lb_b = lower_bound(0), lower_bound(1)

    for d in range(2):
        st_ref[d] = s0_ref[d].T if has_state else jnp.zeros((HEAD_DIM, HEAD_DIM), F32)

    def body(ci, carry):
        rf = pl.ds(pl.multiple_of(ci * C, C), C)
        rb = pl.ds(pl.multiple_of((n_chunks - 1 - ci) * C, C), C)
        o_f, st_f = _hgrn_chunk(_silu(q_ref[rf, :]), v_ref[rf, :], ff_ref[rf, :], lb_f,
                                st_ref[0], False)
        o_b, st_b = _hgrn_chunk(_silu(q_ref[rb, :]), v_ref[rb, :], fb_ref[rb, :], lb_b,
                                st_ref[1], True)
        of_ref[rf, :] = o_f
        ob_ref[rb, :] = o_b
        st_ref[0] = st_f
        st_ref[1] = st_b
        return carry

    lax.fori_loop(0, n_chunks, body, 0)

    for d in range(2):
        s_ref[d] = st_ref[d].T
    o = of_ref[...] + ob_ref[...]
    o_ref[...] = (_rms(o, ong_ref[...]) * _silu(g_ref[...])).astype(o_ref.dtype)


def _hgrn(proj, lb_raw, onorm_g_l, layer, state0, o_prev):
    has_state = state0 is not None
    n_batch, seq, row0 = (DEC_BATCH, DEC_SEQ, N_CTX // DEC_SEQ) if has_state else (BATCH, SEQ, 0)

    def col(block):
        return pl.BlockSpec((seq, HEAD_DIM), lambda b, h: (row0 + b, block + h))

    in_specs = [
        pl.BlockSpec((2 * DEPTH, HEAD_DIM), lambda b, h: (0, h)),
        col(COL_HQ), col(COL_HI), col(COL_HG), col(COL_HFF), col(COL_HFB),
        pl.BlockSpec((1, HEAD_DIM), lambda b, h: (0, 0)),
    ]
    args = [lb_raw.reshape(2 * DEPTH, HGRN_W), proj, proj, proj, proj, proj,
            onorm_g_l.reshape(1, HEAD_DIM)]
    aliases = {}
    if has_state:
        in_specs += [
            pl.BlockSpec((None, None, 2, None, HEAD_DIM, HEAD_DIM),
                         lambda b, h: (b, layer, 0, h, 0, 0)),
            pl.BlockSpec(memory_space=pl.ANY),
        ]
        args += [state0, o_prev]
        aliases = {len(args) - 1: 0}
    o, s = pl.pallas_call(
        functools.partial(_hgrn_kernel, layer=layer, seq=seq, has_state=has_state),
        out_shape=(jax.ShapeDtypeStruct((N_ROWS, HGRN_W), BF16),
                   jax.ShapeDtypeStruct((n_batch, 2, HGRN_HEADS, HEAD_DIM, HEAD_DIM), F32)),
        grid=(n_batch, HGRN_HEADS),
        in_specs=in_specs,
        out_specs=(pl.BlockSpec((seq, HEAD_DIM), lambda b, h: (row0 + b, h)),
                   pl.BlockSpec((None, 2, None, HEAD_DIM, HEAD_DIM), lambda b, h: (b, 0, h, 0, 0))),
        scratch_shapes=[pltpu.VMEM((seq, HEAD_DIM), F32), pltpu.VMEM((seq, HEAD_DIM), F32),
                        pltpu.VMEM((2, HEAD_DIM, HEAD_DIM), F32)],
        input_output_aliases=aliases,
        compiler_params=pltpu.CompilerParams(
            dimension_semantics=("parallel", "parallel"), vmem_limit_bytes=VMEM_LIMIT),
        name=f"hgrn_l{layer}_{'lat' if has_state else 'ctx'}",
    )(*args)
    return o, s


def _swap_pairs(x, width):
    lanes = x.shape[-1]
    lane = lax.broadcasted_iota(jnp.int32, x.shape, x.ndim - 1)
    from_right = pltpu.roll(x, lanes - width, x.ndim - 1)
    from_left = pltpu.roll(x, width, x.ndim - 1)
    return jnp.where(lane % (2 * width) < width, from_right, from_left)


def _rope(x, cos, sin_signed, quarter):
    return x * cos + _swap_pairs(x, quarter) * sin_signed


def _rms_halves(x, g):
    half = x.shape[-1] // 2
    lane = lax.broadcasted_iota(jnp.int32, x.shape, x.ndim - 1)
    lo = lane < half
    sq = x * x
    ms_lo = jnp.sum(jnp.where(lo, sq, 0.0), axis=-1, keepdims=True) / half
    ms_hi = jnp.sum(jnp.where(lo, 0.0, sq), axis=-1, keepdims=True) / half
    ms = jnp.where(lo, ms_lo, ms_hi)
    return (x * lax.rsqrt(ms + EPS)) * g


def _gqa_kernel(*refs, seq, latent):
    if latent:
        (qg_ref, kg_ref, q_ref, k_ref, v_ref, ck_ref, cv_ref, cos_ref, sin_ref, _alias,
         o_ref, kt_ref, vt_ref) = refs
    else:
        (qg_ref, kg_ref, q_ref, k_ref, v_ref, o_ref, kn_ref, kt_ref, vt_ref) = refs
    past = PAST_LEN if latent else 0
    quarter = HEAD_DIM // 4

    @pl.when(pl.program_id(2) == 0)
    def _():
        kn = _rms(k_ref[...], kg_ref[...])
        if latent:
            kn = _rope(kn, cos_ref[...], sin_ref[...], quarter)
            kt_ref[0:past, :] = ck_ref[...].astype(BF16)
            vt_ref[0:past, :] = cv_ref[...].astype(BF16)
        else:
            kn_ref[...] = kn
        kt_ref[past:past + seq, :] = kn.astype(BF16)
        vt_ref[past:past + seq, :] = v_ref[...].astype(BF16)

    scale = HEAD_DIM ** -0.5

    def body(r, carry):
        rows = pl.ds(pl.multiple_of(r * Q_BLOCK, Q_BLOCK), Q_BLOCK)
        qn = _rms(q_ref[rows, :], qg_ref[...])
        if latent:
            qn = _rope(qn, cos_ref[rows, :], sin_ref[rows, :], quarter)
        s = _dot_nt(qn.astype(BF16), kt_ref[...]) * scale
        p = jnp.exp(s - jnp.max(s, axis=-1, keepdims=True))
        den = jnp.sum(p, axis=-1, keepdims=True)
        o = _dot(p.astype(BF16), vt_ref[...]) / den
        o_ref[rows, :] = o.astype(o_ref.dtype)
        return carry

    lax.fori_loop(0, seq // Q_BLOCK, body, 0)


def _gqa(proj, qnorm_g, knorm_g, layer, latent_args):
    latent = latent_args is not None
    n_batch, seq, row0 = (DEC_BATCH, DEC_SEQ, N_CTX // DEC_SEQ) if latent else (BATCH, SEQ, 0)
    past = PAST_LEN if latent else 0

    def vec():
        return pl.BlockSpec((1, HEAD_DIM), lambda b, h, g: (0, 0))

    in_specs = [
        vec(), vec(),
        pl.BlockSpec((seq, HEAD_DIM), lambda b, h, g: (row0 + b, COL_GQ + h * GQA_GROUP + g)),
        pl.BlockSpec((seq, HEAD_DIM), lambda b, h, g: (row0 + b, COL_GK + h)),
        pl.BlockSpec((seq, HEAD_DIM), lambda b, h, g: (row0 + b, COL_GV + h)),
    ]
    args = [qnorm_g.reshape(1, HEAD_DIM), knorm_g.reshape(1, HEAD_DIM), proj, proj, proj]
    o_spec = pl.BlockSpec((seq, HEAD_DIM), lambda b, h, g: (row0 + b, h * GQA_GROUP + g))
    o_shape = jax.ShapeDtypeStruct((N_ROWS, GQA_W), BF16)
    aliases = {}
    if latent:
        cache_k, cache_v, cos, sin_signed, o_prev = latent_args
        cache_spec = pl.BlockSpec((None, None, PAST_LEN, HEAD_DIM), lambda b, h, g: (b, layer, 0, h))
        table_spec = pl.BlockSpec((seq, HEAD_DIM), lambda b, h, g: (0, 0))
        in_specs += [cache_spec, cache_spec, table_spec, table_spec, pl.BlockSpec(memory_space=pl.ANY)]
        args += [cache_k.reshape(DEC_BATCH, DEPTH, PAST_LEN, GQA_KV_HEADS * HEAD_DIM),
                 cache_v.reshape(DEC_BATCH, DEPTH, PAST_LEN, GQA_KV_HEADS * HEAD_DIM),
                 cos, sin_signed, o_prev]
        aliases = {len(args) - 1: 0}
        out_shape, out_specs = o_shape, o_spec
    else:
        out_shape = (o_shape, jax.ShapeDtypeStruct((N_CTX, GQA_KV_HEADS * HEAD_DIM), F32))
        out_specs = (o_spec, pl.BlockSpec((seq, HEAD_DIM), lambda b, h, g: (b, h)))
    return pl.pallas_call(
        functools.partial(_gqa_kernel, seq=seq, latent=latent),
        out_shape=out_shape,
        grid=(n_batch, GQA_KV_HEADS, GQA_GROUP),
        in_specs=in_specs,
        out_specs=out_specs,
        scratch_shapes=[pltpu.VMEM((past + seq, HEAD_DIM), BF16),
                        pltpu.VMEM((past + seq, HEAD_DIM), BF16)],
        input_output_aliases=aliases,
        compiler_params=pltpu.CompilerParams(
            dimension_semantics=("parallel", "parallel", "arbitrary"), vmem_limit_bytes=VMEM_LIMIT),
        name=f"gqa_l{layer}_{'lat' if latent else 'ctx'}",
    )(*args)


def _diff_kernel(*refs, seq, latent, lam_init):
    if latent:
        (qg_ref, kg_ref, sg_ref, lam_ref, q_ref, k_ref, v_ref, ck_ref, cv_ref, cos_ref, sin_ref,
         _alias, o_ref, kt_ref, vt_ref) = refs
    else:
        (qg_ref, kg_ref, sg_ref, lam_ref, q_ref, k_ref, v_ref, o_ref, kn_ref, kt_ref, vt_ref) = refs
    past = PAST_LEN if latent else 0
    quarter = DIFF_QK_DIM // 4

    kn = _rms_halves(k_ref[...], kg_ref[...])
    if latent:
        kn = _rope(kn, cos_ref[...], sin_ref[...], quarter)
        kt_ref[0:past, :] = ck_ref[...].astype(BF16)
        vt_ref[0:past, :] = cv_ref[...].astype(BF16)
    else:
        kn_ref[...] = kn
    kt_ref[past:past + seq, :] = kn.astype(BF16)
    vt_ref[past:past + seq, :] = v_ref[...].astype(BF16)

    lv = lam_ref[...]
    lam = (jnp.exp(jnp.sum(lv[0:1, :] * lv[1:2, :], axis=-1, keepdims=True))
           - jnp.exp(jnp.sum(lv[2:3, :] * lv[3:4, :], axis=-1, keepdims=True)) + lam_init)
    scale = DIFF_QK_DIM ** -0.5

    def softmax(s):
        p = jnp.exp(s - jnp.max(s, axis=-1, keepdims=True))
        return p / jnp.sum(p, axis=-1, keepdims=True)

    def body(r, carry):
        rows = pl.ds(pl.multiple_of(r * Q_BLOCK, Q_BLOCK), Q_BLOCK)
        qn = _rms_halves(q_ref[rows, :], qg_ref[...])
        if latent:
            qn = _rope(qn, cos_ref[rows, :], sin_ref[rows, :], quarter)
        lo = lax.broadcasted_iota(jnp.int32, qn.shape, 1) < DIFF_QK_DIM
        kt = kt_ref[...]
        s1 = _dot_nt(jnp.where(lo, qn, 0.0).astype(BF16), kt) * scale
        s2 = _dot_nt(jnp.where(lo, 0.0, qn).astype(BF16), kt) * scale
        a = softmax(s1) - lam * softmax(s2)
        o = _dot(a.astype(BF16), vt_ref[...])
        o = _rms(o, sg_ref[...]) * (1.0 - lam_init)
        o_ref[rows, :] = o.astype(o_ref.dtype)
        return carry

    lax.fori_loop(0, seq // Q_BLOCK, body, 0)


def _diff(proj, qnorm_g, knorm_g, subln_g, lam_params, layer, latent_args):
    latent = latent_args is not None
    n_batch, seq, row0 = (DEC_BATCH, DEC_SEQ, N_CTX // DEC_SEQ) if latent else (BATCH, SEQ, 0)
    past = PAST_LEN if latent else 0
    lam_init = 0.8 - 0.6 * math.exp(-0.3 * layer)

    def vec():
        return pl.BlockSpec((1, HEAD_DIM), lambda b, h: (0, 0))

    in_specs = [
        vec(), vec(), vec(),
        pl.BlockSpec((4, DIFF_QK_DIM), lambda b, h: (0, 0)),
        pl.BlockSpec((seq, HEAD_DIM), lambda b, h: (row0 + b, COL_DQ + h)),
        pl.BlockSpec((seq, HEAD_DIM), lambda b, h: (row0 + b, COL_DK + h)),
        pl.BlockSpec((seq, HEAD_DIM), lambda b, h: (row0 + b, COL_DV + h)),
    ]
    args = [jnp.tile(qnorm_g, 2).reshape(1, HEAD_DIM), jnp.tile(knorm_g, 2).reshape(1, HEAD_DIM),
            subln_g.reshape(1, HEAD_DIM), lam_params, proj, proj, proj]
    o_spec = pl.BlockSpec((seq, HEAD_DIM), lambda b, h: (row0 + b, h))
    o_shape = jax.ShapeDtypeStruct((N_ROWS, DIFF_W), BF16)
    aliases = {}
    if latent:
        cache_k, cache_v, cos, sin_signed, o_prev = latent_args
        cache_spec = pl.BlockSpec((None, None, PAST_LEN, HEAD_DIM), lambda b, h: (b, layer, 0, h))
        table_spec = pl.BlockSpec((seq, HEAD_DIM), lambda b, h: (0, 0))
        in_specs += [cache_spec, cache_spec, table_spec, table_spec, pl.BlockSpec(memory_space=pl.ANY)]
        args += [cache_k.reshape(DEC_BATCH, DEPTH, PAST_LEN, DIFF_W),
                 cache_v.reshape(DEC_BATCH, DEPTH, PAST_LEN, DIFF_W), cos, sin_signed, o_prev]
        aliases = {len(args) - 1: 0}
        out_shape, out_specs = o_shape, o_spec
    else:
        out_shape = (o_shape, jax.ShapeDtypeStruct((N_CTX, DIFF_W), F32))
        out_specs = (o_spec, pl.BlockSpec((seq, HEAD_DIM), lambda b, h: (b, h)))
    return pl.pallas_call(
        functools.partial(_diff_kernel, seq=seq, latent=latent, lam_init=lam_init),
        out_shape=out_shape,
        grid=(n_batch, DIFF_HEADS),
        in_specs=in_specs,
        out_specs=out_specs,
        scratch_shapes=[pltpu.VMEM((past + seq, HEAD_DIM), BF16),
                        pltpu.VMEM((past + seq, HEAD_DIM), BF16)],
        input_output_aliases=aliases,
        compiler_params=pltpu.CompilerParams(
            dimension_semantics=("parallel", "parallel"), vmem_limit_bytes=VMEM_LIMIT),
        name=f"diff_l{layer}_{'lat' if latent else 'ctx'}",
    )(*args)


def _rope_tables(n_tokens, dim, repeat):
    quarter = dim // 4
    t = jnp.arange(n_tokens)
    pos = jnp.stack([t // GRID_W, t % GRID_W], axis=-1).astype(F32)
    inv = ROPE_BASE ** (-jnp.arange(quarter, dtype=F32) / quarter)
    ang = pos[:, :, None] * inv
    cos, sin = jnp.cos(ang), jnp.sin(ang)
    cos_l = jnp.concatenate([cos, cos], axis=-1).reshape(n_tokens, dim)
    sin_l = jnp.concatenate([-sin, sin], axis=-1).reshape(n_tokens, dim)
    return jnp.tile(cos_l, (1, repeat)), jnp.tile(sin_l, (1, repeat))


def kernel(x_prompt, x_sample, c, cache_gqa_k, cache_gqa_v, cache_diff_k, cache_diff_v, state_hgrn,
           c_ctx, w_mod, b_mod, norm_g, ffn_w_gate, ffn_w_up, ffn_w_down, w_in, w_out, hgrn_lb_raw,
           hgrn_onorm_g, gqa_qnorm_g, gqa_knorm_g, diff_qnorm_g, diff_knorm_g, diff_lambda,
           diff_subln_g):
    x = jnp.concatenate([x_prompt.reshape(N_CTX, D_MODEL), x_sample.reshape(N_LAT, D_MODEL)], axis=0)
    cond = jnp.concatenate(
        [c_ctx[None, :], c, jnp.zeros((COND_PAD - N_COND, D_MODEL), F32)], axis=0)
    mod = _modulation(cond, w_mod, b_mod)

    cos_g, sin_g = _rope_tables(DEC_SEQ, HEAD_DIM, 1)
    cos_d, sin_d = _rope_tables(DEC_SEQ, DIFF_QK_DIM, 2)
    cache_dk = cache_diff_k.reshape(DEC_BATCH, DEPTH, PAST_LEN, DIFF_HEADS, 2 * DIFF_QK_DIM)

    kg_l, vg_l, kd_l, vd_l, s_l = [], [], [], [], []
    for l in range(DEPTH):
        x = _ffn(x, mod[l], norm_g[l], ffn_w_gate, ffn_w_up, ffn_w_down, l, 0)
        proj = _proj_in(x, mod[l], norm_g[l], w_in, l)

        o_h, s_ctx = _hgrn(proj, hgrn_lb_raw, hgrn_onorm_g[l], l, None, None)
        o_h, _ = _hgrn(proj, hgrn_lb_raw, hgrn_onorm_g[l], l, state_hgrn, o_h)

        o_g, kn = _gqa(proj, gqa_qnorm_g[l], gqa_knorm_g[l], l, None)
        o_g = _gqa(proj, gqa_qnorm_g[l], gqa_knorm_g[l], l,
                   (cache_gqa_k, cache_gqa_v, cos_g, sin_g, o_g))

        o_d, k2n = _diff(proj, diff_qnorm_g[l], diff_knorm_g[l], diff_subln_g[l], diff_lambda[l], l,
                         None)
        o_d = _diff(proj, diff_qnorm_g[l], diff_knorm_g[l], diff_subln_g[l], diff_lambda[l], l,
                    (cache_dk, cache_diff_v, cos_d, sin_d, o_d))

        x = _proj_out(x, mod[l], o_h, o_g, o_d, w_out, l)
        x = _ffn(x, mod[l], norm_g[l], ffn_w_gate, ffn_w_up, ffn_w_down, l, 1)

        ctx = proj[:N_CTX]
        kg_l.append(kn.reshape(BATCH, SEQ, GQA_KV_HEADS, HEAD_DIM))
        vg_l.append(ctx[:, COL_GV * 128:(COL_GV + GQA_KV_HEADS) * 128]
                    .reshape(BATCH, SEQ, GQA_KV_HEADS, HEAD_DIM))
        kd_l.append(k2n.reshape(BATCH, SEQ, DIFF_HEADS, 2, DIFF_QK_DIM))
        vd_l.append(ctx[:, COL_DV * 128:(COL_DV + DIFF_HEADS) * 128]
                    .reshape(BATCH, SEQ, DIFF_HEADS, HEAD_DIM))
        s_l.append(s_ctx)

    y_prompt = x[:N_CTX].reshape(BATCH, SEQ, D_MODEL)
    y_sample = x[N_CTX:].reshape(DEC_BATCH, DEC_SEQ, D_MODEL)
    return (y_prompt, y_sample, jnp.stack(kg_l, axis=1), jnp.stack(vg_l, axis=1),
            jnp.stack(kd_l, axis=1), jnp.stack(vd_l, axis=1), jnp.stack(s_l, axis=1))
```

```python
import functools
import math

import jax
import jax.numpy as jnp
from jax import lax
from jax.experimental import pallas as pl
from jax.experimental.pallas import tpu as pltpu

F32 = jnp.float32
BF16 = jnp.bfloat16

D_MODEL = 2048
BATCH = 16
SEQ = 256
DEPTH = 2
DEC_BATCH = 2
DEC_SEQ = 1024
PAST_LEN = 256
GRID_W = 64
HEAD_DIM = 128
HGRN_HEADS = 4
GQA_Q_HEADS = 6
GQA_KV_HEADS = 2
GQA_GROUP = GQA_Q_HEADS // GQA_KV_HEADS
DIFF_HEADS = 6
DIFF_QK_DIM = 64
FFN_DIM = 5632
N_MOD = 9
IN_WIDTH = 6144
ROPE_BASE = 10000.0
EPS = 1e-6
LOG2E = math.log2(math.e)

N_CTX = BATCH * SEQ
N_LAT = DEC_BATCH * DEC_SEQ
N_ROWS = N_CTX + N_LAT
N_COND = 1 + DEC_BATCH
COND_PAD = 8

COL_HQ, COL_HI, COL_HG, COL_HFF, COL_HFB = 0, 4, 8, 12, 16
COL_GQ, COL_GK, COL_GV = 20, 26, 28
COL_DQ, COL_DK, COL_DV = 30, 36, 42

HGRN_W = HGRN_HEADS * HEAD_DIM
GQA_W = GQA_Q_HEADS * HEAD_DIM
GQA_KV_W = GQA_KV_HEADS * HEAD_DIM
DIFF_W = DIFF_HEADS * HEAD_DIM

VMEM_LIMIT = 60 * 1024 * 1024

TM = 1024
ROW_CHUNK = 128
FFN_TF = 256
FFN_NB = 512
IN_TN = 512
OUT_TN = 512
MOD_TN = 1024
HGRN_CHUNK = 64
HGRN_SUB = 8
HGRN_HP = 4
Q_BLOCK = 256

CTX_TILES = N_CTX // TM
LAT_TILES = N_LAT // TM
ALL_TILES = CTX_TILES + LAT_TILES


def _cond_of_tile(i):
    tiles_per_latent = DEC_SEQ // TM
    return jnp.where(i < CTX_TILES, 0, 1 + (i - CTX_TILES) // tiles_per_latent)


def _silu(x):
    return x * jax.nn.sigmoid(x)


def _dot(a, b):
    return jnp.dot(a, b, preferred_element_type=F32)


def _dot_nt(a, b):
    return lax.dot_general(a, b, (((1,), (1,)), ((), ())), preferred_element_type=F32)


def _dot_tn(a, b):
    return lax.dot_general(a, b, (((0,), (0,)), ((), ())), preferred_element_type=F32)


def _rms(x, g):
    ms = jnp.mean(x * x, axis=-1, keepdims=True)
    return (x * lax.rsqrt(ms + EPS)) * g


def _params(*semantics):
    return pltpu.CompilerParams(dimension_semantics=semantics, vmem_limit_bytes=VMEM_LIMIT)


_ANY = pl.BlockSpec(memory_space=pl.ANY)


def _mod_kernel(cond_ref, w_ref, b_ref, o_ref):
    a = _silu(cond_ref[...]).astype(BF16)
    o_ref[...] = _dot(a, w_ref[...].astype(BF16)) + b_ref[...]


def _modulation(cond, w_mod, b_mod):
    width = N_MOD * D_MODEL
    out = pl.pallas_call(
        _mod_kernel,
        out_shape=jax.ShapeDtypeStruct((DEPTH, COND_PAD, width), F32),
        grid=(DEPTH, width // MOD_TN),
        in_specs=[
            pl.BlockSpec((COND_PAD, D_MODEL), lambda l, j: (0, 0)),
            pl.BlockSpec((None, D_MODEL, MOD_TN), lambda l, j: (l, 0, j)),
            pl.BlockSpec((None, 1, MOD_TN), lambda l, j: (l, 0, j)),
        ],
        out_specs=pl.BlockSpec((None, COND_PAD, MOD_TN), lambda l, j: (l, 0, j)),
        compiler_params=_params("parallel", "parallel"),
        name="modulation",
    )(cond, w_mod, b_mod.reshape(DEPTH, 1, width))
    return out[:, :N_COND].reshape(DEPTH, N_COND, N_MOD, D_MODEL)


def _mod_norm_into(x_ref, mod_ref, g_ref, h_ref, sub):
    shift = mod_ref[3 * sub:3 * sub + 1, :]
    scale1 = 1.0 + mod_ref[3 * sub + 1:3 * sub + 2, :]
    g = g_ref[sub:sub + 1, :]

    def body(r, carry):
        rows = pl.ds(pl.multiple_of(r * ROW_CHUNK, ROW_CHUNK), ROW_CHUNK)
        h_ref[rows, :] = (_rms(x_ref[rows, :], g) * scale1 + shift).astype(BF16)
        return carry

    lax.fori_loop(0, TM // ROW_CHUNK, body, 0)


def _ffn_kernel(x_ref, mod_ref, g_ref, wg_ref, wu_ref, wd_ref, *rest, sub):
    o_ref, h_ref = rest[-2:]
    j = pl.program_id(1)

    @pl.when(j == 0)
    def _():
        _mod_norm_into(x_ref, mod_ref, g_ref, h_ref, sub)
        o_ref[...] = jnp.zeros_like(o_ref)

    h = h_ref[...]
    gate_act = _dot(h, wg_ref[...].astype(BF16))
    up = _dot(h, wu_ref[...].astype(BF16))
    a = (_silu(gate_act) * up).astype(BF16)
    for n in range(0, D_MODEL, FFN_NB):
        o_ref[:, n:n + FFN_NB] += _dot(a, wd_ref[:, n:n + FFN_NB].astype(BF16))

    @pl.when(j == pl.num_programs(1) - 1)
    def _():
        gate = mod_ref[3 * sub + 2:3 * sub + 3, :]

        def body(r, carry):
            rows = pl.ds(pl.multiple_of(r * ROW_CHUNK, ROW_CHUNK), ROW_CHUNK)
            o_ref[rows, :] = x_ref[rows, :] + gate * (0.5 * o_ref[rows, :])
            return carry

        lax.fori_loop(0, TM // ROW_CHUNK, body, 0)


def _ffn(x, mod_l, norm_g_l, w_gate, w_up, w_down, layer, which, *, n_tiles=ALL_TILES, in_off=0,
         out_off=0, tile0=0, out_rows=N_ROWS, o_prev=None):
    sub = 2 * which
    in_specs = [
        pl.BlockSpec((TM, D_MODEL), lambda i, j: (in_off + i, 0)),
        pl.BlockSpec((None, N_MOD, D_MODEL), lambda i, j: (_cond_of_tile(tile0 + i), 0, 0)),
        pl.BlockSpec((3, D_MODEL), lambda i, j: (0, 0)),
        pl.BlockSpec((None, None, D_MODEL, FFN_TF), lambda i, j: (layer, which, 0, j)),
        pl.BlockSpec((None, None, D_MODEL, FFN_TF), lambda i, j: (layer, which, 0, j)),
        pl.BlockSpec((None, None, FFN_TF, D_MODEL), lambda i, j: (layer, which, j, 0)),
    ]
    args = [x, mod_l, norm_g_l, w_gate, w_up, w_down]
    aliases = {}
    if o_prev is not None:
        in_specs.append(_ANY)
        args.append(o_prev)
        aliases = {len(args) - 1: 0}
    return pl.pallas_call(
        functools.partial(_ffn_kernel, sub=sub),
        out_shape=jax.ShapeDtypeStruct((out_rows, D_MODEL), F32),
        grid=(n_tiles, FFN_DIM // FFN_TF),
        in_specs=in_specs,
        out_specs=pl.BlockSpec((TM, D_MODEL), lambda i, j: (out_off + i, 0)),
        scratch_shapes=[pltpu.VMEM((TM, D_MODEL), BF16)],
        input_output_aliases=aliases,
        compiler_params=_params("parallel", "arbitrary"),
        name=f"ffn_l{layer}_h{which}_t{tile0}n{n_tiles}",
    )(*args)


def _proj_in_kernel(x_ref, mod_ref, g_ref, w_ref, o_ref, h_ref):
    @pl.when(pl.program_id(1) == 0)
    def _():
        _mod_norm_into(x_ref, mod_ref, g_ref, h_ref, 1)

    o_ref[...] = _dot(h_ref[...], w_ref[...].astype(BF16))


def _proj_in(x, mod_l, norm_g_l, w_in, layer):
    return pl.pallas_call(
        _proj_in_kernel,
        out_shape=jax.ShapeDtypeStruct((N_ROWS, IN_WIDTH), F32),
        grid=(ALL_TILES, IN_WIDTH // IN_TN),
        in_specs=[
            pl.BlockSpec((TM, D_MODEL), lambda i, j: (i, 0)),
            pl.BlockSpec((None, N_MOD, D_MODEL), lambda i, j: (_cond_of_tile(i), 0, 0)),
            pl.BlockSpec((3, D_MODEL), lambda i, j: (0, 0)),
            pl.BlockSpec((None, D_MODEL, IN_TN), lambda i, j: (layer, 0, j)),
        ],
        out_specs=pl.BlockSpec((TM, IN_TN), lambda i, j: (i, j)),
        scratch_shapes=[pltpu.VMEM((TM, D_MODEL), BF16)],
        compiler_params=_params("parallel", "arbitrary"),
        name=f"proj_in_l{layer}",
    )(x, mod_l, norm_g_l, w_in)


def _proj_out_kernel(x_ref, mod_ref, oh_ref, og_ref, od_ref, w_ref, o_ref):
    m = _dot(oh_ref[...], w_ref[0:HGRN_W, :].astype(BF16))
    m += _dot(og_ref[...], w_ref[HGRN_W:HGRN_W + GQA_W, :].astype(BF16))
    m += _dot(od_ref[...], w_ref[HGRN_W + GQA_W:, :].astype(BF16))
    o_ref[...] = x_ref[...] + mod_ref[5:6, :] * m


def _proj_out(x, mod_l, o_h, o_g, o_d, w_out, layer):
    return pl.pallas_call(
        _proj_out_kernel,
        out_shape=jax.ShapeDtypeStruct((N_ROWS, D_MODEL), F32),
        grid=(ALL_TILES, D_MODEL // OUT_TN),
        in_specs=[
            pl.BlockSpec((TM, OUT_TN), lambda i, j: (i, j)),
            pl.BlockSpec((None, N_MOD, OUT_TN), lambda i, j: (_cond_of_tile(i), 0, j)),
            pl.BlockSpec((TM, HGRN_W), lambda i, j: (i, 0)),
            pl.BlockSpec((TM, GQA_W), lambda i, j: (i, 0)),
            pl.BlockSpec((TM, DIFF_W), lambda i, j: (i, 0)),
            pl.BlockSpec((None, D_MODEL, OUT_TN), lambda i, j: (layer, 0, j)),
        ],
        out_specs=pl.BlockSpec((TM, OUT_TN), lambda i, j: (i, j)),
        compiler_params=_params("parallel", "parallel"),
        name=f"proj_out_l{layer}",
    )(x, mod_l, o_h, o_g, o_d, w_out)


def _log2_forget_and_key(z, lb):
    soft = jnp.log(1.0 + jnp.exp(-jnp.abs(z)))
    log_1mlb = jnp.log1p(-lb)
    a = jnp.log(lb)
    c = log_1mlb + (jnp.minimum(z, 0.0) - soft)
    log_f = jnp.maximum(a, c) + jnp.log(1.0 + jnp.exp(-jnp.abs(a - c)))
    log_k = log_1mlb + (jnp.minimum(-z, 0.0) - soft)
    return log_f * LOG2E, log_k * LOG2E


def _hgrn_chunk(qs, v, z, lb, st, reverse):
    C, SUB = HGRN_CHUNK, HGRN_SUB
    n_sub = C // SUB
    lf2, lk2 = _log2_forget_and_key(z, lb)
    r_i = lax.broadcasted_iota(jnp.int32, (C, C), 0)
    c_i = lax.broadcasted_iota(jnp.int32, (C, C), 1)
    tri = jnp.where((r_i <= c_i) if reverse else (r_i >= c_i), 1.0, 0.0).astype(F32)
    b = jnp.dot(tri, lf2, precision=lax.Precision.HIGHEST, preferred_element_type=F32)
    c = b - lk2
    b_tot = b[0:1, :] if reverse else b[C - 1:C, :]

    o = _dot_nt((qs * jnp.exp2(b)).astype(BF16), st.astype(BF16))

    row_s = lax.broadcasted_iota(jnp.int32, (SUB, 1), 0)
    lane_c = lax.broadcasted_iota(jnp.int32, (SUB, C), 1)

    terms = []
    for i in range(n_sub):
        lo = i * SUB
        qi, bi, ci = qs[lo:lo + SUB, :], b[lo:lo + SUB, :], c[lo:lo + SUB, :]
        for s in range(SUB):
            keep = (row_s <= s) if reverse else (row_s >= s)
            terms.append(qi * jnp.exp2(jnp.where(keep, bi - ci[s:s + 1, :], -jnp.inf)))
    k_sums = _dot(jnp.concatenate(terms, axis=0).astype(BF16), jnp.ones((HEAD_DIM, C), BF16))

    blocks = []
    for i in range(n_sub):
        lo, hi = i * SUB, (i + 1) * SUB
        acc = jnp.zeros((SUB, C), F32)
        for s in range(SUB):
            j = lo + s
            acc = jnp.where(lane_c == j, k_sums[j * SUB:(j + 1) * SUB, :], acc)
        has_off = (i < n_sub - 1) if reverse else (i > 0)
        if has_off:
            ref = b[hi:hi + 1, :] if reverse else b[lo - 1:lo, :]
            qt = (qs[lo:hi, :] * jnp.exp2(b[lo:hi, :] - ref)).astype(BF16)
            if reverse:
                kt = jnp.concatenate([jnp.zeros((hi, HEAD_DIM), F32), jnp.exp2(ref - c[hi:, :])], axis=0)
            else:
                kt = jnp.concatenate([jnp.exp2(ref - c[:lo, :]), jnp.zeros((C - lo, HEAD_DIM), F32)], axis=0)
            acc = acc + _dot_nt(qt, kt.astype(BF16))
        blocks.append(acc)
    a = jnp.concatenate(blocks, axis=0)
    o = o + _dot(a.astype(BF16), v.astype(BF16))

    st_new = jnp.exp2(b_tot) * st + _dot_tn(v.astype(BF16), jnp.exp2(b_tot - c).astype(BF16))
    return o, st_new


def _hgrn_kernel(*refs, layer, seq, latent, n_alias):
    raw_ref, q_ref, v_ref, g_ref, ff_ref, fb_ref, ong_ref = refs[:7]
    s0_ref = refs[7] if latent else None
    outs = refs[7 + (1 if latent else 0) + n_alias:]
    if latent:
        o_ref, of_ref, ob_ref, st_ref = outs
        s_ref = None
    else:
        o_ref, s_ref, of_ref, ob_ref, st_ref = outs
    C = HGRN_CHUNK
    n_chunks = seq // C

    def lower_bound(d, lanes):
        rows = [raw_ref[2 * l + d:2 * l + d + 1, lanes] for l in range(DEPTH)]
        m = functools.reduce(jnp.maximum, rows)
        e = [jnp.exp(r - m) for r in rows]
        tot = functools.reduce(lambda x, y: x + y, e)
        lb = jnp.zeros_like(m)
        for l in range(1, layer + 1):
            lb = lb + e[l] / tot
        return lb

    head_lanes = [slice(hp * HEAD_DIM, (hp + 1) * HEAD_DIM) for hp in range(HGRN_HP)]
    lbs = [[lower_bound(d, lanes) for d in range(2)] for lanes in head_lanes]

    for hp in range(HGRN_HP):
        for d in range(2):
            st_ref[hp, d] = s0_ref[d, hp].T if latent else jnp.zeros((HEAD_DIM, HEAD_DIM), F32)

    def body(ci, carry):
        rf = pl.ds(pl.multiple_of(ci * C, C), C)
        rb = pl.ds(pl.multiple_of((n_chunks - 1 - ci) * C, C), C)
        for hp, lanes in enumerate(head_lanes):
            o_f, st_f = _hgrn_chunk(_silu(q_ref[rf, lanes]), v_ref[rf, lanes], ff_ref[rf, lanes],
                                    lbs[hp][0], st_ref[hp, 0], False)
            o_b, st_b = _hgrn_chunk(_silu(q_ref[rb, lanes]), v_ref[rb, lanes], fb_ref[rb, lanes],
                                    lbs[hp][1], st_ref[hp, 1], True)
            of_ref[rf, lanes] = o_f
            ob_ref[rb, lanes] = o_b
            st_ref[hp, 0] = st_f
            st_ref[hp, 1] = st_b
        return carry

    lax.fori_loop(0, n_chunks, body, 0)

    for hp, lanes in enumerate(head_lanes):
        if s_ref is not None:
            for d in range(2):
                s_ref[d, hp] = st_ref[hp, d].T
        o = of_ref[:, lanes] + ob_ref[:, lanes]
        o_ref[:, lanes] = (_rms(o, ong_ref[...]) * _silu(g_ref[:, lanes])).astype(o_ref.dtype)


def _hgrn(proj, lb_raw, onorm_g_l, layer, *, state0=None, o_prev=None, s_prev=None):
    latent = state0 is not None
    n_batch, seq, row0 = (DEC_BATCH, DEC_SEQ, N_CTX // DEC_SEQ) if latent else (BATCH, SEQ, 0)
    width = HGRN_HP * HEAD_DIM

    def col(block):
        return pl.BlockSpec((seq, width), lambda b, h: (row0 + b, block // HGRN_HP + h))

    in_specs = [
        pl.BlockSpec((2 * DEPTH, width), lambda b, h: (0, h)),
        col(COL_HQ), col(COL_HI), col(COL_HG), col(COL_HFF), col(COL_HFB),
        pl.BlockSpec((1, HEAD_DIM), lambda b, h: (0, 0)),
    ]
    args = [lb_raw.reshape(2 * DEPTH, HGRN_W), proj, proj, proj, proj, proj,
            onorm_g_l.reshape(1, HEAD_DIM)]
    o_shape = jax.ShapeDtypeStruct((N_ROWS, HGRN_W), BF16)
    o_spec = pl.BlockSpec((seq, width), lambda b, h: (row0 + b, h))
    aliases = {}
    if latent:
        in_specs += [pl.BlockSpec((None, None, 2, HGRN_HP, HEAD_DIM, HEAD_DIM),
                                  lambda b, h: (b, layer, 0, h, 0, 0)), _ANY]
        args += [state0, o_prev]
        aliases = {len(args) - 1: 0}
        out_shape, out_specs = o_shape, o_spec
    else:
        if s_prev is not None:
            in_specs.append(_ANY)
            args.append(s_prev)
            aliases = {len(args) - 1: 1}
        out_shape = (o_shape, jax.ShapeDtypeStruct(
            (BATCH, DEPTH, 2, HGRN_HEADS, HEAD_DIM, HEAD_DIM), F32))
        out_specs = (o_spec, pl.BlockSpec((None, None, 2, HGRN_HP, HEAD_DIM, HEAD_DIM),
                                          lambda b, h: (b, layer, 0, h, 0, 0)))
    return pl.pallas_call(
        functools.partial(_hgrn_kernel, layer=layer, seq=seq, latent=latent, n_alias=len(aliases)),
        out_shape=out_shape,
        grid=(n_batch, HGRN_HEADS // HGRN_HP),
        in_specs=in_specs,
        out_specs=out_specs,
        scratch_shapes=[pltpu.VMEM((seq, width), F32), pltpu.VMEM((seq, width), F32),
                        pltpu.VMEM((HGRN_HP, 2, HEAD_DIM, HEAD_DIM), F32)],
        input_output_aliases=aliases,
        compiler_params=_params("parallel", "parallel"),
        name=f"hgrn_l{layer}_{'lat' if latent else 'ctx'}",
    )(*args)


def _swap_pairs(x, width):
    lanes = x.shape[-1]
    lane = lax.broadcasted_iota(jnp.int32, x.shape, x.ndim - 1)
    from_right = pltpu.roll(x, lanes - width, x.ndim - 1)
    from_left = pltpu.roll(x, width, x.ndim - 1)
    return jnp.where(lane % (2 * width) < width, from_right, from_left)


def _rope(x, cos, sin_signed, quarter):
    return x * cos + _swap_pairs(x, quarter) * sin_signed


def _rms_halves(x, g):
    half = x.shape[-1] // 2
    lane = lax.broadcasted_iota(jnp.int32, x.shape, x.ndim - 1)
    lo = lane < half
    sq = x * x
    ms_lo = jnp.sum(jnp.where(lo, sq, 0.0), axis=-1, keepdims=True) / half
    ms_hi = jnp.sum(jnp.where(lo, 0.0, sq), axis=-1, keepdims=True) / half
    ms = jnp.where(lo, ms_lo, ms_hi)
    return (x * lax.rsqrt(ms + EPS)) * g


def _gqa_kernel(*refs, seq, latent, n_alias):
    qg_ref, kg_ref, q_ref, k_ref, v_ref = refs[:5]
    if latent:
        ck_ref, cv_ref, cos_ref, sin_ref = refs[5:9]
        o_ref, kt_ref, vt_ref = refs[9 + n_alias:]
    else:
        o_ref, kn_ref, vo_ref, kt_ref, vt_ref = refs[5 + n_alias:]
    past = PAST_LEN if latent else 0
    quarter = HEAD_DIM // 4

    @pl.when(pl.program_id(2) == 0)
    def _():
        kn = _rms(k_ref[...], kg_ref[...])
        v = v_ref[...]
        if latent:
            kn = _rope(kn, cos_ref[...], sin_ref[...], quarter)
            kt_ref[0:past, :] = ck_ref[...].astype(BF16)
            vt_ref[0:past, :] = cv_ref[...].astype(BF16)
        else:
            kn_ref[...] = kn
            vo_ref[...] = v
        kt_ref[past:past + seq, :] = kn.astype(BF16)
        vt_ref[past:past + seq, :] = v.astype(BF16)

    q_scale = HEAD_DIM ** -0.5 * LOG2E

    def body(r, carry):
        rows = pl.ds(pl.multiple_of(r * Q_BLOCK, Q_BLOCK), Q_BLOCK)
        qn = _rms(q_ref[rows, :], qg_ref[...])
        if latent:
            qn = _rope(qn, cos_ref[rows, :], sin_ref[rows, :], quarter)
        s = _dot_nt((qn * q_scale).astype(BF16), kt_ref[...])
        p = jnp.exp2(s - jnp.max(s, axis=-1, keepdims=True))
        inv = 1.0 / jnp.sum(p, axis=-1, keepdims=True)
        o_ref[rows, :] = (_dot(p.astype(BF16), vt_ref[...]) * inv).astype(o_ref.dtype)
        return carry

    lax.fori_loop(0, seq // Q_BLOCK, body, 0)


def _gqa(proj, qnorm_g, knorm_g, layer, *, latent_args=None, kv_prev=None):
    latent = latent_args is not None
    n_batch, seq, row0 = (DEC_BATCH, DEC_SEQ, N_CTX // DEC_SEQ) if latent else (BATCH, SEQ, 0)
    past = PAST_LEN if latent else 0

    def vec():
        return pl.BlockSpec((1, HEAD_DIM), lambda b, h, g: (0, 0))

    in_specs = [
        vec(), vec(),
        pl.BlockSpec((seq, HEAD_DIM), lambda b, h, g: (row0 + b, COL_GQ + h * GQA_GROUP + g)),
        pl.BlockSpec((seq, HEAD_DIM), lambda b, h, g: (row0 + b, COL_GK + h)),
        pl.BlockSpec((seq, HEAD_DIM), lambda b, h, g: (row0 + b, COL_GV + h)),
    ]
    args = [qnorm_g.reshape(1, HEAD_DIM), knorm_g.reshape(1, HEAD_DIM), proj, proj, proj]
    o_spec = pl.BlockSpec((seq, HEAD_DIM), lambda b, h, g: (row0 + b, h * GQA_GROUP + g))
    o_shape = jax.ShapeDtypeStruct((N_ROWS, GQA_W), BF16)
    cache_spec = pl.BlockSpec((None, None, SEQ, HEAD_DIM), lambda b, h, g: (b, layer, 0, h))
    aliases = {}
    if latent:
        cache_k, cache_v, cos, sin_signed, o_prev = latent_args
        table_spec = pl.BlockSpec((seq, HEAD_DIM), lambda b, h, g: (0, 0))
        in_specs += [cache_spec, cache_spec, table_spec, table_spec, _ANY]
        args += [cache_k.reshape(DEC_BATCH, DEPTH, PAST_LEN, GQA_KV_W),
                 cache_v.reshape(DEC_BATCH, DEPTH, PAST_LEN, GQA_KV_W), cos, sin_signed, o_prev]
        aliases = {len(args) - 1: 0}
        out_shape, out_specs = o_shape, o_spec
    else:
        if kv_prev is not None:
            in_specs += [_ANY, _ANY]
            args += list(kv_prev)
            aliases = {len(args) - 2: 1, len(args) - 1: 2}
        new_shape = jax.ShapeDtypeStruct((BATCH, DEPTH, SEQ, GQA_KV_W), F32)
        out_shape = (o_shape, new_shape, new_shape)
        out_specs = (o_spec, cache_spec, cache_spec)
    return pl.pallas_call(
        functools.partial(_gqa_kernel, seq=seq, latent=latent, n_alias=len(aliases)),
        out_shape=out_shape,
        grid=(n_batch, GQA_KV_HEADS, GQA_GROUP),
        in_specs=in_specs,
        out_specs=out_specs,
        scratch_shapes=[pltpu.VMEM((past + seq, HEAD_DIM), BF16),
                        pltpu.VMEM((past + seq, HEAD_DIM), BF16)],
        input_output_aliases=aliases,
        compiler_params=_params("parallel", "parallel", "arbitrary"),
        name=f"gqa_l{layer}_{'lat' if latent else 'ctx'}",
    )(*args)


def _diff_kernel(*refs, seq, latent, lam_init, n_alias):
    qg_ref, kg_ref, sg_ref, lam_ref, q_ref, k_ref, v_ref = refs[:7]
    if latent:
        ck_ref, cv_ref, cos_ref, sin_ref = refs[7:11]
        o_ref, kt_ref, vt_ref = refs[11 + n_alias:]
    else:
        o_ref, kn_ref, vo_ref, kt_ref, vt_ref = refs[7 + n_alias:]
    past = PAST_LEN if latent else 0
    quarter = DIFF_QK_DIM // 4

    kn = _rms_halves(k_ref[...], kg_ref[...])
    v = v_ref[...]
    if latent:
        kn = _rope(kn, cos_ref[...], sin_ref[...], quarter)
        kt_ref[0:past, :] = ck_ref[...].astype(BF16)
        vt_ref[0:past, :] = cv_ref[...].astype(BF16)
    else:
        kn_ref[...] = kn
        vo_ref[...] = v
    kt_ref[past:past + seq, :] = kn.astype(BF16)
    vt_ref[past:past + seq, :] = v.astype(BF16)

    lv = lam_ref[...]
    lam = (jnp.exp(jnp.sum(lv[0:1, :] * lv[1:2, :], axis=-1, keepdims=True))
           - jnp.exp(jnp.sum(lv[2:3, :] * lv[3:4, :], axis=-1, keepdims=True)) + lam_init)
    q_scale = DIFF_QK_DIM ** -0.5 * LOG2E

    def body(r, carry):
        rows = pl.ds(pl.multiple_of(r * Q_BLOCK, Q_BLOCK), Q_BLOCK)
        qn = _rms_halves(q_ref[rows, :], qg_ref[...])
        if latent:
            qn = _rope(qn, cos_ref[rows, :], sin_ref[rows, :], quarter)
        qn = qn * q_scale
        lo = lax.broadcasted_iota(jnp.int32, qn.shape, 1) < DIFF_QK_DIM
        kt = kt_ref[...]
        s1 = _dot_nt(jnp.where(lo, qn, 0.0).astype(BF16), kt)
        s2 = _dot_nt(jnp.where(lo, 0.0, qn).astype(BF16), kt)
        p1 = jnp.exp2(s1 - jnp.max(s1, axis=-1, keepdims=True))
        p2 = jnp.exp2(s2 - jnp.max(s2, axis=-1, keepdims=True))
        w1 = 1.0 / jnp.sum(p1, axis=-1, keepdims=True)
        w2 = lam / jnp.sum(p2, axis=-1, keepdims=True)
        a = p1 * w1 - p2 * w2
        o = _dot(a.astype(BF16), vt_ref[...])
        o = _rms(o, sg_ref[...]) * (1.0 - lam_init)
        o_ref[rows, :] = o.astype(o_ref.dtype)
        return carry

    lax.fori_loop(0, seq // Q_BLOCK, body, 0)


def _diff(proj, qnorm_g, knorm_g, subln_g, lam_params, layer, *, latent_args=None, kv_prev=None):
    latent = latent_args is not None
    n_batch, seq, row0 = (DEC_BATCH, DEC_SEQ, N_CTX // DEC_SEQ) if latent else (BATCH, SEQ, 0)
    past = PAST_LEN if latent else 0
    lam_init = 0.8 - 0.6 * math.exp(-0.3 * layer)

    def vec():
        return pl.BlockSpec((1, HEAD_DIM), lambda b, h: (0, 0))

    in_specs = [
        vec(), vec(), vec(),
        pl.BlockSpec((4, DIFF_QK_DIM), lambda b, h: (0, 0)),
        pl.BlockSpec((seq, HEAD_DIM), lambda b, h: (row0 + b, COL_DQ + h)),
        pl.BlockSpec((seq, HEAD_DIM), lambda b, h: (row0 + b, COL_DK + h)),
        pl.BlockSpec((seq, HEAD_DIM), lambda b, h: (row0 + b, COL_DV + h)),
    ]
    args = [jnp.tile(qnorm_g, 2).reshape(1, HEAD_DIM), jnp.tile(knorm_g, 2).reshape(1, HEAD_DIM),
            subln_g.reshape(1, HEAD_DIM), lam_params, proj, proj, proj]
    o_spec = pl.BlockSpec((seq, HEAD_DIM), lambda b, h: (row0 + b, h))
    o_shape = jax.ShapeDtypeStruct((N_ROWS, DIFF_W), BF16)
    cache_spec = pl.BlockSpec((None, None, SEQ, HEAD_DIM), lambda b, h: (b, layer, 0, h))
    aliases = {}
    if latent:
        cache_k, cache_v, cos, sin_signed, o_prev = latent_args
        table_spec = pl.BlockSpec((seq, HEAD_DIM), lambda b, h: (0, 0))
        in_specs += [cache_spec, cache_spec, table_spec, table_spec, _ANY]
        args += [cache_k.reshape(DEC_BATCH, DEPTH, PAST_LEN, DIFF_W),
                 cache_v.reshape(DEC_BATCH, DEPTH, PAST_LEN, DIFF_W), cos, sin_signed, o_prev]
        aliases = {len(args) - 1: 0}
        out_shape, out_specs = o_shape, o_spec
    else:
        if kv_prev is not None:
            in_specs += [_ANY, _ANY]
            args += list(kv_prev)
            aliases = {len(args) - 2: 1, len(args) - 1: 2}
        new_shape = jax.ShapeDtypeStruct((BATCH, DEPTH, SEQ, DIFF_W), F32)
        out_shape = (o_shape, new_shape, new_shape)
        out_specs = (o_spec, cache_spec, cache_spec)
    return pl.pallas_call(
        functools.partial(_diff_kernel, seq=seq, latent=latent, lam_init=lam_init,
                          n_alias=len(aliases)),
        out_shape=out_shape,
        grid=(n_batch, DIFF_HEADS),
        in_specs=in_specs,
        out_specs=out_specs,
        scratch_shapes=[pltpu.VMEM((past + seq, HEAD_DIM), BF16),
                        pltpu.VMEM((past + seq, HEAD_DIM), BF16)],
        input_output_aliases=aliases,
        compiler_params=_params("parallel", "parallel"),
        name=f"diff_l{layer}_{'lat' if latent else 'ctx'}",
    )(*args)


def _rope_tables(n_tokens, dim, repeat):
    quarter = dim // 4
    t = jnp.arange(n_tokens)
    pos = jnp.stack([t // GRID_W, t % GRID_W], axis=-1).astype(F32)
    inv = ROPE_BASE ** (-jnp.arange(quarter, dtype=F32) / quarter)
    ang = pos[:, :, None] * inv
    cos, sin = jnp.cos(ang), jnp.sin(ang)
    cos_l = jnp.concatenate([cos, cos], axis=-1).reshape(n_tokens, dim)
    sin_l = jnp.concatenate([-sin, sin], axis=-1).reshape(n_tokens, dim)
    return jnp.tile(cos_l, (1, repeat)), jnp.tile(sin_l, (1, repeat))


def kernel(x_prompt, x_sample, c, cache_gqa_k, cache_gqa_v, cache_diff_k, cache_diff_v, state_hgrn,
           c_ctx, w_mod, b_mod, norm_g, ffn_w_gate, ffn_w_up, ffn_w_down, w_in, w_out, hgrn_lb_raw,
           hgrn_onorm_g, gqa_qnorm_g, gqa_knorm_g, diff_qnorm_g, diff_knorm_g, diff_lambda,
           diff_subln_g):
    cond = jnp.concatenate(
        [c_ctx[None, :], c, jnp.zeros((COND_PAD - N_COND, D_MODEL), F32)], axis=0)
    mod = _modulation(cond, w_mod, b_mod)

    cos_g, sin_g = _rope_tables(DEC_SEQ, HEAD_DIM, 1)
    cos_d, sin_d = _rope_tables(DEC_SEQ, DIFF_QK_DIM, 2)
    cache_dk = cache_diff_k.reshape(DEC_BATCH, DEPTH, PAST_LEN, DIFF_HEADS, 2 * DIFF_QK_DIM)
    ctx_rows = dict(n_tiles=CTX_TILES, tile0=0)
    lat_rows = dict(n_tiles=LAT_TILES, tile0=CTX_TILES)

    x = None
    states = gqa_kv = diff_kv = None
    for l in range(DEPTH):
        ffn_w = (mod[l], norm_g[l], ffn_w_gate, ffn_w_up, ffn_w_down, l)
        if l == 0:
            x = _ffn(x_prompt.reshape(N_CTX, D_MODEL), *ffn_w, 0, **ctx_rows)
            x = _ffn(x_sample.reshape(N_LAT, D_MODEL), *ffn_w, 0, **lat_rows,
                     out_off=CTX_TILES, o_prev=x)
        else:
            x = _ffn(x, *ffn_w, 0)
        proj = _proj_in(x, mod[l], norm_g[l], w_in, l)

        o_h, states = _hgrn(proj, hgrn_lb_raw, hgrn_onorm_g[l], l, s_prev=states)
        o_h = _hgrn(proj, hgrn_lb_raw, hgrn_onorm_g[l], l, state0=state_hgrn, o_prev=o_h)

        gqa_w = (proj, gqa_qnorm_g[l], gqa_knorm_g[l], l)
        o_g, *gqa_kv = _gqa(*gqa_w, kv_prev=gqa_kv)
        o_g = _gqa(*gqa_w, latent_args=(cache_gqa_k, cache_gqa_v, cos_g, sin_g, o_g))

        diff_w = (proj, diff_qnorm_g[l], diff_knorm_g[l], diff_subln_g[l], diff_lambda[l], l)
        o_d, *diff_kv = _diff(*diff_w, kv_prev=diff_kv)
        o_d = _diff(*diff_w, latent_args=(cache_dk, cache_diff_v, cos_d, sin_d, o_d))

        x = _proj_out(x, mod[l], o_h, o_g, o_d, w_out, l)
        if l < DEPTH - 1:
            x = _ffn(x, *ffn_w, 1)
        else:
            y_prompt = _ffn(x, *ffn_w, 1, **ctx_rows, out_rows=N_CTX)
            y_sample = _ffn(x, *ffn_w, 1, **lat_rows, in_off=CTX_TILES, out_rows=N_LAT)

    return (y_prompt.reshape(BATCH, SEQ, D_MODEL), y_sample.reshape(DEC_BATCH, DEC_SEQ, D_MODEL),
            gqa_kv[0].reshape(BATCH, DEPTH, SEQ, GQA_KV_HEADS, HEAD_DIM),
            gqa_kv[1].reshape(BATCH, DEPTH, SEQ, GQA_KV_HEADS, HEAD_DIM),
            diff_kv[0].reshape(BATCH, DEPTH, SEQ, DIFF_HEADS, 2, DIFF_QK_DIM),
            diff_kv[1].reshape(BATCH, DEPTH, SEQ, DIFF_HEADS, HEAD_DIM),
            states)
```

```python
import functools
import math

import jax
import jax.numpy as jnp
from jax import lax
from jax.experimental import pallas as pl
from jax.experimental.pallas import tpu as pltpu

F32 = jnp.float32
BF16 = jnp.bfloat16

D_MODEL = 2048
BATCH = 16
SEQ = 256
DEPTH = 2
DEC_BATCH = 2
DEC_SEQ = 1024
PAST_LEN = 256
GRID_W = 64
HEAD_DIM = 128
HGRN_HEADS = 4
GQA_Q_HEADS = 6
GQA_KV_HEADS = 2
GQA_GROUP = GQA_Q_HEADS // GQA_KV_HEADS
DIFF_HEADS = 6
DIFF_QK_DIM = 64
FFN_DIM = 5632
N_MOD = 9
IN_WIDTH = 6144
ROPE_BASE = 10000.0
EPS = 1e-6
LOG2E = math.log2(math.e)

N_CTX = BATCH * SEQ
N_LAT = DEC_BATCH * DEC_SEQ
N_ROWS = N_CTX + N_LAT
N_COND = 1 + DEC_BATCH
COND_PAD = 8

COL_HQ, COL_HI, COL_HG, COL_HFF, COL_HFB = 0, 4, 8, 12, 16
COL_GQ, COL_GK, COL_GV = 20, 26, 28
COL_DQ, COL_DK, COL_DV = 30, 36, 42

HGRN_W = HGRN_HEADS * HEAD_DIM
GQA_W = GQA_Q_HEADS * HEAD_DIM
GQA_KV_W = GQA_KV_HEADS * HEAD_DIM
DIFF_W = DIFF_HEADS * HEAD_DIM

VMEM_LIMIT = 60 * 1024 * 1024

TM = 1024
ROW_CHUNK = 128
FFN_TF = 256
FFN_NB = 512
IN_TN = 512
OUT_TN = 512
MOD_TN = 1024
HGRN_CHUNK = 64
HGRN_SUB = 8
HGRN_HP = 4
Q_BLOCK = 256

CTX_TILES = N_CTX // TM
LAT_TILES = N_LAT // TM
ALL_TILES = CTX_TILES + LAT_TILES


def _cond_of_tile(i):
    tiles_per_latent = DEC_SEQ // TM
    return jnp.where(i < CTX_TILES, 0, 1 + (i - CTX_TILES) // tiles_per_latent)


def _silu(x):
    return x * jax.nn.sigmoid(x)


def _dot(a, b):
    return jnp.dot(a, b, preferred_element_type=F32)


def _dot_nt(a, b):
    return lax.dot_general(a, b, (((1,), (1,)), ((), ())), preferred_element_type=F32)


def _dot_tn(a, b):
    return lax.dot_general(a, b, (((0,), (0,)), ((), ())), preferred_element_type=F32)


def _rms(x, g):
    ms = jnp.mean(x * x, axis=-1, keepdims=True)
    return (x * lax.rsqrt(ms + EPS)) * g


def _params(*semantics):
    return pltpu.CompilerParams(dimension_semantics=semantics, vmem_limit_bytes=VMEM_LIMIT)


_ANY = pl.BlockSpec(memory_space=pl.ANY)


def _mod_kernel(cond_ref, w_ref, b_ref, o_ref):
    a = _silu(cond_ref[...]).astype(BF16)
    o_ref[...] = _dot(a, w_ref[...].astype(BF16)) + b_ref[...]


def _modulation(cond, w_mod, b_mod):
    width = N_MOD * D_MODEL
    out = pl.pallas_call(
        _mod_kernel,
        out_shape=jax.ShapeDtypeStruct((DEPTH, COND_PAD, width), F32),
        grid=(DEPTH, width // MOD_TN),
        in_specs=[
            pl.BlockSpec((COND_PAD, D_MODEL), lambda l, j: (0, 0)),
            pl.BlockSpec((None, D_MODEL, MOD_TN), lambda l, j: (l, 0, j)),
            pl.BlockSpec((None, 1, MOD_TN), lambda l, j: (l, 0, j)),
        ],
        out_specs=pl.BlockSpec((None, COND_PAD, MOD_TN), lambda l, j: (l, 0, j)),
        compiler_params=_params("parallel", "parallel"),
        name="modulation",
    )(cond, w_mod, b_mod.reshape(DEPTH, 1, width))
    return out[:, :N_COND].reshape(DEPTH, N_COND, N_MOD, D_MODEL)


def _mod_norm_into(x_ref, mod_ref, g_ref, h_ref, sub):
    shift = mod_ref[3 * sub:3 * sub + 1, :]
    gain = g_ref[sub:sub + 1, :] * (1.0 + mod_ref[3 * sub + 1:3 * sub + 2, :])

    def body(r, carry):
        rows = pl.ds(pl.multiple_of(r * ROW_CHUNK, ROW_CHUNK), ROW_CHUNK)
        h_ref[rows, :] = (_rms(x_ref[rows, :], gain) + shift).astype(BF16)
        return carry

    lax.fori_loop(0, TM // ROW_CHUNK, body, 0)


def _ffn_kernel(x_ref, mod_ref, g_ref, wg_ref, wu_ref, wd_ref, *rest, sub):
    o_ref, h_ref = rest[-2:]
    j = pl.program_id(1)

    @pl.when(j == 0)
    def _():
        _mod_norm_into(x_ref, mod_ref, g_ref, h_ref, sub)
        o_ref[...] = jnp.zeros_like(o_ref)

    h = h_ref[...]
    gate_act = _dot(h, wg_ref[...].astype(BF16))
    up = _dot(h, wu_ref[...].astype(BF16))
    a = (_silu(gate_act) * up).astype(BF16)
    for n in range(0, D_MODEL, FFN_NB):
        o_ref[:, n:n + FFN_NB] += _dot(a, wd_ref[:, n:n + FFN_NB].astype(BF16))

    @pl.when(j == pl.num_programs(1) - 1)
    def _():
        gate = mod_ref[3 * sub + 2:3 * sub + 3, :]

        def body(r, carry):
            rows = pl.ds(pl.multiple_of(r * ROW_CHUNK, ROW_CHUNK), ROW_CHUNK)
            o_ref[rows, :] = x_ref[rows, :] + gate * (0.5 * o_ref[rows, :])
            return carry

        lax.fori_loop(0, TM // ROW_CHUNK, body, 0)


def _ffn(x, mod_l, norm_g_l, w_gate, w_up, w_down, layer, which, *, n_tiles=ALL_TILES, in_off=0,
         out_off=0, tile0=0, out_rows=N_ROWS, o_prev=None):
    sub = 2 * which
    in_specs = [
        pl.BlockSpec((TM, D_MODEL), lambda i, j: (in_off + i, 0)),
        pl.BlockSpec((None, N_MOD, D_MODEL), lambda i, j: (_cond_of_tile(tile0 + i), 0, 0)),
        pl.BlockSpec((3, D_MODEL), lambda i, j: (0, 0)),
        pl.BlockSpec((None, None, D_MODEL, FFN_TF), lambda i, j: (layer, which, 0, j)),
        pl.BlockSpec((None, None, D_MODEL, FFN_TF), lambda i, j: (layer, which, 0, j)),
        pl.BlockSpec((None, None, FFN_TF, D_MODEL), lambda i, j: (layer, which, j, 0)),
    ]
    args = [x, mod_l, norm_g_l, w_gate, w_up, w_down]
    aliases = {}
    if o_prev is not None:
        in_specs.append(_ANY)
        args.append(o_prev)
        aliases = {len(args) - 1: 0}
    return pl.pallas_call(
        functools.partial(_ffn_kernel, sub=sub),
        out_shape=jax.ShapeDtypeStruct((out_rows, D_MODEL), F32),
        grid=(n_tiles, FFN_DIM // FFN_TF),
        in_specs=in_specs,
        out_specs=pl.BlockSpec((TM, D_MODEL), lambda i, j: (out_off + i, 0)),
        scratch_shapes=[pltpu.VMEM((TM, D_MODEL), BF16)],
        input_output_aliases=aliases,
        compiler_params=_params("parallel", "arbitrary"),
        name=f"ffn_l{layer}_h{which}_t{tile0}n{n_tiles}",
    )(*args)


def _proj_in_kernel(x_ref, mod_ref, g_ref, w_ref, o_ref, h_ref):
    @pl.when(pl.program_id(1) == 0)
    def _():
        _mod_norm_into(x_ref, mod_ref, g_ref, h_ref, 1)

    o_ref[...] = _dot(h_ref[...], w_ref[...].astype(BF16))


def _proj_in(x, mod_l, norm_g_l, w_in, layer):
    return pl.pallas_call(
        _proj_in_kernel,
        out_shape=jax.ShapeDtypeStruct((N_ROWS, IN_WIDTH), F32),
        grid=(ALL_TILES, IN_WIDTH // IN_TN),
        in_specs=[
            pl.BlockSpec((TM, D_MODEL), lambda i, j: (i, 0)),
            pl.BlockSpec((None, N_MOD, D_MODEL), lambda i, j: (_cond_of_tile(i), 0, 0)),
            pl.BlockSpec((3, D_MODEL), lambda i, j: (0, 0)),
            pl.BlockSpec((None, D_MODEL, IN_TN), lambda i, j: (layer, 0, j)),
        ],
        out_specs=pl.BlockSpec((TM, IN_TN), lambda i, j: (i, j)),
        scratch_shapes=[pltpu.VMEM((TM, D_MODEL), BF16)],
        compiler_params=_params("parallel", "arbitrary"),
        name=f"proj_in_l{layer}",
    )(x, mod_l, norm_g_l, w_in)


def _proj_out_kernel(x_ref, mod_ref, oh_ref, og_ref, od_ref, w_ref, o_ref):
    m = _dot(oh_ref[...], w_ref[0:HGRN_W, :].astype(BF16))
    m += _dot(og_ref[...], w_ref[HGRN_W:HGRN_W + GQA_W, :].astype(BF16))
    m += _dot(od_ref[...], w_ref[HGRN_W + GQA_W:, :].astype(BF16))
    o_ref[...] = x_ref[...] + mod_ref[5:6, :] * m


def _proj_out(x, mod_l, o_h, o_g, o_d, w_out, layer):
    return pl.pallas_call(
        _proj_out_kernel,
        out_shape=jax.ShapeDtypeStruct((N_ROWS, D_MODEL), F32),
        grid=(ALL_TILES, D_MODEL // OUT_TN),
        in_specs=[
            pl.BlockSpec((TM, OUT_TN), lambda i, j: (i, j)),
            pl.BlockSpec((None, N_MOD, OUT_TN), lambda i, j: (_cond_of_tile(i), 0, j)),
            pl.BlockSpec((TM, HGRN_W), lambda i, j: (i, 0)),
            pl.BlockSpec((TM, GQA_W), lambda i, j: (i, 0)),
            pl.BlockSpec((TM, DIFF_W), lambda i, j: (i, 0)),
            pl.BlockSpec((None, D_MODEL, OUT_TN), lambda i, j: (layer, 0, j)),
        ],
        out_specs=pl.BlockSpec((TM, OUT_TN), lambda i, j: (i, j)),
        compiler_params=_params("parallel", "parallel"),
        name=f"proj_out_l{layer}",
    )(x, mod_l, o_h, o_g, o_d, w_out)


def _log2_forget_and_key(z, lb):
    z2 = z * LOG2E
    soft = jnp.log2(1.0 + jnp.exp2(-jnp.abs(z2)))
    log_1mlb = jnp.log1p(-lb) * LOG2E
    a = jnp.log2(lb)
    c = log_1mlb + (jnp.minimum(z2, 0.0) - soft)
    log_f = jnp.maximum(a, c) + jnp.log2(1.0 + jnp.exp2(-jnp.abs(a - c)))
    log_k = log_1mlb + (jnp.minimum(-z2, 0.0) - soft)
    return log_f, log_k


def _cumsum_rows(x, reverse):
    tile = HGRN_SUB
    row = lax.broadcasted_iota(jnp.int32, (tile, 1), 0)
    tiles = []
    for j in range(x.shape[0] // tile):
        y = x[j * tile:(j + 1) * tile, :]
        for sh in (1, 2, 4):
            if reverse:
                y = y + jnp.where(row < tile - sh, pltpu.roll(y, tile - sh, 0), 0.0)
            else:
                y = y + jnp.where(row >= sh, pltpu.roll(y, sh, 0), 0.0)
        tiles.append(y)
    order = range(len(tiles) - 1, -1, -1) if reverse else range(len(tiles))
    carry = None
    for j in order:
        if carry is not None:
            tiles[j] = tiles[j] + carry
        carry = tiles[j][0:1, :] if reverse else tiles[j][tile - 1:tile, :]
    return jnp.concatenate(tiles, axis=0)


def _hgrn_prepare(q, z, lb, reverse):
    lf2, lk2 = _log2_forget_and_key(z, lb)
    b = _cumsum_rows(lf2, reverse)
    return _silu(q), b, b - lk2


def _hgrn_chunk(qs, b, c, v, st, reverse):
    C, SUB = HGRN_CHUNK, HGRN_SUB
    n_sub = C // SUB
    b_tot = b[0:1, :] if reverse else b[C - 1:C, :]

    o = _dot_nt((qs * jnp.exp2(b)).astype(BF16), st.astype(BF16))

    terms = []
    for i in range(n_sub):
        lo = i * SUB
        qi, bi, ci = qs[lo:lo + SUB, :], b[lo:lo + SUB, :], c[lo:lo + SUB, :]
        for s in range(SUB):
            terms.append(qi * jnp.exp2(bi - ci[s:s + 1, :]))
    k_sums = _dot(jnp.concatenate(terms, axis=0).astype(BF16), jnp.ones((HEAD_DIM, C), BF16))

    row = lax.broadcasted_iota(jnp.int32, (SUB, C), 0)
    lane = lax.broadcasted_iota(jnp.int32, (SUB, C), 1)
    lane_s = lane % SUB
    causal = (lane_s >= row) if reverse else (lane_s <= row)
    blocks = []
    for i in range(n_sub):
        lo, hi = i * SUB, (i + 1) * SUB
        diag = k_sums[lo * SUB:(lo + 1) * SUB, :]
        for s in range(1, SUB):
            diag = jnp.where(lane_s == s, k_sums[(lo + s) * SUB:(lo + s + 1) * SUB, :], diag)
        has_off = (i < n_sub - 1) if reverse else (i > 0)
        if has_off:
            ref = b[hi:hi + 1, :] if reverse else b[lo - 1:lo, :]
            qt = (qs[lo:hi, :] * jnp.exp2(b[lo:hi, :] - ref)).astype(BF16)
            if reverse:
                kt = jnp.concatenate([jnp.zeros((hi, HEAD_DIM), F32), jnp.exp2(ref - c[hi:, :])], axis=0)
            else:
                kt = jnp.concatenate([jnp.exp2(ref - c[:lo, :]), jnp.zeros((C - lo, HEAD_DIM), F32)], axis=0)
            off = _dot_nt(qt, kt.astype(BF16))
        else:
            off = jnp.zeros((SUB, C), F32)
        blocks.append(jnp.where((lane // SUB == i) & causal, diag, off))
    a = jnp.concatenate(blocks, axis=0)
    o = o + _dot(a.astype(BF16), v.astype(BF16))

    st_new = jnp.exp2(b_tot) * st + _dot_tn(v.astype(BF16), jnp.exp2(b_tot - c).astype(BF16))
    return o, st_new


def _hgrn_kernel(*refs, layer, seq, latent, n_alias):
    raw_ref, q_ref, v_ref, g_ref, ff_ref, fb_ref, ong_ref = refs[:7]
    s0_ref = refs[7] if latent else None
    outs = refs[7 + (1 if latent else 0) + n_alias:]
    if latent:
        o_ref, of_ref, ob_ref, st_ref, pre_ref = outs
        s_ref = None
    else:
        o_ref, s_ref, of_ref, ob_ref, st_ref, pre_ref = outs
    C = HGRN_CHUNK
    n_chunks = seq // C

    def lower_bound(d, lanes):
        rows = [raw_ref[2 * l + d:2 * l + d + 1, lanes] for l in range(DEPTH)]
        m = functools.reduce(jnp.maximum, rows)
        e = [jnp.exp(r - m) for r in rows]
        tot = functools.reduce(lambda x, y: x + y, e)
        lb = jnp.zeros_like(m)
        for l in range(1, layer + 1):
            lb = lb + e[l] / tot
        return lb

    head_lanes = [slice(hp * HEAD_DIM, (hp + 1) * HEAD_DIM) for hp in range(HGRN_HP)]
    lbs = [[lower_bound(d, lanes) for d in range(2)] for lanes in head_lanes]

    for hp in range(HGRN_HP):
        for d in range(2):
            st_ref[hp, d] = s0_ref[d, hp].T if latent else jnp.zeros((HEAD_DIM, HEAD_DIM), F32)

    def chunk_rows(ci):
        return (pl.ds(pl.multiple_of(ci * C, C), C),
                pl.ds(pl.multiple_of((n_chunks - 1 - ci) * C, C), C))

    def prepare(ci, slot):
        rows = chunk_rows(ci)
        for hp, lanes in enumerate(head_lanes):
            for d, f_ref in enumerate((ff_ref, fb_ref)):
                pre = _hgrn_prepare(q_ref[rows[d], lanes], f_ref[rows[d], lanes], lbs[hp][d], d == 1)
                for n, val in enumerate(pre):
                    pre_ref[slot, 2 * hp + d, n] = val

    prepare(0, 0)

    def body(ci, carry):
        slot = ci % 2
        rows = chunk_rows(ci)
        for hp, lanes in enumerate(head_lanes):
            for d, acc_ref in enumerate((of_ref, ob_ref)):
                qs, b, c = (pre_ref[slot, 2 * hp + d, n] for n in range(3))
                o, st = _hgrn_chunk(qs, b, c, v_ref[rows[d], lanes], st_ref[hp, d], d == 1)
                acc_ref[rows[d], lanes] = o
                st_ref[hp, d] = st
        prepare(jnp.minimum(ci + 1, n_chunks - 1), 1 - slot)
        return carry

    lax.fori_loop(0, n_chunks, body, 0)

    for hp, lanes in enumerate(head_lanes):
        if s_ref is not None:
            for d in range(2):
                s_ref[d, hp] = st_ref[hp, d].T
        o = of_ref[:, lanes] + ob_ref[:, lanes]
        o_ref[:, lanes] = (_rms(o, ong_ref[...]) * _silu(g_ref[:, lanes])).astype(o_ref.dtype)


def _hgrn(proj, lb_raw, onorm_g_l, layer, *, state0=None, o_prev=None, s_prev=None):
    latent = state0 is not None
    n_batch, seq, row0 = (DEC_BATCH, DEC_SEQ, N_CTX // DEC_SEQ) if latent else (BATCH, SEQ, 0)
    width = HGRN_HP * HEAD_DIM

    def col(block):
        return pl.BlockSpec((seq, width), lambda b, h: (row0 + b, block // HGRN_HP + h))

    in_specs = [
        pl.BlockSpec((2 * DEPTH, width), lambda b, h: (0, h)),
        col(COL_HQ), col(COL_HI), col(COL_HG), col(COL_HFF), col(COL_HFB),
        pl.BlockSpec((1, HEAD_DIM), lambda b, h: (0, 0)),
    ]
    args = [lb_raw.reshape(2 * DEPTH, HGRN_W), proj, proj, proj, proj, proj,
            onorm_g_l.reshape(1, HEAD_DIM)]
    o_shape = jax.ShapeDtypeStruct((N_ROWS, HGRN_W), BF16)
    o_spec = pl.BlockSpec((seq, width), lambda b, h: (row0 + b, h))
    aliases = {}
    if latent:
        in_specs += [pl.BlockSpec((None, None, 2, HGRN_HP, HEAD_DIM, HEAD_DIM),
                                  lambda b, h: (b, layer, 0, h, 0, 0)), _ANY]
        args += [state0, o_prev]
        aliases = {len(args) - 1: 0}
        out_shape, out_specs = o_shape, o_spec
    else:
        if s_prev is not None:
            in_specs.append(_ANY)
            args.append(s_prev)
            aliases = {len(args) - 1: 1}
        out_shape = (o_shape, jax.ShapeDtypeStruct(
            (BATCH, DEPTH, 2, HGRN_HEADS, HEAD_DIM, HEAD_DIM), F32))
        out_specs = (o_spec, pl.BlockSpec((None, None, 2, HGRN_HP, HEAD_DIM, HEAD_DIM),
                                          lambda b, h: (b, layer, 0, h, 0, 0)))
    return pl.pallas_call(
        functools.partial(_hgrn_kernel, layer=layer, seq=seq, latent=latent, n_alias=len(aliases)),
        out_shape=out_shape,
        grid=(n_batch, HGRN_HEADS // HGRN_HP),
        in_specs=in_specs,
        out_specs=out_specs,
        scratch_shapes=[pltpu.VMEM((seq, width), F32), pltpu.VMEM((seq, width), F32),
                        pltpu.VMEM((HGRN_HP, 2, HEAD_DIM, HEAD_DIM), F32),
                        pltpu.VMEM((2, 2 * HGRN_HP, 3, HGRN_CHUNK, HEAD_DIM), F32)],
        input_output_aliases=aliases,
        compiler_params=_params("parallel", "parallel"),
        name=f"hgrn_l{layer}_{'lat' if latent else 'ctx'}",
    )(*args)


def _swap_pairs(x, width):
    lanes = x.shape[-1]
    lane = lax.broadcasted_iota(jnp.int32, x.shape, x.ndim - 1)
    from_right = pltpu.roll(x, lanes - width, x.ndim - 1)
    from_left = pltpu.roll(x, width, x.ndim - 1)
    return jnp.where(lane % (2 * width) < width, from_right, from_left)


def _rope(x, cos, sin_signed, quarter):
    return x * cos + _swap_pairs(x, quarter) * sin_signed


def _rms_halves(x, g):
    half = x.shape[-1] // 2
    lane = lax.broadcasted_iota(jnp.int32, x.shape, x.ndim - 1)
    lo = lane < half
    sq = x * x
    ms_lo = jnp.sum(jnp.where(lo, sq, 0.0), axis=-1, keepdims=True) / half
    ms_hi = jnp.sum(jnp.where(lo, 0.0, sq), axis=-1, keepdims=True) / half
    ms = jnp.where(lo, ms_lo, ms_hi)
    return (x * lax.rsqrt(ms + EPS)) * g


def _gqa_kernel(*refs, seq, latent, n_alias):
    qg_ref, kg_ref, q_ref, k_ref, v_ref = refs[:5]
    if latent:
        ck_ref, cv_ref, cos_ref, sin_ref = refs[5:9]
        o_ref, kt_ref, vt_ref = refs[9 + n_alias:]
    else:
        o_ref, kn_ref, vo_ref, kt_ref, vt_ref = refs[5 + n_alias:]
    past = PAST_LEN if latent else 0
    quarter = HEAD_DIM // 4

    @pl.when(pl.program_id(2) == 0)
    def _():
        kn = _rms(k_ref[...], kg_ref[...])
        v = v_ref[...]
        if latent:
            kn = _rope(kn, cos_ref[...], sin_ref[...], quarter)
            kt_ref[0:past, :] = ck_ref[...].astype(BF16)
            vt_ref[0:past, :] = cv_ref[...].astype(BF16)
        else:
            kn_ref[...] = kn
            vo_ref[...] = v
        kt_ref[past:past + seq, :] = kn.astype(BF16)
        vt_ref[past:past + seq, :] = v.astype(BF16)

    q_scale = HEAD_DIM ** -0.5 * LOG2E

    def body(r, carry):
        rows = pl.ds(pl.multiple_of(r * Q_BLOCK, Q_BLOCK), Q_BLOCK)
        qn = _rms(q_ref[rows, :], qg_ref[...])
        if latent:
            qn = _rope(qn, cos_ref[rows, :], sin_ref[rows, :], quarter)
        s = _dot_nt((qn * q_scale).astype(BF16), kt_ref[...])
        p = jnp.exp2(s - jnp.max(s, axis=-1, keepdims=True))
        inv = 1.0 / jnp.sum(p, axis=-1, keepdims=True)
        o_ref[rows, :] = (_dot(p.astype(BF16), vt_ref[...]) * inv).astype(o_ref.dtype)
        return carry

    lax.fori_loop(0, seq // Q_BLOCK, body, 0, unroll=True)


def _gqa(proj, qnorm_g, knorm_g, layer, *, latent_args=None, kv_prev=None):
    latent = latent_args is not None
    n_batch, seq, row0 = (DEC_BATCH, DEC_SEQ, N_CTX // DEC_SEQ) if latent else (BATCH, SEQ, 0)
    past = PAST_LEN if latent else 0

    def vec():
        return pl.BlockSpec((1, HEAD_DIM), lambda b, h, g: (0, 0))

    in_specs = [
        vec(), vec(),
        pl.BlockSpec((seq, HEAD_DIM), lambda b, h, g: (row0 + b, COL_GQ + h * GQA_GROUP + g)),
        pl.BlockSpec((seq, HEAD_DIM), lambda b, h, g: (row0 + b, COL_GK + h)),
        pl.BlockSpec((seq, HEAD_DIM), lambda b, h, g: (row0 + b, COL_GV + h)),
    ]
    args = [qnorm_g.reshape(1, HEAD_DIM), knorm_g.reshape(1, HEAD_DIM), proj, proj, proj]
    o_spec = pl.BlockSpec((seq, HEAD_DIM), lambda b, h, g: (row0 + b, h * GQA_GROUP + g))
    o_shape = jax.ShapeDtypeStruct((N_ROWS, GQA_W), BF16)
    cache_spec = pl.BlockSpec((None, None, SEQ, HEAD_DIM), lambda b, h, g: (b, layer, 0, h))
    aliases = {}
    if latent:
        cache_k, cache_v, cos, sin_signed, o_prev = latent_args
        table_spec = pl.BlockSpec((seq, HEAD_DIM), lambda b, h, g: (0, 0))
        in_specs += [cache_spec, cache_spec, table_spec, table_spec, _ANY]
        args += [cache_k.reshape(DEC_BATCH, DEPTH, PAST_LEN, GQA_KV_W),
                 cache_v.reshape(DEC_BATCH, DEPTH, PAST_LEN, GQA_KV_W), cos, sin_signed, o_prev]
        aliases = {len(args) - 1: 0}
        out_shape, out_specs = o_shape, o_spec
    else:
        if kv_prev is not None:
            in_specs += [_ANY, _ANY]
            args += list(kv_prev)
            aliases = {len(args) - 2: 1, len(args) - 1: 2}
        new_shape = jax.ShapeDtypeStruct((BATCH, DEPTH, SEQ, GQA_KV_W), F32)
        out_shape = (o_shape, new_shape, new_shape)
        out_specs = (o_spec, cache_spec, cache_spec)
    return pl.pallas_call(
        functools.partial(_gqa_kernel, seq=seq, latent=latent, n_alias=len(aliases)),
        out_shape=out_shape,
        grid=(n_batch, GQA_KV_HEADS, GQA_GROUP),
        in_specs=in_specs,
        out_specs=out_specs,
        scratch_shapes=[pltpu.VMEM((past + seq, HEAD_DIM), BF16),
                        pltpu.VMEM((past + seq, HEAD_DIM), BF16)],
        input_output_aliases=aliases,
        compiler_params=_params("parallel", "parallel", "arbitrary"),
        name=f"gqa_l{layer}_{'lat' if latent else 'ctx'}",
    )(*args)


def _diff_kernel(*refs, seq, latent, lam_init, n_alias):
    qg_ref, kg_ref, sg_ref, lam_ref, q_ref, k_ref, v_ref = refs[:7]
    if latent:
        ck_ref, cv_ref, cos_ref, sin_ref = refs[7:11]
        o_ref, kt_ref, vt_ref = refs[11 + n_alias:]
    else:
        o_ref, kn_ref, vo_ref, kt_ref, vt_ref = refs[7 + n_alias:]
    past = PAST_LEN if latent else 0
    quarter = DIFF_QK_DIM // 4

    kn = _rms_halves(k_ref[...], kg_ref[...])
    v = v_ref[...]
    if latent:
        kn = _rope(kn, cos_ref[...], sin_ref[...], quarter)
        kt_ref[0:past, :] = ck_ref[...].astype(BF16)
        vt_ref[0:past, :] = cv_ref[...].astype(BF16)
    else:
        kn_ref[...] = kn
        vo_ref[...] = v
    kt_ref[past:past + seq, :] = kn.astype(BF16)
    vt_ref[past:past + seq, :] = v.astype(BF16)

    lv = lam_ref[...]
    lam = (jnp.exp(jnp.sum(lv[0:1, :] * lv[1:2, :], axis=-1, keepdims=True))
           - jnp.exp(jnp.sum(lv[2:3, :] * lv[3:4, :], axis=-1, keepdims=True)) + lam_init)
    q_scale = DIFF_QK_DIM ** -0.5 * LOG2E

    def body(r, carry):
        rows = pl.ds(pl.multiple_of(r * Q_BLOCK, Q_BLOCK), Q_BLOCK)
        qn = _rms_halves(q_ref[rows, :], qg_ref[...])
        if latent:
            qn = _rope(qn, cos_ref[rows, :], sin_ref[rows, :], quarter)
        qn = qn * q_scale
        lo = lax.broadcasted_iota(jnp.int32, qn.shape, 1) < DIFF_QK_DIM
        kt = kt_ref[...]
        s1 = _dot_nt(jnp.where(lo, qn, 0.0).astype(BF16), kt)
        s2 = _dot_nt(jnp.where(lo, 0.0, qn).astype(BF16), kt)
        p1 = jnp.exp2(s1 - jnp.max(s1, axis=-1, keepdims=True))
        p2 = jnp.exp2(s2 - jnp.max(s2, axis=-1, keepdims=True))
        w1 = 1.0 / jnp.sum(p1, axis=-1, keepdims=True)
        w2 = lam / jnp.sum(p2, axis=-1, keepdims=True)
        a = p1 * w1 - p2 * w2
        o = _dot(a.astype(BF16), vt_ref[...])
        o = _rms(o, sg_ref[...]) * (1.0 - lam_init)
        o_ref[rows, :] = o.astype(o_ref.dtype)
        return carry

    lax.fori_loop(0, seq // Q_BLOCK, body, 0, unroll=True)


def _diff(proj, qnorm_g, knorm_g, subln_g, lam_params, layer, *, latent_args=None, kv_prev=None):
    latent = latent_args is not None
    n_batch, seq, row0 = (DEC_BATCH, DEC_SEQ, N_CTX // DEC_SEQ) if latent else (BATCH, SEQ, 0)
    past = PAST_LEN if latent else 0
    lam_init = 0.8 - 0.6 * math.exp(-0.3 * layer)

    def vec():
        return pl.BlockSpec((1, HEAD_DIM), lambda b, h: (0, 0))

    in_specs = [
        vec(), vec(), vec(),
        pl.BlockSpec((4, DIFF_QK_DIM), lambda b, h: (0, 0)),
        pl.BlockSpec((seq, HEAD_DIM), lambda b, h: (row0 + b, COL_DQ + h)),
        pl.BlockSpec((seq, HEAD_DIM), lambda b, h: (row0 + b, COL_DK + h)),
        pl.BlockSpec((seq, HEAD_DIM), lambda b, h: (row0 + b, COL_DV + h)),
    ]
    args = [jnp.tile(qnorm_g, 2).reshape(1, HEAD_DIM), jnp.tile(knorm_g, 2).reshape(1, HEAD_DIM),
            subln_g.reshape(1, HEAD_DIM), lam_params, proj, proj, proj]
    o_spec = pl.BlockSpec((seq, HEAD_DIM), lambda b, h: (row0 + b, h))
    o_shape = jax.ShapeDtypeStruct((N_ROWS, DIFF_W), BF16)
    cache_spec = pl.BlockSpec((None, None, SEQ, HEAD_DIM), lambda b, h: (b, layer, 0, h))
    aliases = {}
    if latent:
        cache_k, cache_v, cos, sin_signed, o_prev = latent_args
        table_spec = pl.BlockSpec((seq, HEAD_DIM), lambda b, h: (0, 0))
        in_specs += [cache_spec, cache_spec, table_spec, table_spec, _ANY]
        args += [cache_k.reshape(DEC_BATCH, DEPTH, PAST_LEN, DIFF_W),
                 cache_v.reshape(DEC_BATCH, DEPTH, PAST_LEN, DIFF_W), cos, sin_signed, o_prev]
        aliases = {len(args) - 1: 0}
        out_shape, out_specs = o_shape, o_spec
    else:
        if kv_prev is not None:
            in_specs += [_ANY, _ANY]
            args += list(kv_prev)
            aliases = {len(args) - 2: 1, len(args) - 1: 2}
        new_shape = jax.ShapeDtypeStruct((BATCH, DEPTH, SEQ, DIFF_W), F32)
        out_shape = (o_shape, new_shape, new_shape)
        out_specs = (o_spec, cache_spec, cache_spec)
    return pl.pallas_call(
        functools.partial(_diff_kernel, seq=seq, latent=latent, lam_init=lam_init,
                          n_alias=len(aliases)),
        out_shape=out_shape,
        grid=(n_batch, DIFF_HEADS),
        in_specs=in_specs,
        out_specs=out_specs,
        scratch_shapes=[pltpu.VMEM((past + seq, HEAD_DIM), BF16),
                        pltpu.VMEM((past + seq, HEAD_DIM), BF16)],
        input_output_aliases=aliases,
        compiler_params=_params("parallel", "parallel"),
        name=f"diff_l{layer}_{'lat' if latent else 'ctx'}",
    )(*args)


def _rope_tables(n_tokens, dim, repeat):
    quarter = dim // 4
    t = jnp.arange(n_tokens)
    pos = jnp.stack([t // GRID_W, t % GRID_W], axis=-1).astype(F32)
    inv = ROPE_BASE ** (-jnp.arange(quarter, dtype=F32) / quarter)
    ang = pos[:, :, None] * inv
    cos, sin = jnp.cos(ang), jnp.sin(ang)
    cos_l = jnp.concatenate([cos, cos], axis=-1).reshape(n_tokens, dim)
    sin_l = jnp.concatenate([-sin, sin], axis=-1).reshape(n_tokens, dim)
    return jnp.tile(cos_l, (1, repeat)), jnp.tile(sin_l, (1, repeat))


def kernel(x_prompt, x_sample, c, cache_gqa_k, cache_gqa_v, cache_diff_k, cache_diff_v, state_hgrn,
           c_ctx, w_mod, b_mod, norm_g, ffn_w_gate, ffn_w_up, ffn_w_down, w_in, w_out, hgrn_lb_raw,
           hgrn_onorm_g, gqa_qnorm_g, gqa_knorm_g, diff_qnorm_g, diff_knorm_g, diff_lambda,
           diff_subln_g):
    cond = jnp.concatenate(
        [c_ctx[None, :], c, jnp.zeros((COND_PAD - N_COND, D_MODEL), F32)], axis=0)
    mod = _modulation(cond, w_mod, b_mod)

    cos_g, sin_g = _rope_tables(DEC_SEQ, HEAD_DIM, 1)
    cos_d, sin_d = _rope_tables(DEC_SEQ, DIFF_QK_DIM, 2)
    cache_dk = cache_diff_k.reshape(DEC_BATCH, DEPTH, PAST_LEN, DIFF_HEADS, 2 * DIFF_QK_DIM)
    ctx_rows = dict(n_tiles=CTX_TILES, tile0=0)
    lat_rows = dict(n_tiles=LAT_TILES, tile0=CTX_TILES)

    x = None
    states = gqa_kv = diff_kv = None
    for l in range(DEPTH):
        ffn_w = (mod[l], norm_g[l], ffn_w_gate, ffn_w_up, ffn_w_down, l)
        if l == 0:
            x = _ffn(x_prompt.reshape(N_CTX, D_MODEL), *ffn_w, 0, **ctx_rows)
            x = _ffn(x_sample.reshape(N_LAT, D_MODEL), *ffn_w, 0, **lat_rows,
                     out_off=CTX_TILES, o_prev=x)
        else:
            x = _ffn(x, *ffn_w, 0)
        proj = _proj_in(x, mod[l], norm_g[l], w_in, l)

        o_h, states = _hgrn(proj, hgrn_lb_raw, hgrn_onorm_g[l], l, s_prev=states)
        o_h = _hgrn(proj, hgrn_lb_raw, hgrn_onorm_g[l], l, state0=state_hgrn, o_prev=o_h)

        gqa_w = (proj, gqa_qnorm_g[l], gqa_knorm_g[l], l)
        o_g, *gqa_kv = _gqa(*gqa_w, kv_prev=gqa_kv)
        o_g = _gqa(*gqa_w, latent_args=(cache_gqa_k, cache_gqa_v, cos_g, sin_g, o_g))

        diff_w = (proj, diff_qnorm_g[l], diff_knorm_g[l], diff_subln_g[l], diff_lambda[l], l)
        o_d, *diff_kv = _diff(*diff_w, kv_prev=diff_kv)
        o_d = _diff(*diff_w, latent_args=(cache_dk, cache_diff_v, cos_d, sin_d, o_d))

        x = _proj_out(x, mod[l], o_h, o_g, o_d, w_out, l)
        if l < DEPTH - 1:
            x = _ffn(x, *ffn_w, 1)
        else:
            y_prompt = _ffn(x, *ffn_w, 1, **ctx_rows, out_rows=N_CTX)
            y_sample = _ffn(x, *ffn_w, 1, **lat_rows, in_off=CTX_TILES, out_rows=N_LAT)

    return (y_prompt.reshape(BATCH, SEQ, D_MODEL), y_sample.reshape(DEC_BATCH, DEC_SEQ, D_MODEL),
            gqa_kv[0].reshape(BATCH, DEPTH, SEQ, GQA_KV_HEADS, HEAD_DIM),
            gqa_kv[1].reshape(BATCH, DEPTH, SEQ, GQA_KV_HEADS, HEAD_DIM),
            diff_kv[0].reshape(BATCH, DEPTH, SEQ, DIFF_HEADS, 2, DIFF_QK_DIM),
            diff_kv[1].reshape(BATCH, DEPTH, SEQ, DIFF_HEADS, HEAD_DIM),
            states)
```

```python
import functools
import math

import jax
import jax.numpy as jnp
from jax import lax
from jax.experimental import pallas as pl
from jax.experimental.pallas import tpu as pltpu

F32 = jnp.float32
BF16 = jnp.bfloat16

D_MODEL = 2048
BATCH = 16
SEQ = 256
DEPTH = 2
DEC_BATCH = 2
DEC_SEQ = 1024
PAST_LEN = 256
GRID_W = 64
HEAD_DIM = 128
HGRN_HEADS = 4
GQA_Q_HEADS = 6
GQA_KV_HEADS = 2
GQA_GROUP = GQA_Q_HEADS // GQA_KV_HEADS
DIFF_HEADS = 6
DIFF_QK_DIM = 64
FFN_DIM = 5632
N_MOD = 9
IN_WIDTH = 6144
ROPE_BASE = 10000.0
EPS = 1e-6
LOG2E = math.log2(math.e)

N_CTX = BATCH * SEQ
N_LAT = DEC_BATCH * DEC_SEQ
N_ROWS = N_CTX + N_LAT
N_COND = 1 + DEC_BATCH
COND_PAD = 8

COL_HQ, COL_HI, COL_HG, COL_HFF, COL_HFB = 0, 4, 8, 12, 16
COL_GQ, COL_GK, COL_GV = 20, 26, 28
COL_DQ, COL_DK, COL_DV = 30, 36, 42

HGRN_W = HGRN_HEADS * HEAD_DIM
GQA_W = GQA_Q_HEADS * HEAD_DIM
GQA_KV_W = GQA_KV_HEADS * HEAD_DIM
DIFF_W = DIFF_HEADS * HEAD_DIM

VMEM_LIMIT = 60 * 1024 * 1024

TM = 1024
ROW_CHUNK = 128
FFN_TF = 256
FFN_NB = 512
IN_TN = 512
IN_TILES_PER_STEP = 2
OUT_TN = 512
MOD_TN = 1024
HGRN_CHUNK = 64
HGRN_SUB = 8
HGRN_HP = 4
Q_BLOCK = 512
GQA_Q_BLOCK = 256
DIFF_HP_CTX = 6
DIFF_HP_LAT = 2

CTX_TILES = N_CTX // TM
LAT_TILES = N_LAT // TM
ALL_TILES = CTX_TILES + LAT_TILES


def _cond_of_tile(i):
    tiles_per_latent = DEC_SEQ // TM
    return jnp.where(i < CTX_TILES, 0, 1 + (i - CTX_TILES) // tiles_per_latent)


def _silu(x):
    return x * jax.nn.sigmoid(x)


def _dot(a, b):
    return jnp.dot(a, b, preferred_element_type=F32)


def _dot_nt(a, b):
    return lax.dot_general(a, b, (((1,), (1,)), ((), ())), preferred_element_type=F32)


def _dot_tn(a, b):
    return lax.dot_general(a, b, (((0,), (0,)), ((), ())), preferred_element_type=F32)


def _rms(x, g):
    ms = jnp.mean(x * x, axis=-1, keepdims=True)
    return (x * lax.rsqrt(ms + EPS)) * g


def _params(*semantics):
    return pltpu.CompilerParams(dimension_semantics=semantics, vmem_limit_bytes=VMEM_LIMIT)


_ANY = pl.BlockSpec(memory_space=pl.ANY)


def _mod_kernel(cond_ref, w_ref, b_ref, o_ref):
    a = _silu(cond_ref[...]).astype(BF16)
    o_ref[...] = _dot(a, w_ref[...].astype(BF16)) + b_ref[...]


def _modulation(cond, w_mod, b_mod):
    width = N_MOD * D_MODEL
    out = pl.pallas_call(
        _mod_kernel,
        out_shape=jax.ShapeDtypeStruct((DEPTH, COND_PAD, width), F32),
        grid=(DEPTH, width // MOD_TN),
        in_specs=[
            pl.BlockSpec((COND_PAD, D_MODEL), lambda l, j: (0, 0)),
            pl.BlockSpec((None, D_MODEL, MOD_TN), lambda l, j: (l, 0, j)),
            pl.BlockSpec((None, 1, MOD_TN), lambda l, j: (l, 0, j)),
        ],
        out_specs=pl.BlockSpec((None, COND_PAD, MOD_TN), lambda l, j: (l, 0, j)),
        compiler_params=_params("parallel", "parallel"),
        name="modulation",
    )(cond, w_mod, b_mod.reshape(DEPTH, 1, width))
    return out[:, :N_COND].reshape(DEPTH, N_COND, N_MOD, D_MODEL)


def _mod_norm_into(x_ref, mod_ref, g_ref, h_ref, sub, row0=0):
    shift = mod_ref[3 * sub:3 * sub + 1, :]
    gain = g_ref[sub:sub + 1, :] * (1.0 + mod_ref[3 * sub + 1:3 * sub + 2, :])

    def body(r, carry):
        rows = pl.ds(pl.multiple_of(row0 + r * ROW_CHUNK, ROW_CHUNK), ROW_CHUNK)
        h_ref[rows, :] = (_rms(x_ref[rows, :], gain) + shift).astype(BF16)
        return carry

    lax.fori_loop(0, TM // ROW_CHUNK, body, 0)


def _ffn_kernel(x_ref, mod_ref, g_ref, wg_ref, wu_ref, wd_ref, *rest, sub):
    o_ref, h_ref = rest[-2:]
    j = pl.program_id(1)

    @pl.when(j == 0)
    def _():
        _mod_norm_into(x_ref, mod_ref, g_ref, h_ref, sub)
        o_ref[...] = jnp.zeros_like(o_ref)

    h = h_ref[...]
    gate_act = _dot(h, wg_ref[...].astype(BF16))
    up = _dot(h, wu_ref[...].astype(BF16))
    a = (_silu(gate_act) * up).astype(BF16)
    for n in range(0, D_MODEL, FFN_NB):
        o_ref[:, n:n + FFN_NB] += _dot(a, wd_ref[:, n:n + FFN_NB].astype(BF16))

    @pl.when(j == pl.num_programs(1) - 1)
    def _():
        gate = mod_ref[3 * sub + 2:3 * sub + 3, :]

        def body(r, carry):
            rows = pl.ds(pl.multiple_of(r * ROW_CHUNK, ROW_CHUNK), ROW_CHUNK)
            o_ref[rows, :] = x_ref[rows, :] + gate * (0.5 * o_ref[rows, :])
            return carry

        lax.fori_loop(0, TM // ROW_CHUNK, body, 0)


def _ffn(x, mod_l, norm_g_l, w_gate, w_up, w_down, layer, which, *, n_tiles=ALL_TILES, in_off=0,
         out_off=0, tile0=0, out_rows=N_ROWS, o_prev=None):
    sub = 2 * which
    in_specs = [
        pl.BlockSpec((TM, D_MODEL), lambda i, j: (in_off + i, 0)),
        pl.BlockSpec((None, N_MOD, D_MODEL), lambda i, j: (_cond_of_tile(tile0 + i), 0, 0)),
        pl.BlockSpec((3, D_MODEL), lambda i, j: (0, 0)),
        pl.BlockSpec((None, None, D_MODEL, FFN_TF), lambda i, j: (layer, which, 0, j)),
        pl.BlockSpec((None, None, D_MODEL, FFN_TF), lambda i, j: (layer, which, 0, j)),
        pl.BlockSpec((None, None, FFN_TF, D_MODEL), lambda i, j: (layer, which, j, 0)),
    ]
    args = [x, mod_l, norm_g_l, w_gate, w_up, w_down]
    aliases = {}
    if o_prev is not None:
        in_specs.append(_ANY)
        args.append(o_prev)
        aliases = {len(args) - 1: 0}
    return pl.pallas_call(
        functools.partial(_ffn_kernel, sub=sub),
        out_shape=jax.ShapeDtypeStruct((out_rows, D_MODEL), F32),
        grid=(n_tiles, FFN_DIM // FFN_TF),
        in_specs=in_specs,
        out_specs=pl.BlockSpec((TM, D_MODEL), lambda i, j: (out_off + i, 0)),
        scratch_shapes=[pltpu.VMEM((TM, D_MODEL), BF16)],
        input_output_aliases=aliases,
        compiler_params=_params("parallel", "arbitrary"),
        name=f"ffn_l{layer}_h{which}_t{tile0}n{n_tiles}",
    )(*args)


def _proj_in_kernel(x_ref, mod_ref, g_ref, w_ref, o_ref, h_ref):
    @pl.when(pl.program_id(1) == 0)
    def _():
        for part in range(IN_TILES_PER_STEP):
            cond = _cond_of_tile(pl.program_id(0) * IN_TILES_PER_STEP + part)
            _mod_norm_into(x_ref, mod_ref.at[cond], g_ref, h_ref, 1, row0=part * TM)

    o_ref[...] = _dot(h_ref[...], w_ref[...].astype(BF16))


def _proj_in(x, mod_l, norm_g_l, w_in, layer):
    rows = IN_TILES_PER_STEP * TM
    return pl.pallas_call(
        _proj_in_kernel,
        out_shape=jax.ShapeDtypeStruct((N_ROWS, IN_WIDTH), F32),
        grid=(N_ROWS // rows, IN_WIDTH // IN_TN),
        in_specs=[
            pl.BlockSpec((rows, D_MODEL), lambda i, j: (i, 0), pipeline_mode=pl.Buffered(1)),
            pl.BlockSpec((N_COND, N_MOD, D_MODEL), lambda i, j: (0, 0, 0)),
            pl.BlockSpec((3, D_MODEL), lambda i, j: (0, 0)),
            pl.BlockSpec((None, D_MODEL, IN_TN), lambda i, j: (layer, 0, j)),
        ],
        out_specs=pl.BlockSpec((rows, IN_TN), lambda i, j: (i, j)),
        scratch_shapes=[pltpu.VMEM((rows, D_MODEL), BF16)],
        compiler_params=_params("parallel", "arbitrary"),
        name=f"proj_in_l{layer}",
    )(x, mod_l, norm_g_l, w_in)


def _proj_out_kernel(x_ref, mod_ref, oh_ref, og_ref, od_ref, w_ref, o_ref):
    m = _dot(oh_ref[...], w_ref[0:HGRN_W, :].astype(BF16))
    m += _dot(og_ref[...], w_ref[HGRN_W:HGRN_W + GQA_W, :].astype(BF16))
    m += _dot(od_ref[...], w_ref[HGRN_W + GQA_W:, :].astype(BF16))
    o_ref[...] = x_ref[...] + mod_ref[5:6, :] * m


def _proj_out(x, mod_l, o_h, o_g, o_d, w_out, layer):
    return pl.pallas_call(
        _proj_out_kernel,
        out_shape=jax.ShapeDtypeStruct((N_ROWS, D_MODEL), F32),
        grid=(ALL_TILES, D_MODEL // OUT_TN),
        in_specs=[
            pl.BlockSpec((TM, OUT_TN), lambda i, j: (i, j)),
            pl.BlockSpec((None, N_MOD, OUT_TN), lambda i, j: (_cond_of_tile(i), 0, j)),
            pl.BlockSpec((TM, HGRN_W), lambda i, j: (i, 0)),
            pl.BlockSpec((TM, GQA_W), lambda i, j: (i, 0)),
            pl.BlockSpec((TM, DIFF_W), lambda i, j: (i, 0)),
            pl.BlockSpec((None, D_MODEL, OUT_TN), lambda i, j: (layer, 0, j)),
        ],
        out_specs=pl.BlockSpec((TM, OUT_TN), lambda i, j: (i, j)),
        compiler_params=_params("parallel", "parallel"),
        name=f"proj_out_l{layer}",
    )(x, mod_l, o_h, o_g, o_d, w_out)


def _log2_forget_and_key(z, lb):
    z2 = z * LOG2E
    soft = jnp.log2(1.0 + jnp.exp2(-jnp.abs(z2)))
    log_1mlb = jnp.log1p(-lb) * LOG2E
    a = jnp.log2(lb)
    c = log_1mlb + (jnp.minimum(z2, 0.0) - soft)
    log_f = jnp.maximum(a, c) + jnp.log2(1.0 + jnp.exp2(-jnp.abs(a - c)))
    log_k = log_1mlb + (jnp.minimum(-z2, 0.0) - soft)
    return log_f, log_k


def _cumsum_rows(x, reverse):
    tile = HGRN_SUB
    row = lax.broadcasted_iota(jnp.int32, (tile, 1), 0)
    tiles = []
    for j in range(x.shape[0] // tile):
        y = x[j * tile:(j + 1) * tile, :]
        for sh in (1, 2, 4):
            if reverse:
                y = y + jnp.where(row < tile - sh, pltpu.roll(y, tile - sh, 0), 0.0)
            else:
                y = y + jnp.where(row >= sh, pltpu.roll(y, sh, 0), 0.0)
        tiles.append(y)
    order = range(len(tiles) - 1, -1, -1) if reverse else range(len(tiles))
    carry = None
    for j in order:
        if carry is not None:
            tiles[j] = tiles[j] + carry
        carry = tiles[j][0:1, :] if reverse else tiles[j][tile - 1:tile, :]
    return jnp.concatenate(tiles, axis=0)


def _hgrn_prepare(q, z, lb, reverse):
    lf2, lk2 = _log2_forget_and_key(z, lb)
    b = _cumsum_rows(lf2, reverse)
    return _silu(q), b, b - lk2


def _hgrn_chunk(qs, b, c, v, st, reverse):
    C, SUB = HGRN_CHUNK, HGRN_SUB
    n_sub = C // SUB
    b_tot = b[0:1, :] if reverse else b[C - 1:C, :]

    o = _dot_nt((qs * jnp.exp2(b)).astype(BF16), st.astype(BF16))

    terms = []
    for i in range(n_sub):
        lo = i * SUB
        qi, bi, ci = qs[lo:lo + SUB, :], b[lo:lo + SUB, :], c[lo:lo + SUB, :]
        for s in range(SUB):
            terms.append(qi * jnp.exp2(bi - ci[s:s + 1, :]))
    k_sums = _dot(jnp.concatenate(terms, axis=0).astype(BF16), jnp.ones((HEAD_DIM, C), BF16))

    row = lax.broadcasted_iota(jnp.int32, (SUB, C), 0)
    lane = lax.broadcasted_iota(jnp.int32, (SUB, C), 1)
    lane_s = lane % SUB
    causal = (lane_s >= row) if reverse else (lane_s <= row)
    blocks = []
    for i in range(n_sub):
        lo, hi = i * SUB, (i + 1) * SUB
        diag = k_sums[lo * SUB:(lo + 1) * SUB, :]
        for s in range(1, SUB):
            diag = jnp.where(lane_s == s, k_sums[(lo + s) * SUB:(lo + s + 1) * SUB, :], diag)
        has_off = (i < n_sub - 1) if reverse else (i > 0)
        if has_off:
            ref = b[hi:hi + 1, :] if reverse else b[lo - 1:lo, :]
            qt = (qs[lo:hi, :] * jnp.exp2(b[lo:hi, :] - ref)).astype(BF16)
            if reverse:
                kt = jnp.concatenate([jnp.zeros((hi, HEAD_DIM), F32), jnp.exp2(ref - c[hi:, :])], axis=0)
            else:
                kt = jnp.concatenate([jnp.exp2(ref - c[:lo, :]), jnp.zeros((C - lo, HEAD_DIM), F32)], axis=0)
            off = _dot_nt(qt, kt.astype(BF16))
        else:
            off = jnp.zeros((SUB, C), F32)
        blocks.append(jnp.where((lane // SUB == i) & causal, diag, off))
    a = jnp.concatenate(blocks, axis=0)
    o = o + _dot(a.astype(BF16), v.astype(BF16))

    st_new = jnp.exp2(b_tot) * st + _dot_tn(v.astype(BF16), jnp.exp2(b_tot - c).astype(BF16))
    return o, st_new


def _hgrn_kernel(*refs, layer, seq, latent, n_alias):
    raw_ref, q_ref, v_ref, g_ref, ff_ref, fb_ref, ong_ref = refs[:7]
    s0_ref = refs[7] if latent else None
    outs = refs[7 + (1 if latent else 0) + n_alias:]
    if latent:
        o_ref, of_ref, ob_ref, st_ref, pre_ref = outs
        s_ref = None
    else:
        o_ref, s_ref, of_ref, ob_ref, st_ref, pre_ref = outs
    C = HGRN_CHUNK
    n_chunks = seq // C

    def lower_bound(d, lanes):
        rows = [raw_ref[2 * l + d:2 * l + d + 1, lanes] for l in range(DEPTH)]
        m = functools.reduce(jnp.maximum, rows)
        e = [jnp.exp(r - m) for r in rows]
        tot = functools.reduce(lambda x, y: x + y, e)
        lb = jnp.zeros_like(m)
        for l in range(1, layer + 1):
            lb = lb + e[l] / tot
        return lb

    head_lanes = [slice(hp * HEAD_DIM, (hp + 1) * HEAD_DIM) for hp in range(HGRN_HP)]
    lbs = [[lower_bound(d, lanes) for d in range(2)] for lanes in head_lanes]

    for hp in range(HGRN_HP):
        for d in range(2):
            st_ref[hp, d] = s0_ref[d, hp].T if latent else jnp.zeros((HEAD_DIM, HEAD_DIM), F32)

    def chunk_rows(ci):
        return (pl.ds(pl.multiple_of(ci * C, C), C),
                pl.ds(pl.multiple_of((n_chunks - 1 - ci) * C, C), C))

    def prepare(ci, slot):
        rows = chunk_rows(ci)
        for hp, lanes in enumerate(head_lanes):
            for d, f_ref in enumerate((ff_ref, fb_ref)):
                pre = _hgrn_prepare(q_ref[rows[d], lanes], f_ref[rows[d], lanes], lbs[hp][d], d == 1)
                for n, val in enumerate(pre):
                    pre_ref[slot, 2 * hp + d, n] = val

    prepare(0, 0)

    def body(ci, carry):
        slot = ci % 2
        rows = chunk_rows(ci)
        for hp, lanes in enumerate(head_lanes):
            for d, acc_ref in enumerate((of_ref, ob_ref)):
                qs, b, c = (pre_ref[slot, 2 * hp + d, n] for n in range(3))
                o, st = _hgrn_chunk(qs, b, c, v_ref[rows[d], lanes], st_ref[hp, d], d == 1)
                acc_ref[rows[d], lanes] = o
                st_ref[hp, d] = st
        prepare(jnp.minimum(ci + 1, n_chunks - 1), 1 - slot)
        return carry

    lax.fori_loop(0, n_chunks, body, 0)

    for hp, lanes in enumerate(head_lanes):
        if s_ref is not None:
            for d in range(2):
                s_ref[d, hp] = st_ref[hp, d].T
        o = of_ref[:, lanes] + ob_ref[:, lanes]
        o_ref[:, lanes] = (_rms(o, ong_ref[...]) * _silu(g_ref[:, lanes])).astype(o_ref.dtype)


def _hgrn(proj, lb_raw, onorm_g_l, layer, *, state0=None, o_prev=None, s_prev=None):
    latent = state0 is not None
    n_batch, seq, row0 = (DEC_BATCH, DEC_SEQ, N_CTX // DEC_SEQ) if latent else (BATCH, SEQ, 0)
    width = HGRN_HP * HEAD_DIM

    def col(block):
        return pl.BlockSpec((seq, width), lambda b, h: (row0 + b, block // HGRN_HP + h))

    in_specs = [
        pl.BlockSpec((2 * DEPTH, width), lambda b, h: (0, h)),
        col(COL_HQ), col(COL_HI), col(COL_HG), col(COL_HFF), col(COL_HFB),
        pl.BlockSpec((1, HEAD_DIM), lambda b, h: (0, 0)),
    ]
    args = [lb_raw.reshape(2 * DEPTH, HGRN_W), proj, proj, proj, proj, proj,
            onorm_g_l.reshape(1, HEAD_DIM)]
    o_shape = jax.ShapeDtypeStruct((N_ROWS, HGRN_W), BF16)
    o_spec = pl.BlockSpec((seq, width), lambda b, h: (row0 + b, h))
    aliases = {}
    if latent:
        in_specs += [pl.BlockSpec((None, None, 2, HGRN_HP, HEAD_DIM, HEAD_DIM),
                                  lambda b, h: (b, layer, 0, h, 0, 0)), _ANY]
        args += [state0, o_prev]
        aliases = {len(args) - 1: 0}
        out_shape, out_specs = o_shape, o_spec
    else:
        if s_prev is not None:
            in_specs.append(_ANY)
            args.append(s_prev)
            aliases = {len(args) - 1: 1}
        out_shape = (o_shape, jax.ShapeDtypeStruct(
            (BATCH, DEPTH, 2, HGRN_HEADS, HEAD_DIM, HEAD_DIM), F32))
        out_specs = (o_spec, pl.BlockSpec((None, None, 2, HGRN_HP, HEAD_DIM, HEAD_DIM),
                                          lambda b, h: (b, layer, 0, h, 0, 0)))
    return pl.pallas_call(
        functools.partial(_hgrn_kernel, layer=layer, seq=seq, latent=latent, n_alias=len(aliases)),
        out_shape=out_shape,
        grid=(n_batch, HGRN_HEADS // HGRN_HP),
        in_specs=in_specs,
        out_specs=out_specs,
        scratch_shapes=[pltpu.VMEM((seq, width), F32), pltpu.VMEM((seq, width), F32),
                        pltpu.VMEM((HGRN_HP, 2, HEAD_DIM, HEAD_DIM), F32),
                        pltpu.VMEM((2, 2 * HGRN_HP, 3, HGRN_CHUNK, HEAD_DIM), F32)],
        input_output_aliases=aliases,
        compiler_params=_params("parallel", "parallel"),
        name=f"hgrn_l{layer}_{'lat' if latent else 'ctx'}",
    )(*args)


def _swap_pairs(x, width):
    lanes = x.shape[-1]
    lane = lax.broadcasted_iota(jnp.int32, x.shape, x.ndim - 1)
    from_right = pltpu.roll(x, lanes - width, x.ndim - 1)
    from_left = pltpu.roll(x, width, x.ndim - 1)
    return jnp.where(lane % (2 * width) < width, from_right, from_left)


def _rope(x, cos, sin_signed, quarter):
    return x * cos + _swap_pairs(x, quarter) * sin_signed


def _rms_halves(x, g):
    half = x.shape[-1] // 2
    lane = lax.broadcasted_iota(jnp.int32, x.shape, x.ndim - 1)
    lo = lane < half
    sq = x * x
    ms_lo = jnp.sum(jnp.where(lo, sq, 0.0), axis=-1, keepdims=True) / half
    ms_hi = jnp.sum(jnp.where(lo, 0.0, sq), axis=-1, keepdims=True) / half
    ms = jnp.where(lo, ms_lo, ms_hi)
    return (x * lax.rsqrt(ms + EPS)) * g


def _gqa_kernel(*refs, seq, latent, n_alias):
    qg_ref, kg_ref = refs[:2]
    q_refs = refs[2:2 + GQA_GROUP]
    k_ref, v_ref = refs[2 + GQA_GROUP:4 + GQA_GROUP]
    rest = refs[4 + GQA_GROUP:]
    if latent:
        ck_ref, cv_ref, cos_ref, sin_ref = rest[:4]
        o_ref, kt_ref, vt_ref = rest[4 + n_alias:]
    else:
        o_ref, kn_ref, vo_ref, kt_ref, vt_ref = rest[n_alias:]
    past = PAST_LEN if latent else 0
    quarter = HEAD_DIM // 4

    kn = _rms(k_ref[...], kg_ref[...])
    v = v_ref[...]
    if latent:
        kn = _rope(kn, cos_ref[...], sin_ref[...], quarter)
        kt_ref[0:past, :] = ck_ref[...].astype(BF16)
        vt_ref[0:past, :] = cv_ref[...].astype(BF16)
    else:
        kn_ref[...] = kn
        vo_ref[...] = v
    kt_ref[past:past + seq, :] = kn.astype(BF16)
    vt_ref[past:past + seq, :] = v.astype(BF16)

    q_scale = HEAD_DIM ** -0.5 * LOG2E

    def attend(q_rows):
        s = _dot_nt(jnp.concatenate(q_rows, axis=0), kt_ref[...])
        p = jnp.exp2(s - jnp.max(s, axis=-1, keepdims=True))
        inv = 1.0 / jnp.sum(p, axis=-1, keepdims=True)
        o = (_dot(p.astype(BF16), vt_ref[...]) * inv).astype(o_ref.dtype)
        n = q_rows[0].shape[0]
        return [o[i * n:(i + 1) * n, :] for i in range(len(q_rows))]

    def query(q_ref, rows):
        qn = _rms(q_ref[rows, :], qg_ref[...])
        if latent:
            qn = _rope(qn, cos_ref[rows, :], sin_ref[rows, :], quarter)
        return (qn * q_scale).astype(BF16)

    head_lanes = [slice(g * HEAD_DIM, (g + 1) * HEAD_DIM) for g in range(GQA_GROUP)]
    if seq <= GQA_Q_BLOCK:
        rows = slice(0, seq)
        outs = attend([query(q_ref, rows) for q_ref in q_refs])
        for lanes, o in zip(head_lanes, outs):
            o_ref[rows, lanes] = o
    else:
        for lanes, q_ref in zip(head_lanes, q_refs):
            for r in range(seq // GQA_Q_BLOCK):
                rows = slice(r * GQA_Q_BLOCK, (r + 1) * GQA_Q_BLOCK)
                o_ref[rows, lanes] = attend([query(q_ref, rows)])[0]


def _gqa(proj, qnorm_g, knorm_g, layer, *, latent_args=None, kv_prev=None):
    latent = latent_args is not None
    n_batch, seq, row0 = (DEC_BATCH, DEC_SEQ, N_CTX // DEC_SEQ) if latent else (BATCH, SEQ, 0)
    past = PAST_LEN if latent else 0

    def vec():
        return pl.BlockSpec((1, HEAD_DIM), lambda b, h: (0, 0))

    def q_spec(g):
        return pl.BlockSpec((seq, HEAD_DIM), lambda b, h: (row0 + b, COL_GQ + h * GQA_GROUP + g))

    in_specs = [vec(), vec()] + [q_spec(g) for g in range(GQA_GROUP)] + [
        pl.BlockSpec((seq, HEAD_DIM), lambda b, h: (row0 + b, COL_GK + h)),
        pl.BlockSpec((seq, HEAD_DIM), lambda b, h: (row0 + b, COL_GV + h)),
    ]
    args = ([qnorm_g.reshape(1, HEAD_DIM), knorm_g.reshape(1, HEAD_DIM)]
            + [proj] * (GQA_GROUP + 2))
    o_spec = pl.BlockSpec((seq, GQA_GROUP * HEAD_DIM), lambda b, h: (row0 + b, h))
    o_shape = jax.ShapeDtypeStruct((N_ROWS, GQA_W), BF16)
    cache_spec = pl.BlockSpec((None, None, SEQ, HEAD_DIM), lambda b, h: (b, layer, 0, h))
    aliases = {}
    if latent:
        cache_k, cache_v, cos, sin_signed, o_prev = latent_args
        table_spec = pl.BlockSpec((seq, HEAD_DIM), lambda b, h: (0, 0))
        in_specs += [cache_spec, cache_spec, table_spec, table_spec, _ANY]
        args += [cache_k.reshape(DEC_BATCH, DEPTH, PAST_LEN, GQA_KV_W),
                 cache_v.reshape(DEC_BATCH, DEPTH, PAST_LEN, GQA_KV_W), cos, sin_signed, o_prev]
        aliases = {len(args) - 1: 0}
        out_shape, out_specs = o_shape, o_spec
    else:
        if kv_prev is not None:
            in_specs += [_ANY, _ANY]
            args += list(kv_prev)
            aliases = {len(args) - 2: 1, len(args) - 1: 2}
        new_shape = jax.ShapeDtypeStruct((BATCH, DEPTH, SEQ, GQA_KV_W), F32)
        out_shape = (o_shape, new_shape, new_shape)
        out_specs = (o_spec, cache_spec, cache_spec)
    return pl.pallas_call(
        functools.partial(_gqa_kernel, seq=seq, latent=latent, n_alias=len(aliases)),
        out_shape=out_shape,
        grid=(n_batch, GQA_KV_HEADS),
        in_specs=in_specs,
        out_specs=out_specs,
        scratch_shapes=[pltpu.VMEM((past + seq, HEAD_DIM), BF16),
                        pltpu.VMEM((past + seq, HEAD_DIM), BF16)],
        input_output_aliases=aliases,
        compiler_params=_params("parallel", "parallel"),
        name=f"gqa_l{layer}_{'lat' if latent else 'ctx'}",
    )(*args)


def _diff_kernel(*refs, seq, latent, lam_init, n_alias, heads):
    qg_ref, kg_ref, sg_ref, lam_ref, q_ref, k_ref, v_ref = refs[:7]
    if latent:
        ck_ref, cv_ref, cos_ref, sin_ref = refs[7:11]
        o_ref, kt_ref, vt_ref = refs[11 + n_alias:]
    else:
        o_ref, kn_ref, vo_ref, kt_ref, vt_ref = refs[7 + n_alias:]
    past = PAST_LEN if latent else 0
    quarter = DIFF_QK_DIM // 4
    head_lanes = [slice(hd * HEAD_DIM, (hd + 1) * HEAD_DIM) for hd in range(heads)]

    for lanes in head_lanes:
        kn = _rms_halves(k_ref[:, lanes], kg_ref[...])
        v = v_ref[:, lanes]
        if latent:
            kn = _rope(kn, cos_ref[...], sin_ref[...], quarter)
            kt_ref[0:past, lanes] = ck_ref[:, lanes].astype(BF16)
            vt_ref[0:past, lanes] = cv_ref[:, lanes].astype(BF16)
        else:
            kn_ref[:, lanes] = kn
            vo_ref[:, lanes] = v
        kt_ref[past:past + seq, lanes] = kn.astype(BF16)
        vt_ref[past:past + seq, lanes] = v.astype(BF16)

    lv = lam_ref[...]
    lam = (jnp.exp(jnp.sum(lv[0:1, :] * lv[1:2, :], axis=-1, keepdims=True))
           - jnp.exp(jnp.sum(lv[2:3, :] * lv[3:4, :], axis=-1, keepdims=True)) + lam_init)
    q_scale = DIFF_QK_DIM ** -0.5 * LOG2E

    q_block = min(seq, Q_BLOCK)
    for lanes in head_lanes:
        for r in range(seq // q_block):
            rows = slice(r * q_block, (r + 1) * q_block)
            qn = _rms_halves(q_ref[rows, lanes], qg_ref[...])
            if latent:
                qn = _rope(qn, cos_ref[rows, :], sin_ref[rows, :], quarter)
            qn = qn * q_scale
            lo = lax.broadcasted_iota(jnp.int32, qn.shape, 1) < DIFF_QK_DIM
            kt = kt_ref[:, lanes]
            s1 = _dot_nt(jnp.where(lo, qn, 0.0).astype(BF16), kt)
            s2 = _dot_nt(jnp.where(lo, 0.0, qn).astype(BF16), kt)
            p1 = jnp.exp2(s1 - jnp.max(s1, axis=-1, keepdims=True))
            p2 = jnp.exp2(s2 - jnp.max(s2, axis=-1, keepdims=True))
            w1 = 1.0 / jnp.sum(p1, axis=-1, keepdims=True)
            w2 = lam / jnp.sum(p2, axis=-1, keepdims=True)
            a = p1 * w1 - p2 * w2
            o = _dot(a.astype(BF16), vt_ref[:, lanes])
            o = _rms(o, sg_ref[...]) * (1.0 - lam_init)
            o_ref[rows, lanes] = o.astype(o_ref.dtype)


def _diff(proj, qnorm_g, knorm_g, subln_g, lam_params, layer, *, latent_args=None, kv_prev=None):
    latent = latent_args is not None
    n_batch, seq, row0 = (DEC_BATCH, DEC_SEQ, N_CTX // DEC_SEQ) if latent else (BATCH, SEQ, 0)
    past = PAST_LEN if latent else 0
    lam_init = 0.8 - 0.6 * math.exp(-0.3 * layer)
    heads = DIFF_HP_LAT if latent else DIFF_HP_CTX
    width = heads * HEAD_DIM

    def vec():
        return pl.BlockSpec((1, HEAD_DIM), lambda b, h: (0, 0))

    def col(block):
        return pl.BlockSpec((seq, width), lambda b, h: (row0 + b, block // heads + h))

    in_specs = [
        vec(), vec(), vec(),
        pl.BlockSpec((4, DIFF_QK_DIM), lambda b, h: (0, 0)),
        col(COL_DQ), col(COL_DK), col(COL_DV),
    ]
    args = [jnp.tile(qnorm_g, 2).reshape(1, HEAD_DIM), jnp.tile(knorm_g, 2).reshape(1, HEAD_DIM),
            subln_g.reshape(1, HEAD_DIM), lam_params, proj, proj, proj]
    o_spec = pl.BlockSpec((seq, width), lambda b, h: (row0 + b, h))
    o_shape = jax.ShapeDtypeStruct((N_ROWS, DIFF_W), BF16)
    cache_spec = pl.BlockSpec((None, None, SEQ, width), lambda b, h: (b, layer, 0, h))
    aliases = {}
    if latent:
        cache_k, cache_v, cos, sin_signed, o_prev = latent_args
        table_spec = pl.BlockSpec((seq, HEAD_DIM), lambda b, h: (0, 0))
        in_specs += [cache_spec, cache_spec, table_spec, table_spec, _ANY]
        args += [cache_k.reshape(DEC_BATCH, DEPTH, PAST_LEN, DIFF_W),
                 cache_v.reshape(DEC_BATCH, DEPTH, PAST_LEN, DIFF_W), cos, sin_signed, o_prev]
        aliases = {len(args) - 1: 0}
        out_shape, out_specs = o_shape, o_spec
    else:
        if kv_prev is not None:
            in_specs += [_ANY, _ANY]
            args += list(kv_prev)
            aliases = {len(args) - 2: 1, len(args) - 1: 2}
        new_shape = jax.ShapeDtypeStruct((BATCH, DEPTH, SEQ, DIFF_W), F32)
        out_shape = (o_shape, new_shape, new_shape)
        out_specs = (o_spec, cache_spec, cache_spec)
    return pl.pallas_call(
        functools.partial(_diff_kernel, seq=seq, latent=latent, lam_init=lam_init,
                          n_alias=len(aliases), heads=heads),
        out_shape=out_shape,
        grid=(n_batch, DIFF_HEADS // heads),
        in_specs=in_specs,
        out_specs=out_specs,
        scratch_shapes=[pltpu.VMEM((past + seq, width), BF16),
                        pltpu.VMEM((past + seq, width), BF16)],
        input_output_aliases=aliases,
        compiler_params=_params("parallel", "parallel"),
        name=f"diff_l{layer}_{'lat' if latent else 'ctx'}",
    )(*args)


def _rope_tables(n_tokens, dim, repeat):
    quarter = dim // 4
    t = jnp.arange(n_tokens)
    pos = jnp.stack([t // GRID_W, t % GRID_W], axis=-1).astype(F32)
    inv = ROPE_BASE ** (-jnp.arange(quarter, dtype=F32) / quarter)
    ang = pos[:, :, None] * inv
    cos, sin = jnp.cos(ang), jnp.sin(ang)
    cos_l = jnp.concatenate([cos, cos], axis=-1).reshape(n_tokens, dim)
    sin_l = jnp.concatenate([-sin, sin], axis=-1).reshape(n_tokens, dim)
    return jnp.tile(cos_l, (1, repeat)), jnp.tile(sin_l, (1, repeat))


def kernel(x_prompt, x_sample, c, cache_gqa_k, cache_gqa_v, cache_diff_k, cache_diff_v, state_hgrn,
           c_ctx, w_mod, b_mod, norm_g, ffn_w_gate, ffn_w_up, ffn_w_down, w_in, w_out, hgrn_lb_raw,
           hgrn_onorm_g, gqa_qnorm_g, gqa_knorm_g, diff_qnorm_g, diff_knorm_g, diff_lambda,
           diff_subln_g):
    cond = jnp.concatenate(
        [c_ctx[None, :], c, jnp.zeros((COND_PAD - N_COND, D_MODEL), F32)], axis=0)
    mod = _modulation(cond, w_mod, b_mod)

    cos_g, sin_g = _rope_tables(DEC_SEQ, HEAD_DIM, 1)
    cos_d, sin_d = _rope_tables(DEC_SEQ, DIFF_QK_DIM, 2)
    cache_dk = cache_diff_k.reshape(DEC_BATCH, DEPTH, PAST_LEN, DIFF_HEADS, 2 * DIFF_QK_DIM)
    ctx_rows = dict(n_tiles=CTX_TILES, tile0=0)
    lat_rows = dict(n_tiles=LAT_TILES, tile0=CTX_TILES)

    x = None
    states = gqa_kv = diff_kv = None
    for l in range(DEPTH):
        ffn_w = (mod[l], norm_g[l], ffn_w_gate, ffn_w_up, ffn_w_down, l)
        if l == 0:
            x = _ffn(x_prompt.reshape(N_CTX, D_MODEL), *ffn_w, 0, **ctx_rows)
            x = _ffn(x_sample.reshape(N_LAT, D_MODEL), *ffn_w, 0, **lat_rows,
                     out_off=CTX_TILES, o_prev=x)
        else:
            x = _ffn(x, *ffn_w, 0)
        proj = _proj_in(x, mod[l], norm_g[l], w_in, l)

        o_h, states = _hgrn(proj, hgrn_lb_raw, hgrn_onorm_g[l], l, s_prev=states)
        o_h = _hgrn(proj, hgrn_lb_raw, hgrn_onorm_g[l], l, state0=state_hgrn, o_prev=o_h)

        gqa_w = (proj, gqa_qnorm_g[l], gqa_knorm_g[l], l)
        o_g, *gqa_kv = _gqa(*gqa_w, kv_prev=gqa_kv)
        o_g = _gqa(*gqa_w, latent_args=(cache_gqa_k, cache_gqa_v, cos_g, sin_g, o_g))

        diff_w = (proj, diff_qnorm_g[l], diff_knorm_g[l], diff_subln_g[l], diff_lambda[l], l)
        o_d, *diff_kv = _diff(*diff_w, kv_prev=diff_kv)
        o_d = _diff(*diff_w, latent_args=(cache_dk, cache_diff_v, cos_d, sin_d, o_d))

        x = _proj_out(x, mod[l], o_h, o_g, o_d, w_out, l)
        if l < DEPTH - 1:
            x = _ffn(x, *ffn_w, 1)
        else:
            y_prompt = _ffn(x, *ffn_w, 1, **ctx_rows, out_rows=N_CTX)
            y_sample = _ffn(x, *ffn_w, 1, **lat_rows, in_off=CTX_TILES, out_rows=N_LAT)

    return (y_prompt.reshape(BATCH, SEQ, D_MODEL), y_sample.reshape(DEC_BATCH, DEC_SEQ, D_MODEL),
            gqa_kv[0].reshape(BATCH, DEPTH, SEQ, GQA_KV_HEADS, HEAD_DIM),
            gqa_kv[1].reshape(BATCH, DEPTH, SEQ, GQA_KV_HEADS, HEAD_DIM),
            diff_kv[0].reshape(BATCH, DEPTH, SEQ, DIFF_HEADS, 2, DIFF_QK_DIM),
            diff_kv[1].reshape(BATCH, DEPTH, SEQ, DIFF_HEADS, HEAD_DIM),
            states)
```

```python
import functools
import math

import jax
import jax.numpy as jnp
from jax import lax
from jax.experimental import pallas as pl
from jax.experimental.pallas import tpu as pltpu

F32 = jnp.float32
BF16 = jnp.bfloat16

D_MODEL = 2048
BATCH = 16
SEQ = 256
DEPTH = 2
DEC_BATCH = 2
DEC_SEQ = 1024
PAST_LEN = 256
GRID_W = 64
HEAD_DIM = 128
HGRN_HEADS = 4
GQA_Q_HEADS = 6
GQA_KV_HEADS = 2
GQA_GROUP = GQA_Q_HEADS // GQA_KV_HEADS
DIFF_HEADS = 6
DIFF_QK_DIM = 64
FFN_DIM = 5632
N_MOD = 9
IN_WIDTH = 6144
ROPE_BASE = 10000.0
EPS = 1e-6
LOG2E = math.log2(math.e)

N_CTX = BATCH * SEQ
N_LAT = DEC_BATCH * DEC_SEQ
N_ROWS = N_CTX + N_LAT
N_COND = 1 + DEC_BATCH
COND_PAD = 8

COL_HQ, COL_HI, COL_HG, COL_HFF, COL_HFB = 0, 4, 8, 12, 16
COL_GQ, COL_GK, COL_GV = 20, 26, 28
COL_DQ, COL_DK, COL_DV = 30, 36, 42

HGRN_W = HGRN_HEADS * HEAD_DIM
GQA_W = GQA_Q_HEADS * HEAD_DIM
GQA_KV_W = GQA_KV_HEADS * HEAD_DIM
DIFF_W = DIFF_HEADS * HEAD_DIM

VMEM_LIMIT = 60 * 1024 * 1024

TM = 1024
ROW_CHUNK = 128
FFN_TF = 512
FFN_NB = 512
IN_TN = 512
IN_TILES_PER_STEP = 2
OUT_TN = 512
OUT_TILES_PER_STEP = 2
MOD_TN = 1024
HGRN_CHUNK = 64
HGRN_SUB = 8
HGRN_HP = 4
Q_BLOCK = 512
GQA_Q_BLOCK = 256
DIFF_HP_CTX = 6
DIFF_HP_LAT = 2

CTX_TILES = N_CTX // TM
LAT_TILES = N_LAT // TM
ALL_TILES = CTX_TILES + LAT_TILES


def _cond_of_tile(i):
    tiles_per_latent = DEC_SEQ // TM
    return jnp.where(i < CTX_TILES, 0, 1 + (i - CTX_TILES) // tiles_per_latent)


def _silu(x):
    return x * jax.nn.sigmoid(x)


def _dot(a, b):
    return jnp.dot(a, b, preferred_element_type=F32)


def _dot_nt(a, b):
    return lax.dot_general(a, b, (((1,), (1,)), ((), ())), preferred_element_type=F32)


def _dot_tn(a, b):
    return lax.dot_general(a, b, (((0,), (0,)), ((), ())), preferred_element_type=F32)


def _rms(x, g):
    ms = jnp.mean(x * x, axis=-1, keepdims=True)
    return (x * lax.rsqrt(ms + EPS)) * g


def _params(*semantics):
    return pltpu.CompilerParams(dimension_semantics=semantics, vmem_limit_bytes=VMEM_LIMIT)


_ANY = pl.BlockSpec(memory_space=pl.ANY)


def _mod_kernel(cond_ref, w_ref, b_ref, o_ref):
    a = _silu(cond_ref[...]).astype(BF16)
    o_ref[...] = _dot(a, w_ref[...].astype(BF16)) + b_ref[...]


def _modulation(cond, w_mod, b_mod):
    width = N_MOD * D_MODEL
    out = pl.pallas_call(
        _mod_kernel,
        out_shape=jax.ShapeDtypeStruct((DEPTH, COND_PAD, width), F32),
        grid=(DEPTH, width // MOD_TN),
        in_specs=[
            pl.BlockSpec((COND_PAD, D_MODEL), lambda l, j: (0, 0)),
            pl.BlockSpec((None, D_MODEL, MOD_TN), lambda l, j: (l, 0, j)),
            pl.BlockSpec((None, 1, MOD_TN), lambda l, j: (l, 0, j)),
        ],
        out_specs=pl.BlockSpec((None, COND_PAD, MOD_TN), lambda l, j: (l, 0, j)),
        compiler_params=_params("parallel", "parallel"),
        name="modulation",
    )(cond, w_mod, b_mod.reshape(DEPTH, 1, width))
    return out[:, :N_COND].reshape(DEPTH, N_COND, N_MOD, D_MODEL)


def _mod_norm_into(x_ref, mod_ref, g_ref, h_ref, sub, row0=0):
    shift = mod_ref[3 * sub:3 * sub + 1, :]
    gain = g_ref[sub:sub + 1, :] * (1.0 + mod_ref[3 * sub + 1:3 * sub + 2, :])

    def body(r, carry):
        rows = pl.ds(pl.multiple_of(row0 + r * ROW_CHUNK, ROW_CHUNK), ROW_CHUNK)
        h_ref[rows, :] = (_rms(x_ref[rows, :], gain) + shift).astype(BF16)
        return carry

    lax.fori_loop(0, TM // ROW_CHUNK, body, 0)


def _ffn_kernel(x_ref, mod_ref, g_ref, wg_ref, wu_ref, wd_ref, *rest, sub):
    o_ref, h_ref = rest[-2:]
    j = pl.program_id(1)

    @pl.when(j == 0)
    def _():
        _mod_norm_into(x_ref, mod_ref, g_ref, h_ref, sub)
        o_ref[...] = jnp.zeros_like(o_ref)

    h = h_ref[...]
    gate_act = _dot(h, wg_ref[...].astype(BF16))
    up = _dot(h, wu_ref[...].astype(BF16))
    a = (_silu(gate_act) * up).astype(BF16)
    for n in range(0, D_MODEL, FFN_NB):
        o_ref[:, n:n + FFN_NB] += _dot(a, wd_ref[:, n:n + FFN_NB].astype(BF16))

    @pl.when(j == pl.num_programs(1) - 1)
    def _():
        gate = mod_ref[3 * sub + 2:3 * sub + 3, :]

        def body(r, carry):
            rows = pl.ds(pl.multiple_of(r * ROW_CHUNK, ROW_CHUNK), ROW_CHUNK)
            o_ref[rows, :] = x_ref[rows, :] + gate * (0.5 * o_ref[rows, :])
            return carry

        lax.fori_loop(0, TM // ROW_CHUNK, body, 0)


def _ffn(x, mod_l, norm_g_l, w_gate, w_up, w_down, layer, which, *, n_tiles=ALL_TILES, in_off=0,
         out_off=0, tile0=0, out_rows=N_ROWS, o_prev=None):
    sub = 2 * which
    in_specs = [
        pl.BlockSpec((TM, D_MODEL), lambda i, j: (in_off + i, 0), pipeline_mode=pl.Buffered(1)),
        pl.BlockSpec((None, N_MOD, D_MODEL), lambda i, j: (_cond_of_tile(tile0 + i), 0, 0)),
        pl.BlockSpec((3, D_MODEL), lambda i, j: (0, 0)),
        pl.BlockSpec((None, None, D_MODEL, FFN_TF), lambda i, j: (layer, which, 0, j)),
        pl.BlockSpec((None, None, D_MODEL, FFN_TF), lambda i, j: (layer, which, 0, j)),
        pl.BlockSpec((None, None, FFN_TF, D_MODEL), lambda i, j: (layer, which, j, 0)),
    ]
    args = [x, mod_l, norm_g_l, w_gate, w_up, w_down]
    aliases = {}
    if o_prev is not None:
        in_specs.append(_ANY)
        args.append(o_prev)
        aliases = {len(args) - 1: 0}
    return pl.pallas_call(
        functools.partial(_ffn_kernel, sub=sub),
        out_shape=jax.ShapeDtypeStruct((out_rows, D_MODEL), F32),
        grid=(n_tiles, FFN_DIM // FFN_TF),
        in_specs=in_specs,
        out_specs=pl.BlockSpec((TM, D_MODEL), lambda i, j: (out_off + i, 0)),
        scratch_shapes=[pltpu.VMEM((TM, D_MODEL), BF16)],
        input_output_aliases=aliases,
        compiler_params=_params("parallel", "arbitrary"),
        name=f"ffn_l{layer}_h{which}_t{tile0}n{n_tiles}",
    )(*args)


def _proj_in_kernel(x_ref, mod_ref, g_ref, w_ref, o_ref, h_ref):
    @pl.when(pl.program_id(1) == 0)
    def _():
        for part in range(IN_TILES_PER_STEP):
            cond = _cond_of_tile(pl.program_id(0) * IN_TILES_PER_STEP + part)
            _mod_norm_into(x_ref, mod_ref.at[cond], g_ref, h_ref, 1, row0=part * TM)

    o_ref[...] = _dot(h_ref[...], w_ref[...].astype(BF16))


def _proj_in(x, mod_l, norm_g_l, w_in, layer):
    rows = IN_TILES_PER_STEP * TM
    return pl.pallas_call(
        _proj_in_kernel,
        out_shape=jax.ShapeDtypeStruct((N_ROWS, IN_WIDTH), F32),
        grid=(N_ROWS // rows, IN_WIDTH // IN_TN),
        in_specs=[
            pl.BlockSpec((rows, D_MODEL), lambda i, j: (i, 0), pipeline_mode=pl.Buffered(1)),
            pl.BlockSpec((N_COND, N_MOD, D_MODEL), lambda i, j: (0, 0, 0)),
            pl.BlockSpec((3, D_MODEL), lambda i, j: (0, 0)),
            pl.BlockSpec((None, D_MODEL, IN_TN), lambda i, j: (layer, 0, j)),
        ],
        out_specs=pl.BlockSpec((rows, IN_TN), lambda i, j: (i, j)),
        scratch_shapes=[pltpu.VMEM((rows, D_MODEL), BF16)],
        compiler_params=_params("parallel", "arbitrary"),
        name=f"proj_in_l{layer}",
    )(x, mod_l, norm_g_l, w_in)


def _proj_out_kernel(x_ref, mod_ref, oh_ref, og_ref, od_ref, w_ref, o_ref):
    m = _dot(oh_ref[...], w_ref[0:HGRN_W, :].astype(BF16))
    m += _dot(og_ref[...], w_ref[HGRN_W:HGRN_W + GQA_W, :].astype(BF16))
    m += _dot(od_ref[...], w_ref[HGRN_W + GQA_W:, :].astype(BF16))
    for part in range(OUT_TILES_PER_STEP):
        rows = slice(part * TM, (part + 1) * TM)
        cond = _cond_of_tile(pl.program_id(0) * OUT_TILES_PER_STEP + part)
        o_ref[rows, :] = x_ref[rows, :] + mod_ref[cond, 5:6, :] * m[rows, :]


def _proj_out(x, mod_l, o_h, o_g, o_d, w_out, layer):
    rows = OUT_TILES_PER_STEP * TM
    return pl.pallas_call(
        _proj_out_kernel,
        out_shape=jax.ShapeDtypeStruct((N_ROWS, D_MODEL), F32),
        grid=(N_ROWS // rows, D_MODEL // OUT_TN),
        in_specs=[
            pl.BlockSpec((rows, OUT_TN), lambda i, j: (i, j)),
            pl.BlockSpec((N_COND, N_MOD, OUT_TN), lambda i, j: (0, 0, j)),
            pl.BlockSpec((rows, HGRN_W), lambda i, j: (i, 0)),
            pl.BlockSpec((rows, GQA_W), lambda i, j: (i, 0)),
            pl.BlockSpec((rows, DIFF_W), lambda i, j: (i, 0)),
            pl.BlockSpec((None, D_MODEL, OUT_TN), lambda i, j: (layer, 0, j)),
        ],
        out_specs=pl.BlockSpec((rows, OUT_TN), lambda i, j: (i, j)),
        compiler_params=_params("parallel", "parallel"),
        name=f"proj_out_l{layer}",
    )(x, mod_l, o_h, o_g, o_d, w_out)


def _log2_forget_and_key(z, lb):
    z2 = z * LOG2E
    soft = jnp.log2(1.0 + jnp.exp2(-jnp.abs(z2)))
    log_1mlb = jnp.log1p(-lb) * LOG2E
    a = jnp.log2(lb)
    c = log_1mlb + (jnp.minimum(z2, 0.0) - soft)
    log_f = jnp.maximum(a, c) + jnp.log2(1.0 + jnp.exp2(-jnp.abs(a - c)))
    log_k = log_1mlb + (jnp.minimum(-z2, 0.0) - soft)
    return log_f, log_k


def _cumsum_rows(x, reverse):
    tile = HGRN_SUB
    row = lax.broadcasted_iota(jnp.int32, (tile, 1), 0)
    tiles = []
    for j in range(x.shape[0] // tile):
        y = x[j * tile:(j + 1) * tile, :]
        for sh in (1, 2, 4):
            if reverse:
                y = y + jnp.where(row < tile - sh, pltpu.roll(y, tile - sh, 0), 0.0)
            else:
                y = y + jnp.where(row >= sh, pltpu.roll(y, sh, 0), 0.0)
        tiles.append(y)
    order = range(len(tiles) - 1, -1, -1) if reverse else range(len(tiles))
    carry = None
    for j in order:
        if carry is not None:
            tiles[j] = tiles[j] + carry
        carry = tiles[j][0:1, :] if reverse else tiles[j][tile - 1:tile, :]
    return jnp.concatenate(tiles, axis=0)


def _hgrn_prepare(q, z, lb, reverse):
    lf2, lk2 = _log2_forget_and_key(z, lb)
    b = _cumsum_rows(lf2, reverse)
    return _silu(q), b, b - lk2


def _hgrn_chunk(qs, b, c, v, st, reverse):
    C, SUB = HGRN_CHUNK, HGRN_SUB
    n_sub = C // SUB
    b_tot = b[0:1, :] if reverse else b[C - 1:C, :]

    o = _dot_nt((qs * jnp.exp2(b)).astype(BF16), st.astype(BF16))

    terms = []
    for i in range(n_sub):
        lo = i * SUB
        qi, bi, ci = qs[lo:lo + SUB, :], b[lo:lo + SUB, :], c[lo:lo + SUB, :]
        for s in range(SUB):
            terms.append(qi * jnp.exp2(bi - ci[s:s + 1, :]))
    k_sums = _dot(jnp.concatenate(terms, axis=0).astype(BF16), jnp.ones((HEAD_DIM, C), BF16))

    row = lax.broadcasted_iota(jnp.int32, (SUB, C), 0)
    lane = lax.broadcasted_iota(jnp.int32, (SUB, C), 1)
    lane_s = lane % SUB
    causal = (lane_s >= row) if reverse else (lane_s <= row)
    blocks = []
    for i in range(n_sub):
        lo, hi = i * SUB, (i + 1) * SUB
        diag = k_sums[lo * SUB:(lo + 1) * SUB, :]
        for s in range(1, SUB):
            diag = jnp.where(lane_s == s, k_sums[(lo + s) * SUB:(lo + s + 1) * SUB, :], diag)
        has_off = (i < n_sub - 1) if reverse else (i > 0)
        if has_off:
            ref = b[hi:hi + 1, :] if reverse else b[lo - 1:lo, :]
            qt = (qs[lo:hi, :] * jnp.exp2(b[lo:hi, :] - ref)).astype(BF16)
            if reverse:
                kt = jnp.concatenate([jnp.zeros((hi, HEAD_DIM), F32), jnp.exp2(ref - c[hi:, :])], axis=0)
            else:
                kt = jnp.concatenate([jnp.exp2(ref - c[:lo, :]), jnp.zeros((C - lo, HEAD_DIM), F32)], axis=0)
            off = _dot_nt(qt, kt.astype(BF16))
        else:
            off = jnp.zeros((SUB, C), F32)
        blocks.append(jnp.where((lane // SUB == i) & causal, diag, off))
    a = jnp.concatenate(blocks, axis=0)
    o = o + _dot(a.astype(BF16), v.astype(BF16))

    st_new = jnp.exp2(b_tot) * st + _dot_tn(v.astype(BF16), jnp.exp2(b_tot - c).astype(BF16))
    return o, st_new


def _hgrn_kernel(*refs, layer, seq, latent, n_alias):
    raw_ref, q_ref, v_ref, g_ref, ff_ref, fb_ref, ong_ref = refs[:7]
    s0_ref = refs[7] if latent else None
    outs = refs[7 + (1 if latent else 0) + n_alias:]
    if latent:
        o_ref, of_ref, ob_ref, st_ref, pre_ref = outs
        s_ref = None
    else:
        o_ref, s_ref, of_ref, ob_ref, st_ref, pre_ref = outs
    C = HGRN_CHUNK
    n_chunks = seq // C

    def lower_bound(d, lanes):
        rows = [raw_ref[2 * l + d:2 * l + d + 1, lanes] for l in range(DEPTH)]
        m = functools.reduce(jnp.maximum, rows)
        e = [jnp.exp(r - m) for r in rows]
        tot = functools.reduce(lambda x, y: x + y, e)
        lb = jnp.zeros_like(m)
        for l in range(1, layer + 1):
            lb = lb + e[l] / tot
        return lb

    head_lanes = [slice(hp * HEAD_DIM, (hp + 1) * HEAD_DIM) for hp in range(HGRN_HP)]
    lbs = [[lower_bound(d, lanes) for d in range(2)] for lanes in head_lanes]

    for hp in range(HGRN_HP):
        for d in range(2):
            st_ref[hp, d] = s0_ref[d, hp].T if latent else jnp.zeros((HEAD_DIM, HEAD_DIM), F32)

    def chunk_rows(ci):
        return (pl.ds(pl.multiple_of(ci * C, C), C),
                pl.ds(pl.multiple_of((n_chunks - 1 - ci) * C, C), C))

    def prepare(ci, slot):
        rows = chunk_rows(ci)
        for hp, lanes in enumerate(head_lanes):
            for d, f_ref in enumerate((ff_ref, fb_ref)):
                pre = _hgrn_prepare(q_ref[rows[d], lanes], f_ref[rows[d], lanes], lbs[hp][d], d == 1)
                for n, val in enumerate(pre):
                    pre_ref[slot, 2 * hp + d, n] = val

    prepare(0, 0)

    def body(ci, carry):
        slot = ci % 2
        rows = chunk_rows(ci)
        for hp, lanes in enumerate(head_lanes):
            for d, acc_ref in enumerate((of_ref, ob_ref)):
                qs, b, c = (pre_ref[slot, 2 * hp + d, n] for n in range(3))
                o, st = _hgrn_chunk(qs, b, c, v_ref[rows[d], lanes], st_ref[hp, d], d == 1)
                acc_ref[rows[d], lanes] = o
                st_ref[hp, d] = st
        prepare(jnp.minimum(ci + 1, n_chunks - 1), 1 - slot)
        return carry

    lax.fori_loop(0, n_chunks, body, 0)

    for hp, lanes in enumerate(head_lanes):
        if s_ref is not None:
            for d in range(2):
                s_ref[d, hp] = st_ref[hp, d].T
        o = of_ref[:, lanes] + ob_ref[:, lanes]
        o_ref[:, lanes] = (_rms(o, ong_ref[...]) * _silu(g_ref[:, lanes])).astype(o_ref.dtype)


def _hgrn(proj, lb_raw, onorm_g_l, layer, *, state0=None, o_prev=None, s_prev=None):
    latent = state0 is not None
    n_batch, seq, row0 = (DEC_BATCH, DEC_SEQ, N_CTX // DEC_SEQ) if latent else (BATCH, SEQ, 0)
    width = HGRN_HP * HEAD_DIM

    def col(block):
        return pl.BlockSpec((seq, width), lambda b, h: (row0 + b, block // HGRN_HP + h))

    in_specs = [
        pl.BlockSpec((2 * DEPTH, width), lambda b, h: (0, h)),
        col(COL_HQ), col(COL_HI), col(COL_HG), col(COL_HFF), col(COL_HFB),
        pl.BlockSpec((1, HEAD_DIM), lambda b, h: (0, 0)),
    ]
    args = [lb_raw.reshape(2 * DEPTH, HGRN_W), proj, proj, proj, proj, proj,
            onorm_g_l.reshape(1, HEAD_DIM)]
    o_shape = jax.ShapeDtypeStruct((N_ROWS, HGRN_W), BF16)
    o_spec = pl.BlockSpec((seq, width), lambda b, h: (row0 + b, h))
    aliases = {}
    if latent:
        in_specs += [pl.BlockSpec((None, None, 2, HGRN_HP, HEAD_DIM, HEAD_DIM),
                                  lambda b, h: (b, layer, 0, h, 0, 0)), _ANY]
        args += [state0, o_prev]
        aliases = {len(args) - 1: 0}
        out_shape, out_specs = o_shape, o_spec
    else:
        if s_prev is not None:
            in_specs.append(_ANY)
            args.append(s_prev)
            aliases = {len(args) - 1: 1}
        out_shape = (o_shape, jax.ShapeDtypeStruct(
            (BATCH, DEPTH, 2, HGRN_HEADS, HEAD_DIM, HEAD_DIM), F32))
        out_specs = (o_spec, pl.BlockSpec((None, None, 2, HGRN_HP, HEAD_DIM, HEAD_DIM),
                                          lambda b, h: (b, layer, 0, h, 0, 0)))
    return pl.pallas_call(
        functools.partial(_hgrn_kernel, layer=layer, seq=seq, latent=latent, n_alias=len(aliases)),
        out_shape=out_shape,
        grid=(n_batch, HGRN_HEADS // HGRN_HP),
        in_specs=in_specs,
        out_specs=out_specs,
        scratch_shapes=[pltpu.VMEM((seq, width), F32), pltpu.VMEM((seq, width), F32),
                        pltpu.VMEM((HGRN_HP, 2, HEAD_DIM, HEAD_DIM), F32),
                        pltpu.VMEM((2, 2 * HGRN_HP, 3, HGRN_CHUNK, HEAD_DIM), F32)],
        input_output_aliases=aliases,
        compiler_params=_params("parallel", "parallel"),
        name=f"hgrn_l{layer}_{'lat' if latent else 'ctx'}",
    )(*args)


def _swap_pairs(x, width):
    lanes = x.shape[-1]
    lane = lax.broadcasted_iota(jnp.int32, x.shape, x.ndim - 1)
    from_right = pltpu.roll(x, lanes - width, x.ndim - 1)
    from_left = pltpu.roll(x, width, x.ndim - 1)
    return jnp.where(lane % (2 * width) < width, from_right, from_left)


def _rope(x, cos, sin_signed, quarter):
    return x * cos + _swap_pairs(x, quarter) * sin_signed


def _rms_halves(x, g):
    half = x.shape[-1] // 2
    lane = lax.broadcasted_iota(jnp.int32, x.shape, x.ndim - 1)
    lo = lane < half
    sq = x * x
    ms_lo = jnp.sum(jnp.where(lo, sq, 0.0), axis=-1, keepdims=True) / half
    ms_hi = jnp.sum(jnp.where(lo, 0.0, sq), axis=-1, keepdims=True) / half
    ms = jnp.where(lo, ms_lo, ms_hi)
    return (x * lax.rsqrt(ms + EPS)) * g


def _gqa_kernel(*refs, seq, latent, n_alias):
    qg_ref, kg_ref = refs[:2]
    q_refs = refs[2:2 + GQA_GROUP]
    k_ref, v_ref = refs[2 + GQA_GROUP:4 + GQA_GROUP]
    rest = refs[4 + GQA_GROUP:]
    if latent:
        ck_ref, cv_ref, cos_ref, sin_ref = rest[:4]
        o_ref, kt_ref, vt_ref = rest[4 + n_alias:]
    else:
        o_ref, kn_ref, vo_ref, kt_ref, vt_ref = rest[n_alias:]
    past = PAST_LEN if latent else 0
    quarter = HEAD_DIM // 4

    kn = _rms(k_ref[...], kg_ref[...])
    v = v_ref[...]
    if latent:
        kn = _rope(kn, cos_ref[...], sin_ref[...], quarter)
        kt_ref[0:past, :] = ck_ref[...].astype(BF16)
        vt_ref[0:past, :] = cv_ref[...].astype(BF16)
    else:
        kn_ref[...] = kn
        vo_ref[...] = v
    kt_ref[past:past + seq, :] = kn.astype(BF16)
    vt_ref[past:past + seq, :] = v.astype(BF16)

    q_scale = HEAD_DIM ** -0.5 * LOG2E

    def attend(q_rows):
        s = _dot_nt(jnp.concatenate(q_rows, axis=0), kt_ref[...])
        p = jnp.exp2(s - jnp.max(s, axis=-1, keepdims=True))
        inv = 1.0 / jnp.sum(p, axis=-1, keepdims=True)
        o = (_dot(p.astype(BF16), vt_ref[...]) * inv).astype(o_ref.dtype)
        n = q_rows[0].shape[0]
        return [o[i * n:(i + 1) * n, :] for i in range(len(q_rows))]

    def query(q_ref, rows):
        qn = _rms(q_ref[rows, :], qg_ref[...])
        if latent:
            qn = _rope(qn, cos_ref[rows, :], sin_ref[rows, :], quarter)
        return (qn * q_scale).astype(BF16)

    head_lanes = [slice(g * HEAD_DIM, (g + 1) * HEAD_DIM) for g in range(GQA_GROUP)]
    if seq <= GQA_Q_BLOCK:
        rows = slice(0, seq)
        outs = attend([query(q_ref, rows) for q_ref in q_refs])
        for lanes, o in zip(head_lanes, outs):
            o_ref[rows, lanes] = o
    else:
        for lanes, q_ref in zip(head_lanes, q_refs):
            for r in range(seq // GQA_Q_BLOCK):
                rows = slice(r * GQA_Q_BLOCK, (r + 1) * GQA_Q_BLOCK)
                o_ref[rows, lanes] = attend([query(q_ref, rows)])[0]


def _gqa(proj, qnorm_g, knorm_g, layer, *, latent_args=None, kv_prev=None):
    latent = latent_args is not None
    n_batch, seq, row0 = (DEC_BATCH, DEC_SEQ, N_CTX // DEC_SEQ) if latent else (BATCH, SEQ, 0)
    past = PAST_LEN if latent else 0

    def vec():
        return pl.BlockSpec((1, HEAD_DIM), lambda b, h: (0, 0))

    def q_spec(g):
        return pl.BlockSpec((seq, HEAD_DIM), lambda b, h: (row0 + b, COL_GQ + h * GQA_GROUP + g))

    in_specs = [vec(), vec()] + [q_spec(g) for g in range(GQA_GROUP)] + [
        pl.BlockSpec((seq, HEAD_DIM), lambda b, h: (row0 + b, COL_GK + h)),
        pl.BlockSpec((seq, HEAD_DIM), lambda b, h: (row0 + b, COL_GV + h)),
    ]
    args = ([qnorm_g.reshape(1, HEAD_DIM), knorm_g.reshape(1, HEAD_DIM)]
            + [proj] * (GQA_GROUP + 2))
    o_spec = pl.BlockSpec((seq, GQA_GROUP * HEAD_DIM), lambda b, h: (row0 + b, h))
    o_shape = jax.ShapeDtypeStruct((N_ROWS, GQA_W), BF16)
    cache_spec = pl.BlockSpec((None, None, SEQ, HEAD_DIM), lambda b, h: (b, layer, 0, h))
    aliases = {}
    if latent:
        cache_k, cache_v, cos, sin_signed, o_prev = latent_args
        table_spec = pl.BlockSpec((seq, HEAD_DIM), lambda b, h: (0, 0))
        in_specs += [cache_spec, cache_spec, table_spec, table_spec, _ANY]
        args += [cache_k.reshape(DEC_BATCH, DEPTH, PAST_LEN, GQA_KV_W),
                 cache_v.reshape(DEC_BATCH, DEPTH, PAST_LEN, GQA_KV_W), cos, sin_signed, o_prev]
        aliases = {len(args) - 1: 0}
        out_shape, out_specs = o_shape, o_spec
    else:
        if kv_prev is not None:
            in_specs += [_ANY, _ANY]
            args += list(kv_prev)
            aliases = {len(args) - 2: 1, len(args) - 1: 2}
        new_shape = jax.ShapeDtypeStruct((BATCH, DEPTH, SEQ, GQA_KV_W), F32)
        out_shape = (o_shape, new_shape, new_shape)
        out_specs = (o_spec, cache_spec, cache_spec)
    return pl.pallas_call(
        functools.partial(_gqa_kernel, seq=seq, latent=latent, n_alias=len(aliases)),
        out_shape=out_shape,
        grid=(n_batch, GQA_KV_HEADS),
        in_specs=in_specs,
        out_specs=out_specs,
        scratch_shapes=[pltpu.VMEM((past + seq, HEAD_DIM), BF16),
                        pltpu.VMEM((past + seq, HEAD_DIM), BF16)],
        input_output_aliases=aliases,
        compiler_params=_params("parallel", "parallel"),
        name=f"gqa_l{layer}_{'lat' if latent else 'ctx'}",
    )(*args)


def _diff_kernel(*refs, seq, latent, lam_init, n_alias, heads):
    qg_ref, kg_ref, sg_ref, lam_ref, q_ref, k_ref, v_ref = refs[:7]
    if latent:
        ck_ref, cv_ref, cos_ref, sin_ref = refs[7:11]
        o_ref, kt_ref, vt_ref = refs[11 + n_alias:]
    else:
        o_ref, kn_ref, vo_ref, kt_ref, vt_ref = refs[7 + n_alias:]
    past = PAST_LEN if latent else 0
    quarter = DIFF_QK_DIM // 4
    head_lanes = [slice(hd * HEAD_DIM, (hd + 1) * HEAD_DIM) for hd in range(heads)]

    for lanes in head_lanes:
        kn = _rms_halves(k_ref[:, lanes], kg_ref[...])
        v = v_ref[:, lanes]
        if latent:
            kn = _rope(kn, cos_ref[...], sin_ref[...], quarter)
            kt_ref[0:past, lanes] = ck_ref[:, lanes].astype(BF16)
            vt_ref[0:past, lanes] = cv_ref[:, lanes].astype(BF16)
        else:
            kn_ref[:, lanes] = kn
            vo_ref[:, lanes] = v
        kt_ref[past:past + seq, lanes] = kn.astype(BF16)
        vt_ref[past:past + seq, lanes] = v.astype(BF16)

    lv = lam_ref[...]
    lam = (jnp.exp(jnp.sum(lv[0:1, :] * lv[1:2, :], axis=-1, keepdims=True))
           - jnp.exp(jnp.sum(lv[2:3, :] * lv[3:4, :], axis=-1, keepdims=True)) + lam_init)
    q_scale = DIFF_QK_DIM ** -0.5 * LOG2E

    q_block = min(seq, Q_BLOCK)
    for lanes in head_lanes:
        for r in range(seq // q_block):
            rows = slice(r * q_block, (r + 1) * q_block)
            qn = _rms_halves(q_ref[rows, lanes], qg_ref[...])
            if latent:
                qn = _rope(qn, cos_ref[rows, :], sin_ref[rows, :], quarter)
            qn = qn * q_scale
            lo = lax.broadcasted_iota(jnp.int32, qn.shape, 1) < DIFF_QK_DIM
            kt = kt_ref[:, lanes]
            s1 = _dot_nt(jnp.where(lo, qn, 0.0).astype(BF16), kt)
            s2 = _dot_nt(jnp.where(lo, 0.0, qn).astype(BF16), kt)
            p1 = jnp.exp2(s1 - jnp.max(s1, axis=-1, keepdims=True))
            p2 = jnp.exp2(s2 - jnp.max(s2, axis=-1, keepdims=True))
            w1 = 1.0 / jnp.sum(p1, axis=-1, keepdims=True)
            w2 = lam / jnp.sum(p2, axis=-1, keepdims=True)
            a = p1 * w1 - p2 * w2
            o = _dot(a.astype(BF16), vt_ref[:, lanes])
            o = _rms(o, sg_ref[...]) * (1.0 - lam_init)
            o_ref[rows, lanes] = o.astype(o_ref.dtype)


def _diff(proj, qnorm_g, knorm_g, subln_g, lam_params, layer, *, latent_args=None, kv_prev=None):
    latent = latent_args is not None
    n_batch, seq, row0 = (DEC_BATCH, DEC_SEQ, N_CTX // DEC_SEQ) if latent else (BATCH, SEQ, 0)
    past = PAST_LEN if latent else 0
    lam_init = 0.8 - 0.6 * math.exp(-0.3 * layer)
    heads = DIFF_HP_LAT if latent else DIFF_HP_CTX
    width = heads * HEAD_DIM

    def vec():
        return pl.BlockSpec((1, HEAD_DIM), lambda b, h: (0, 0))

    def col(block):
        return pl.BlockSpec((seq, width), lambda b, h: (row0 + b, block // heads + h))

    in_specs = [
        vec(), vec(), vec(),
        pl.BlockSpec((4, DIFF_QK_DIM), lambda b, h: (0, 0)),
        col(COL_DQ), col(COL_DK), col(COL_DV),
    ]
    args = [jnp.tile(qnorm_g, 2).reshape(1, HEAD_DIM), jnp.tile(knorm_g, 2).reshape(1, HEAD_DIM),
            subln_g.reshape(1, HEAD_DIM), lam_params, proj, proj, proj]
    o_spec = pl.BlockSpec((seq, width), lambda b, h: (row0 + b, h))
    o_shape = jax.ShapeDtypeStruct((N_ROWS, DIFF_W), BF16)
    cache_spec = pl.BlockSpec((None, None, SEQ, width), lambda b, h: (b, layer, 0, h))
    aliases = {}
    if latent:
        cache_k, cache_v, cos, sin_signed, o_prev = latent_args
        table_spec = pl.BlockSpec((seq, HEAD_DIM), lambda b, h: (0, 0))
        in_specs += [cache_spec, cache_spec, table_spec, table_spec, _ANY]
        args += [cache_k.reshape(DEC_BATCH, DEPTH, PAST_LEN, DIFF_W),
                 cache_v.reshape(DEC_BATCH, DEPTH, PAST_LEN, DIFF_W), cos, sin_signed, o_prev]
        aliases = {len(args) - 1: 0}
        out_shape, out_specs = o_shape, o_spec
    else:
        if kv_prev is not None:
            in_specs += [_ANY, _ANY]
            args += list(kv_prev)
            aliases = {len(args) - 2: 1, len(args) - 1: 2}
        new_shape = jax.ShapeDtypeStruct((BATCH, DEPTH, SEQ, DIFF_W), F32)
        out_shape = (o_shape, new_shape, new_shape)
        out_specs = (o_spec, cache_spec, cache_spec)
    return pl.pallas_call(
        functools.partial(_diff_kernel, seq=seq, latent=latent, lam_init=lam_init,
                          n_alias=len(aliases), heads=heads),
        out_shape=out_shape,
        grid=(n_batch, DIFF_HEADS // heads),
        in_specs=in_specs,
        out_specs=out_specs,
        scratch_shapes=[pltpu.VMEM((past + seq, width), BF16),
                        pltpu.VMEM((past + seq, width), BF16)],
        input_output_aliases=aliases,
        compiler_params=_params("parallel", "parallel"),
        name=f"diff_l{layer}_{'lat' if latent else 'ctx'}",
    )(*args)


def _rope_tables(n_tokens, dim, repeat):
    quarter = dim // 4
    t = jnp.arange(n_tokens)
    pos = jnp.stack([t // GRID_W, t % GRID_W], axis=-1).astype(F32)
    inv = ROPE_BASE ** (-jnp.arange(quarter, dtype=F32) / quarter)
    ang = pos[:, :, None] * inv
    cos, sin = jnp.cos(ang), jnp.sin(ang)
    cos_l = jnp.concatenate([cos, cos], axis=-1).reshape(n_tokens, dim)
    sin_l = jnp.concatenate([-sin, sin], axis=-1).reshape(n_tokens, dim)
    return jnp.tile(cos_l, (1, repeat)), jnp.tile(sin_l, (1, repeat))


def kernel(x_prompt, x_sample, c, cache_gqa_k, cache_gqa_v, cache_diff_k, cache_diff_v, state_hgrn,
           c_ctx, w_mod, b_mod, norm_g, ffn_w_gate, ffn_w_up, ffn_w_down, w_in, w_out, hgrn_lb_raw,
           hgrn_onorm_g, gqa_qnorm_g, gqa_knorm_g, diff_qnorm_g, diff_knorm_g, diff_lambda,
           diff_subln_g):
    cond = jnp.concatenate(
        [c_ctx[None, :], c, jnp.zeros((COND_PAD - N_COND, D_MODEL), F32)], axis=0)
    mod = _modulation(cond, w_mod, b_mod)

    cos_g, sin_g = _rope_tables(DEC_SEQ, HEAD_DIM, 1)
    cos_d, sin_d = _rope_tables(DEC_SEQ, DIFF_QK_DIM, 2)
    cache_dk = cache_diff_k.reshape(DEC_BATCH, DEPTH, PAST_LEN, DIFF_HEADS, 2 * DIFF_QK_DIM)
    ctx_rows = dict(n_tiles=CTX_TILES, tile0=0)
    lat_rows = dict(n_tiles=LAT_TILES, tile0=CTX_TILES)

    x = None
    states = gqa_kv = diff_kv = None
    for l in range(DEPTH):
        ffn_w = (mod[l], norm_g[l], ffn_w_gate, ffn_w_up, ffn_w_down, l)
        if l == 0:
            x = _ffn(x_prompt.reshape(N_CTX, D_MODEL), *ffn_w, 0, **ctx_rows)
            x = _ffn(x_sample.reshape(N_LAT, D_MODEL), *ffn_w, 0, **lat_rows,
                     out_off=CTX_TILES, o_prev=x)
        else:
            x = _ffn(x, *ffn_w, 0)
        proj = _proj_in(x, mod[l], norm_g[l], w_in, l)

        o_h, states = _hgrn(proj, hgrn_lb_raw, hgrn_onorm_g[l], l, s_prev=states)
        o_h = _hgrn(proj, hgrn_lb_raw, hgrn_onorm_g[l], l, state0=state_hgrn, o_prev=o_h)

        gqa_w = (proj, gqa_qnorm_g[l], gqa_knorm_g[l], l)
        o_g, *gqa_kv = _gqa(*gqa_w, kv_prev=gqa_kv)
        o_g = _gqa(*gqa_w, latent_args=(cache_gqa_k, cache_gqa_v, cos_g, sin_g, o_g))

        diff_w = (proj, diff_qnorm_g[l], diff_knorm_g[l], diff_subln_g[l], diff_lambda[l], l)
        o_d, *diff_kv = _diff(*diff_w, kv_prev=diff_kv)
        o_d = _diff(*diff_w, latent_args=(cache_dk, cache_diff_v, cos_d, sin_d, o_d))

        x = _proj_out(x, mod[l], o_h, o_g, o_d, w_out, l)
        if l < DEPTH - 1:
            x = _ffn(x, *ffn_w, 1)
        else:
            y_prompt = _ffn(x, *ffn_w, 1, **ctx_rows, out_rows=N_CTX)
            y_sample = _ffn(x, *ffn_w, 1, **lat_rows, in_off=CTX_TILES, out_rows=N_LAT)

    return (y_prompt.reshape(BATCH, SEQ, D_MODEL), y_sample.reshape(DEC_BATCH, DEC_SEQ, D_MODEL),
            gqa_kv[0].reshape(BATCH, DEPTH, SEQ, GQA_KV_HEADS, HEAD_DIM),
            gqa_kv[1].reshape(BATCH, DEPTH, SEQ, GQA_KV_HEADS, HEAD_DIM),
            diff_kv[0].reshape(BATCH, DEPTH, SEQ, DIFF_HEADS, 2, DIFF_QK_DIM),
            diff_kv[1].reshape(BATCH, DEPTH, SEQ, DIFF_HEADS, HEAD_DIM),
            states)
```

```python
import functools
import math

import jax
import jax.numpy as jnp
from jax import lax
from jax.experimental import pallas as pl
from jax.experimental.pallas import tpu as pltpu

F32 = jnp.float32
BF16 = jnp.bfloat16

D_MODEL = 2048
BATCH = 16
SEQ = 256
DEPTH = 2
DEC_BATCH = 2
DEC_SEQ = 1024
PAST_LEN = 256
GRID_W = 64
HEAD_DIM = 128
HGRN_HEADS = 4
GQA_Q_HEADS = 6
GQA_KV_HEADS = 2
GQA_GROUP = GQA_Q_HEADS // GQA_KV_HEADS
DIFF_HEADS = 6
DIFF_QK_DIM = 64
FFN_DIM = 5632
N_MOD = 9
IN_WIDTH = 6144
ROPE_BASE = 10000.0
EPS = 1e-6
LOG2E = math.log2(math.e)

N_CTX = BATCH * SEQ
N_LAT = DEC_BATCH * DEC_SEQ
N_ROWS = N_CTX + N_LAT
N_COND = 1 + DEC_BATCH
COND_PAD = 8

COL_HQ, COL_HI, COL_HG, COL_HFF, COL_HFB = 0, 4, 8, 12, 16
COL_GQ, COL_GK, COL_GV = 20, 26, 28
COL_DQ, COL_DK, COL_DV = 30, 36, 42

HGRN_W = HGRN_HEADS * HEAD_DIM
GQA_W = GQA_Q_HEADS * HEAD_DIM
GQA_KV_W = GQA_KV_HEADS * HEAD_DIM
DIFF_W = DIFF_HEADS * HEAD_DIM

VMEM_LIMIT = 60 * 1024 * 1024

TM = 1024
ROW_CHUNK = 128
FFN_TF = 256
FFN_TILES_PER_STEP = 2
FFN_NB = 512
IN_TN = 512
IN_TILES_PER_STEP = 2
OUT_TN = 512
OUT_TILES_PER_STEP = 2
MOD_TN = 1024
HGRN_CHUNK = 64
HGRN_SUB = 8
HGRN_HP = 4
Q_BLOCK = 512
GQA_Q_BLOCK = 256
DIFF_HP_CTX = 6
DIFF_HP_LAT = 2

CTX_TILES = N_CTX // TM
LAT_TILES = N_LAT // TM
ALL_TILES = CTX_TILES + LAT_TILES


def _cond_of_tile(i):
    tiles_per_latent = DEC_SEQ // TM
    return jnp.where(i < CTX_TILES, 0, 1 + (i - CTX_TILES) // tiles_per_latent)


def _silu(x):
    return x * jax.nn.sigmoid(x)


def _dot(a, b):
    return jnp.dot(a, b, preferred_element_type=F32)


def _dot_nt(a, b):
    return lax.dot_general(a, b, (((1,), (1,)), ((), ())), preferred_element_type=F32)


def _dot_tn(a, b):
    return lax.dot_general(a, b, (((0,), (0,)), ((), ())), preferred_element_type=F32)


def _rms(x, g):
    ms = jnp.mean(x * x, axis=-1, keepdims=True)
    return (x * lax.rsqrt(ms + EPS)) * g


def _params(*semantics):
    return pltpu.CompilerParams(dimension_semantics=semantics, vmem_limit_bytes=VMEM_LIMIT)


_ANY = pl.BlockSpec(memory_space=pl.ANY)


def _mod_kernel(cond_ref, w_ref, b_ref, o_ref):
    a = _silu(cond_ref[...]).astype(BF16)
    o_ref[...] = _dot(a, w_ref[...].astype(BF16)) + b_ref[...]


def _modulation(cond, w_mod, b_mod):
    width = N_MOD * D_MODEL
    out = pl.pallas_call(
        _mod_kernel,
        out_shape=jax.ShapeDtypeStruct((DEPTH, COND_PAD, width), F32),
        grid=(DEPTH, width // MOD_TN),
        in_specs=[
            pl.BlockSpec((COND_PAD, D_MODEL), lambda l, j: (0, 0)),
            pl.BlockSpec((None, D_MODEL, MOD_TN), lambda l, j: (l, 0, j)),
            pl.BlockSpec((None, 1, MOD_TN), lambda l, j: (l, 0, j)),
        ],
        out_specs=pl.BlockSpec((None, COND_PAD, MOD_TN), lambda l, j: (l, 0, j)),
        compiler_params=_params("parallel", "parallel"),
        name="modulation",
    )(cond, w_mod, b_mod.reshape(DEPTH, 1, width))
    return out[:, :N_COND].reshape(DEPTH, N_COND, N_MOD, D_MODEL)


def _mod_norm_into(x_ref, mod_ref, g_ref, h_ref, sub, row0=0):
    shift = mod_ref[3 * sub:3 * sub + 1, :]
    gain = g_ref[sub:sub + 1, :] * (1.0 + mod_ref[3 * sub + 1:3 * sub + 2, :])

    def body(r, carry):
        rows = pl.ds(pl.multiple_of(row0 + r * ROW_CHUNK, ROW_CHUNK), ROW_CHUNK)
        h_ref[rows, :] = (_rms(x_ref[rows, :], gain) + shift).astype(BF16)
        return carry

    lax.fori_loop(0, TM // ROW_CHUNK, body, 0)


def _ffn_kernel(x_ref, mod_ref, g_ref, wg_ref, wu_ref, wd_ref, *rest, sub, tile0):
    o_ref, h_ref = rest[-2:]
    j = pl.program_id(1)
    conds = [_cond_of_tile(tile0 + pl.program_id(0) * FFN_TILES_PER_STEP + part)
             for part in range(FFN_TILES_PER_STEP)]

    @pl.when(j == 0)
    def _():
        for part, cond in enumerate(conds):
            _mod_norm_into(x_ref, mod_ref.at[cond], g_ref, h_ref, sub, row0=part * TM)
        o_ref[...] = jnp.zeros_like(o_ref)

    for part in range(FFN_TILES_PER_STEP):
        rows = slice(part * TM, (part + 1) * TM)
        h = h_ref[rows, :]
        gate_act = _dot(h, wg_ref[...].astype(BF16))
        up = _dot(h, wu_ref[...].astype(BF16))
        a = (_silu(gate_act) * up).astype(BF16)
        for n in range(0, D_MODEL, FFN_NB):
            o_ref[rows, n:n + FFN_NB] += _dot(a, wd_ref[:, n:n + FFN_NB].astype(BF16))

    @pl.when(j == pl.num_programs(1) - 1)
    def _():
        for part, cond in enumerate(conds):
            gate = mod_ref[cond, 3 * sub + 2:3 * sub + 3, :]

            def body(r, carry):
                rows = pl.ds(pl.multiple_of(part * TM + r * ROW_CHUNK, ROW_CHUNK), ROW_CHUNK)
                o_ref[rows, :] = x_ref[rows, :] + gate * (0.5 * o_ref[rows, :])
                return carry

            lax.fori_loop(0, TM // ROW_CHUNK, body, 0)


def _ffn(x, mod_l, norm_g_l, w_gate, w_up, w_down, layer, which, *, n_tiles=ALL_TILES, in_off=0,
         out_off=0, tile0=0, out_rows=N_ROWS, o_prev=None):
    sub = 2 * which
    per = FFN_TILES_PER_STEP
    rows = per * TM
    assert n_tiles % per == 0 and in_off % per == 0 and out_off % per == 0
    in_specs = [
        pl.BlockSpec((rows, D_MODEL), lambda i, j: (in_off // per + i, 0), pipeline_mode=pl.Buffered(1)),
        pl.BlockSpec((N_COND, N_MOD, D_MODEL), lambda i, j: (0, 0, 0)),
        pl.BlockSpec((3, D_MODEL), lambda i, j: (0, 0)),
        pl.BlockSpec((None, None, D_MODEL, FFN_TF), lambda i, j: (layer, which, 0, j)),
        pl.BlockSpec((None, None, D_MODEL, FFN_TF), lambda i, j: (layer, which, 0, j)),
        pl.BlockSpec((None, None, FFN_TF, D_MODEL), lambda i, j: (layer, which, j, 0)),
    ]
    args = [x, mod_l, norm_g_l, w_gate, w_up, w_down]
    aliases = {}
    if o_prev is not None:
        in_specs.append(_ANY)
        args.append(o_prev)
        aliases = {len(args) - 1: 0}
    return pl.pallas_call(
        functools.partial(_ffn_kernel, sub=sub, tile0=tile0),
        out_shape=jax.ShapeDtypeStruct((out_rows, D_MODEL), F32),
        grid=(n_tiles // per, FFN_DIM // FFN_TF),
        in_specs=in_specs,
        out_specs=pl.BlockSpec((rows, D_MODEL), lambda i, j: (out_off // per + i, 0),
                               pipeline_mode=pl.Buffered(1)),
        scratch_shapes=[pltpu.VMEM((rows, D_MODEL), BF16)],
        input_output_aliases=aliases,
        compiler_params=_params("parallel", "arbitrary"),
        name=f"ffn_l{layer}_h{which}_t{tile0}n{n_tiles}",
    )(*args)


def _proj_in_kernel(x_ref, mod_ref, g_ref, w_ref, o_ref, h_ref):
    @pl.when(pl.program_id(1) == 0)
    def _():
        for part in range(IN_TILES_PER_STEP):
            cond = _cond_of_tile(pl.program_id(0) * IN_TILES_PER_STEP + part)
            _mod_norm_into(x_ref, mod_ref.at[cond], g_ref, h_ref, 1, row0=part * TM)

    o_ref[...] = _dot(h_ref[...], w_ref[...].astype(BF16))


def _proj_in(x, mod_l, norm_g_l, w_in, layer):
    rows = IN_TILES_PER_STEP * TM
    return pl.pallas_call(
        _proj_in_kernel,
        out_shape=jax.ShapeDtypeStruct((N_ROWS, IN_WIDTH), F32),
        grid=(N_ROWS // rows, IN_WIDTH // IN_TN),
        in_specs=[
            pl.BlockSpec((rows, D_MODEL), lambda i, j: (i, 0), pipeline_mode=pl.Buffered(1)),
            pl.BlockSpec((N_COND, N_MOD, D_MODEL), lambda i, j: (0, 0, 0)),
            pl.BlockSpec((3, D_MODEL), lambda i, j: (0, 0)),
            pl.BlockSpec((None, D_MODEL, IN_TN), lambda i, j: (layer, 0, j)),
        ],
        out_specs=pl.BlockSpec((rows, IN_TN), lambda i, j: (i, j)),
        scratch_shapes=[pltpu.VMEM((rows, D_MODEL), BF16)],
        compiler_params=_params("parallel", "arbitrary"),
        name=f"proj_in_l{layer}",
    )(x, mod_l, norm_g_l, w_in)


def _proj_out_kernel(x_ref, mod_ref, oh_ref, og_ref, od_ref, w_ref, o_ref):
    m = _dot(oh_ref[...], w_ref[0:HGRN_W, :].astype(BF16))
    m += _dot(og_ref[...], w_ref[HGRN_W:HGRN_W + GQA_W, :].astype(BF16))
    m += _dot(od_ref[...], w_ref[HGRN_W + GQA_W:, :].astype(BF16))
    for part in range(OUT_TILES_PER_STEP):
        rows = slice(part * TM, (part + 1) * TM)
        cond = _cond_of_tile(pl.program_id(0) * OUT_TILES_PER_STEP + part)
        o_ref[rows, :] = x_ref[rows, :] + mod_ref[cond, 5:6, :] * m[rows, :]


def _proj_out(x, mod_l, o_h, o_g, o_d, w_out, layer):
    rows = OUT_TILES_PER_STEP * TM
    return pl.pallas_call(
        _proj_out_kernel,
        out_shape=jax.ShapeDtypeStruct((N_ROWS, D_MODEL), F32),
        grid=(N_ROWS // rows, D_MODEL // OUT_TN),
        in_specs=[
            pl.BlockSpec((rows, OUT_TN), lambda i, j: (i, j)),
            pl.BlockSpec((N_COND, N_MOD, OUT_TN), lambda i, j: (0, 0, j)),
            pl.BlockSpec((rows, HGRN_W), lambda i, j: (i, 0)),
            pl.BlockSpec((rows, GQA_W), lambda i, j: (i, 0)),
            pl.BlockSpec((rows, DIFF_W), lambda i, j: (i, 0)),
            pl.BlockSpec((None, D_MODEL, OUT_TN), lambda i, j: (layer, 0, j)),
        ],
        out_specs=pl.BlockSpec((rows, OUT_TN), lambda i, j: (i, j)),
        compiler_params=_params("parallel", "parallel"),
        name=f"proj_out_l{layer}",
    )(x, mod_l, o_h, o_g, o_d, w_out)


def _log2_forget_and_key(z, lb):
    z2 = z * LOG2E
    soft = jnp.log2(1.0 + jnp.exp2(-jnp.abs(z2)))
    log_1mlb = jnp.log1p(-lb) * LOG2E
    a = jnp.log2(lb)
    c = log_1mlb + (jnp.minimum(z2, 0.0) - soft)
    log_f = jnp.maximum(a, c) + jnp.log2(1.0 + jnp.exp2(-jnp.abs(a - c)))
    log_k = log_1mlb + (jnp.minimum(-z2, 0.0) - soft)
    return log_f, log_k


def _cumsum_rows(x, reverse):
    tile = HGRN_SUB
    row = lax.broadcasted_iota(jnp.int32, (tile, 1), 0)
    tiles = []
    for j in range(x.shape[0] // tile):
        y = x[j * tile:(j + 1) * tile, :]
        for sh in (1, 2, 4):
            if reverse:
                y = y + jnp.where(row < tile - sh, pltpu.roll(y, tile - sh, 0), 0.0)
            else:
                y = y + jnp.where(row >= sh, pltpu.roll(y, sh, 0), 0.0)
        tiles.append(y)
    order = range(len(tiles) - 1, -1, -1) if reverse else range(len(tiles))
    carry = None
    for j in order:
        if carry is not None:
            tiles[j] = tiles[j] + carry
        carry = tiles[j][0:1, :] if reverse else tiles[j][tile - 1:tile, :]
    return jnp.concatenate(tiles, axis=0)


def _hgrn_prepare(q, z, lb, reverse):
    lf2, lk2 = _log2_forget_and_key(z, lb)
    b = _cumsum_rows(lf2, reverse)
    return _silu(q), b, b - lk2


def _hgrn_chunk(qs, b, c, v, st, reverse):
    C, SUB = HGRN_CHUNK, HGRN_SUB
    n_sub = C // SUB
    b_tot = b[0:1, :] if reverse else b[C - 1:C, :]

    o = _dot_nt((qs * jnp.exp2(b)).astype(BF16), st.astype(BF16))

    terms = []
    for i in range(n_sub):
        lo = i * SUB
        qi, bi, ci = qs[lo:lo + SUB, :], b[lo:lo + SUB, :], c[lo:lo + SUB, :]
        for s in range(SUB):
            terms.append(qi * jnp.exp2(bi - ci[s:s + 1, :]))
    k_sums = _dot(jnp.concatenate(terms, axis=0).astype(BF16), jnp.ones((HEAD_DIM, C), BF16))

    row = lax.broadcasted_iota(jnp.int32, (SUB, C), 0)
    lane = lax.broadcasted_iota(jnp.int32, (SUB, C), 1)
    lane_s = lane % SUB
    causal = (lane_s >= row) if reverse else (lane_s <= row)
    blocks = []
    for i in range(n_sub):
        lo, hi = i * SUB, (i + 1) * SUB
        diag = k_sums[lo * SUB:(lo + 1) * SUB, :]
        for s in range(1, SUB):
            diag = jnp.where(lane_s == s, k_sums[(lo + s) * SUB:(lo + s + 1) * SUB, :], diag)
        has_off = (i < n_sub - 1) if reverse else (i > 0)
        if has_off:
            ref = b[hi:hi + 1, :] if reverse else b[lo - 1:lo, :]
            qt = (qs[lo:hi, :] * jnp.exp2(b[lo:hi, :] - ref)).astype(BF16)
            if reverse:
                kt = jnp.concatenate([jnp.zeros((hi, HEAD_DIM), F32), jnp.exp2(ref - c[hi:, :])], axis=0)
            else:
                kt = jnp.concatenate([jnp.exp2(ref - c[:lo, :]), jnp.zeros((C - lo, HEAD_DIM), F32)], axis=0)
            off = _dot_nt(qt, kt.astype(BF16))
        else:
            off = jnp.zeros((SUB, C), F32)
        blocks.append(jnp.where((lane // SUB == i) & causal, diag, off))
    a = jnp.concatenate(blocks, axis=0)
    o = o + _dot(a.astype(BF16), v.astype(BF16))

    st_new = jnp.exp2(b_tot) * st + _dot_tn(v.astype(BF16), jnp.exp2(b_tot - c).astype(BF16))
    return o, st_new


def _hgrn_kernel(*refs, layer, seq, latent, n_alias):
    raw_ref, q_ref, v_ref, g_ref, ff_ref, fb_ref, ong_ref = refs[:7]
    s0_ref = refs[7] if latent else None
    outs = refs[7 + (1 if latent else 0) + n_alias:]
    if latent:
        o_ref, of_ref, ob_ref, st_ref, pre_ref = outs
        s_ref = None
    else:
        o_ref, s_ref, of_ref, ob_ref, st_ref, pre_ref = outs
    C = HGRN_CHUNK
    n_chunks = seq // C

    def lower_bound(d, lanes):
        rows = [raw_ref[2 * l + d:2 * l + d + 1, lanes] for l in range(DEPTH)]
        m = functools.reduce(jnp.maximum, rows)
        e = [jnp.exp(r - m) for r in rows]
        tot = functools.reduce(lambda x, y: x + y, e)
        lb = jnp.zeros_like(m)
        for l in range(1, layer + 1):
            lb = lb + e[l] / tot
        return lb

    head_lanes = [slice(hp * HEAD_DIM, (hp + 1) * HEAD_DIM) for hp in range(HGRN_HP)]
    lbs = [[lower_bound(d, lanes) for d in range(2)] for lanes in head_lanes]

    for hp in range(HGRN_HP):
        for d in range(2):
            st_ref[hp, d] = s0_ref[d, hp].T if latent else jnp.zeros((HEAD_DIM, HEAD_DIM), F32)

    def chunk_rows(ci):
        return (pl.ds(pl.multiple_of(ci * C, C), C),
                pl.ds(pl.multiple_of((n_chunks - 1 - ci) * C, C), C))

    def prepare(ci, slot):
        rows = chunk_rows(ci)
        for hp, lanes in enumerate(head_lanes):
            for d, f_ref in enumerate((ff_ref, fb_ref)):
                pre = _hgrn_prepare(q_ref[rows[d], lanes], f_ref[rows[d], lanes], lbs[hp][d], d == 1)
                for n, val in enumerate(pre):
                    pre_ref[slot, 2 * hp + d, n] = val

    prepare(0, 0)

    def body(ci, carry):
        slot = ci % 2
        rows = chunk_rows(ci)
        for hp, lanes in enumerate(head_lanes):
            for d, acc_ref in enumerate((of_ref, ob_ref)):
                qs, b, c = (pre_ref[slot, 2 * hp + d, n] for n in range(3))
                o, st = _hgrn_chunk(qs, b, c, v_ref[rows[d], lanes], st_ref[hp, d], d == 1)
                acc_ref[rows[d], lanes] = o
                st_ref[hp, d] = st
        prepare(jnp.minimum(ci + 1, n_chunks - 1), 1 - slot)
        return carry

    lax.fori_loop(0, n_chunks, body, 0)

    for hp, lanes in enumerate(head_lanes):
        if s_ref is not None:
            for d in range(2):
                s_ref[d, hp] = st_ref[hp, d].T
        o = of_ref[:, lanes] + ob_ref[:, lanes]
        o_ref[:, lanes] = (_rms(o, ong_ref[...]) * _silu(g_ref[:, lanes])).astype(o_ref.dtype)


def _hgrn(proj, lb_raw, onorm_g_l, layer, *, state0=None, o_prev=None, s_prev=None):
    latent = state0 is not None
    n_batch, seq, row0 = (DEC_BATCH, DEC_SEQ, N_CTX // DEC_SEQ) if latent else (BATCH, SEQ, 0)
    width = HGRN_HP * HEAD_DIM

    def col(block):
        return pl.BlockSpec((seq, width), lambda b, h: (row0 + b, block // HGRN_HP + h))

    in_specs = [
        pl.BlockSpec((2 * DEPTH, width), lambda b, h: (0, h)),
        col(COL_HQ), col(COL_HI), col(COL_HG), col(COL_HFF), col(COL_HFB),
        pl.BlockSpec((1, HEAD_DIM), lambda b, h: (0, 0)),
    ]
    args = [lb_raw.reshape(2 * DEPTH, HGRN_W), proj, proj, proj, proj, proj,
            onorm_g_l.reshape(1, HEAD_DIM)]
    o_shape = jax.ShapeDtypeStruct((N_ROWS, HGRN_W), BF16)
    o_spec = pl.BlockSpec((seq, width), lambda b, h: (row0 + b, h))
    aliases = {}
    if latent:
        in_specs += [pl.BlockSpec((None, None, 2, HGRN_HP, HEAD_DIM, HEAD_DIM),
                                  lambda b, h: (b, layer, 0, h, 0, 0)), _ANY]
        args += [state0, o_prev]
        aliases = {len(args) - 1: 0}
        out_shape, out_specs = o_shape, o_spec
    else:
        if s_prev is not None:
            in_specs.append(_ANY)
            args.append(s_prev)
            aliases = {len(args) - 1: 1}
        out_shape = (o_shape, jax.ShapeDtypeStruct(
            (BATCH, DEPTH, 2, HGRN_HEADS, HEAD_DIM, HEAD_DIM), F32))
        out_specs = (o_spec, pl.BlockSpec((None, None, 2, HGRN_HP, HEAD_DIM, HEAD_DIM),
                                          lambda b, h: (b, layer, 0, h, 0, 0)))
    return pl.pallas_call(
        functools.partial(_hgrn_kernel, layer=layer, seq=seq, latent=latent, n_alias=len(aliases)),
        out_shape=out_shape,
        grid=(n_batch, HGRN_HEADS // HGRN_HP),
        in_specs=in_specs,
        out_specs=out_specs,
        scratch_shapes=[pltpu.VMEM((seq, width), F32), pltpu.VMEM((seq, width), F32),
                        pltpu.VMEM((HGRN_HP, 2, HEAD_DIM, HEAD_DIM), F32),
                        pltpu.VMEM((2, 2 * HGRN_HP, 3, HGRN_CHUNK, HEAD_DIM), F32)],
        input_output_aliases=aliases,
        compiler_params=_params("parallel", "parallel"),
        name=f"hgrn_l{layer}_{'lat' if latent else 'ctx'}",
    )(*args)


def _swap_pairs(x, width):
    lanes = x.shape[-1]
    lane = lax.broadcasted_iota(jnp.int32, x.shape, x.ndim - 1)
    from_right = pltpu.roll(x, lanes - width, x.ndim - 1)
    from_left = pltpu.roll(x, width, x.ndim - 1)
    return jnp.where(lane % (2 * width) < width, from_right, from_left)


def _rope(x, cos, sin_signed, quarter):
    return x * cos + _swap_pairs(x, quarter) * sin_signed


def _rms_halves(x, g):
    half = x.shape[-1] // 2
    lane = lax.broadcasted_iota(jnp.int32, x.shape, x.ndim - 1)
    lo = lane < half
    sq = x * x
    ms_lo = jnp.sum(jnp.where(lo, sq, 0.0), axis=-1, keepdims=True) / half
    ms_hi = jnp.sum(jnp.where(lo, 0.0, sq), axis=-1, keepdims=True) / half
    ms = jnp.where(lo, ms_lo, ms_hi)
    return (x * lax.rsqrt(ms + EPS)) * g


def _gqa_kernel(*refs, seq, latent, n_alias):
    qg_ref, kg_ref = refs[:2]
    q_refs = refs[2:2 + GQA_GROUP]
    k_ref, v_ref = refs[2 + GQA_GROUP:4 + GQA_GROUP]
    rest = refs[4 + GQA_GROUP:]
    if latent:
        ck_ref, cv_ref, cos_ref, sin_ref = rest[:4]
        o_ref, kt_ref, vt_ref = rest[4 + n_alias:]
    else:
        o_ref, kn_ref, vo_ref, kt_ref, vt_ref = rest[n_alias:]
    past = PAST_LEN if latent else 0
    quarter = HEAD_DIM // 4

    kn = _rms(k_ref[...], kg_ref[...])
    v = v_ref[...]
    if latent:
        kn = _rope(kn, cos_ref[...], sin_ref[...], quarter)
        kt_ref[0:past, :] = ck_ref[...].astype(BF16)
        vt_ref[0:past, :] = cv_ref[...].astype(BF16)
    else:
        kn_ref[...] = kn
        vo_ref[...] = v
    kt_ref[past:past + seq, :] = kn.astype(BF16)
    vt_ref[past:past + seq, :] = v.astype(BF16)

    q_scale = HEAD_DIM ** -0.5 * LOG2E

    def attend(q_rows):
        s = _dot_nt(jnp.concatenate(q_rows, axis=0), kt_ref[...])
        p = jnp.exp2(s - jnp.max(s, axis=-1, keepdims=True))
        inv = 1.0 / jnp.sum(p, axis=-1, keepdims=True)
        o = (_dot(p.astype(BF16), vt_ref[...]) * inv).astype(o_ref.dtype)
        n = q_rows[0].shape[0]
        return [o[i * n:(i + 1) * n, :] for i in range(len(q_rows))]

    def query(q_ref, rows):
        qn = _rms(q_ref[rows, :], qg_ref[...])
        if latent:
            qn = _rope(qn, cos_ref[rows, :], sin_ref[rows, :], quarter)
        return (qn * q_scale).astype(BF16)

    head_lanes = [slice(g * HEAD_DIM, (g + 1) * HEAD_DIM) for g in range(GQA_GROUP)]
    if seq <= GQA_Q_BLOCK:
        rows = slice(0, seq)
        outs = attend([query(q_ref, rows) for q_ref in q_refs])
        for lanes, o in zip(head_lanes, outs):
            o_ref[rows, lanes] = o
    else:
        for lanes, q_ref in zip(head_lanes, q_refs):
            for r in range(seq // GQA_Q_BLOCK):
                rows = slice(r * GQA_Q_BLOCK, (r + 1) * GQA_Q_BLOCK)
                o_ref[rows, lanes] = attend([query(q_ref, rows)])[0]


def _gqa(proj, qnorm_g, knorm_g, layer, *, latent_args=None, kv_prev=None):
    latent = latent_args is not None
    n_batch, seq, row0 = (DEC_BATCH, DEC_SEQ, N_CTX // DEC_SEQ) if latent else (BATCH, SEQ, 0)
    past = PAST_LEN if latent else 0

    def vec():
        return pl.BlockSpec((1, HEAD_DIM), lambda b, h: (0, 0))

    def q_spec(g):
        return pl.BlockSpec((seq, HEAD_DIM), lambda b, h: (row0 + b, COL_GQ + h * GQA_GROUP + g))

    in_specs = [vec(), vec()] + [q_spec(g) for g in range(GQA_GROUP)] + [
        pl.BlockSpec((seq, HEAD_DIM), lambda b, h: (row0 + b, COL_GK + h)),
        pl.BlockSpec((seq, HEAD_DIM), lambda b, h: (row0 + b, COL_GV + h)),
    ]
    args = ([qnorm_g.reshape(1, HEAD_DIM), knorm_g.reshape(1, HEAD_DIM)]
            + [proj] * (GQA_GROUP + 2))
    o_spec = pl.BlockSpec((seq, GQA_GROUP * HEAD_DIM), lambda b, h: (row0 + b, h))
    o_shape = jax.ShapeDtypeStruct((N_ROWS, GQA_W), BF16)
    cache_spec = pl.BlockSpec((None, None, SEQ, HEAD_DIM), lambda b, h: (b, layer, 0, h))
    aliases = {}
    if latent:
        cache_k, cache_v, cos, sin_signed, o_prev = latent_args
        table_spec = pl.BlockSpec((seq, HEAD_DIM), lambda b, h: (0, 0))
        in_specs += [cache_spec, cache_spec, table_spec, table_spec, _ANY]
        args += [cache_k.reshape(DEC_BATCH, DEPTH, PAST_LEN, GQA_KV_W),
                 cache_v.reshape(DEC_BATCH, DEPTH, PAST_LEN, GQA_KV_W), cos, sin_signed, o_prev]
        aliases = {len(args) - 1: 0}
        out_shape, out_specs = o_shape, o_spec
    else:
        if kv_prev is not None:
            in_specs += [_ANY, _ANY]
            args += list(kv_prev)
            aliases = {len(args) - 2: 1, len(args) - 1: 2}
        new_shape = jax.ShapeDtypeStruct((BATCH, DEPTH, SEQ, GQA_KV_W), F32)
        out_shape = (o_shape, new_shape, new_shape)
        out_specs = (o_spec, cache_spec, cache_spec)
    return pl.pallas_call(
        functools.partial(_gqa_kernel, seq=seq, latent=latent, n_alias=len(aliases)),
        out_shape=out_shape,
        grid=(n_batch, GQA_KV_HEADS),
        in_specs=in_specs,
        out_specs=out_specs,
        scratch_shapes=[pltpu.VMEM((past + seq, HEAD_DIM), BF16),
                        pltpu.VMEM((past + seq, HEAD_DIM), BF16)],
        input_output_aliases=aliases,
        compiler_params=_params("parallel", "parallel"),
        name=f"gqa_l{layer}_{'lat' if latent else 'ctx'}",
    )(*args)


def _diff_kernel(*refs, seq, latent, lam_init, n_alias, heads):
    qg_ref, kg_ref, sg_ref, lam_ref, q_ref, k_ref, v_ref = refs[:7]
    if latent:
        ck_ref, cv_ref, cos_ref, sin_ref = refs[7:11]
        o_ref, kt_ref, vt_ref = refs[11 + n_alias:]
    else:
        o_ref, kn_ref, vo_ref, kt_ref, vt_ref = refs[7 + n_alias:]
    past = PAST_LEN if latent else 0
    quarter = DIFF_QK_DIM // 4
    head_lanes = [slice(hd * HEAD_DIM, (hd + 1) * HEAD_DIM) for hd in range(heads)]

    for lanes in head_lanes:
        kn = _rms_halves(k_ref[:, lanes], kg_ref[...])
        v = v_ref[:, lanes]
        if latent:
            kn = _rope(kn, cos_ref[...], sin_ref[...], quarter)
            kt_ref[0:past, lanes] = ck_ref[:, lanes].astype(BF16)
            vt_ref[0:past, lanes] = cv_ref[:, lanes].astype(BF16)
        else:
            kn_ref[:, lanes] = kn
            vo_ref[:, lanes] = v
        kt_ref[past:past + seq, lanes] = kn.astype(BF16)
        vt_ref[past:past + seq, lanes] = v.astype(BF16)

    lv = lam_ref[...]
    lam = (jnp.exp(jnp.sum(lv[0:1, :] * lv[1:2, :], axis=-1, keepdims=True))
           - jnp.exp(jnp.sum(lv[2:3, :] * lv[3:4, :], axis=-1, keepdims=True)) + lam_init)
    q_scale = DIFF_QK_DIM ** -0.5 * LOG2E

    q_block = min(seq, Q_BLOCK)
    for lanes in head_lanes:
        for r in range(seq // q_block):
            rows = slice(r * q_block, (r + 1) * q_block)
            qn = _rms_halves(q_ref[rows, lanes], qg_ref[...])
            if latent:
                qn = _rope(qn, cos_ref[rows, :], sin_ref[rows, :], quarter)
            qn = qn * q_scale
            lo = lax.broadcasted_iota(jnp.int32, qn.shape, 1) < DIFF_QK_DIM
            kt = kt_ref[:, lanes]
            s1 = _dot_nt(jnp.where(lo, qn, 0.0).astype(BF16), kt)
            s2 = _dot_nt(jnp.where(lo, 0.0, qn).astype(BF16), kt)
            p1 = jnp.exp2(s1 - jnp.max(s1, axis=-1, keepdims=True))
            p2 = jnp.exp2(s2 - jnp.max(s2, axis=-1, keepdims=True))
            w1 = 1.0 / jnp.sum(p1, axis=-1, keepdims=True)
            w2 = lam / jnp.sum(p2, axis=-1, keepdims=True)
            a = p1 * w1 - p2 * w2
            o = _dot(a.astype(BF16), vt_ref[:, lanes])
            o = _rms(o, sg_ref[...]) * (1.0 - lam_init)
            o_ref[rows, lanes] = o.astype(o_ref.dtype)


def _diff(proj, qnorm_g, knorm_g, subln_g, lam_params, layer, *, latent_args=None, kv_prev=None):
    latent = latent_args is not None
    n_batch, seq, row0 = (DEC_BATCH, DEC_SEQ, N_CTX // DEC_SEQ) if latent else (BATCH, SEQ, 0)
    past = PAST_LEN if latent else 0
    lam_init = 0.8 - 0.6 * math.exp(-0.3 * layer)
    heads = DIFF_HP_LAT if latent else DIFF_HP_CTX
    width = heads * HEAD_DIM

    def vec():
        return pl.BlockSpec((1, HEAD_DIM), lambda b, h: (0, 0))

    def col(block):
        return pl.BlockSpec((seq, width), lambda b, h: (row0 + b, block // heads + h))

    in_specs = [
        vec(), vec(), vec(),
        pl.BlockSpec((4, DIFF_QK_DIM), lambda b, h: (0, 0)),
        col(COL_DQ), col(COL_DK), col(COL_DV),
    ]
    args = [jnp.tile(qnorm_g, 2).reshape(1, HEAD_DIM), jnp.tile(knorm_g, 2).reshape(1, HEAD_DIM),
            subln_g.reshape(1, HEAD_DIM), lam_params, proj, proj, proj]
    o_spec = pl.BlockSpec((seq, width), lambda b, h: (row0 + b, h))
    o_shape = jax.ShapeDtypeStruct((N_ROWS, DIFF_W), BF16)
    cache_spec = pl.BlockSpec((None, None, SEQ, width), lambda b, h: (b, layer, 0, h))
    aliases = {}
    if latent:
        cache_k, cache_v, cos, sin_signed, o_prev = latent_args
        table_spec = pl.BlockSpec((seq, HEAD_DIM), lambda b, h: (0, 0))
        in_specs += [cache_spec, cache_spec, table_spec, table_spec, _ANY]
        args += [cache_k.reshape(DEC_BATCH, DEPTH, PAST_LEN, DIFF_W),
                 cache_v.reshape(DEC_BATCH, DEPTH, PAST_LEN, DIFF_W), cos, sin_signed, o_prev]
        aliases = {len(args) - 1: 0}
        out_shape, out_specs = o_shape, o_spec
    else:
        if kv_prev is not None:
            in_specs += [_ANY, _ANY]
            args += list(kv_prev)
            aliases = {len(args) - 2: 1, len(args) - 1: 2}
        new_shape = jax.ShapeDtypeStruct((BATCH, DEPTH, SEQ, DIFF_W), F32)
        out_shape = (o_shape, new_shape, new_shape)
        out_specs = (o_spec, cache_spec, cache_spec)
    return pl.pallas_call(
        functools.partial(_diff_kernel, seq=seq, latent=latent, lam_init=lam_init,
                          n_alias=len(aliases), heads=heads),
        out_shape=out_shape,
        grid=(n_batch, DIFF_HEADS // heads),
        in_specs=in_specs,
        out_specs=out_specs,
        scratch_shapes=[pltpu.VMEM((past + seq, width), BF16),
                        pltpu.VMEM((past + seq, width), BF16)],
        input_output_aliases=aliases,
        compiler_params=_params("parallel", "parallel"),
        name=f"diff_l{layer}_{'lat' if latent else 'ctx'}",
    )(*args)


def _rope_tables(n_tokens, dim, repeat):
    quarter = dim // 4
    t = jnp.arange(n_tokens)
    pos = jnp.stack([t // GRID_W, t % GRID_W], axis=-1).astype(F32)
    inv = ROPE_BASE ** (-jnp.arange(quarter, dtype=F32) / quarter)
    ang = pos[:, :, None] * inv
    cos, sin = jnp.cos(ang), jnp.sin(ang)
    cos_l = jnp.concatenate([cos, cos], axis=-1).reshape(n_tokens, dim)
    sin_l = jnp.concatenate([-sin, sin], axis=-1).reshape(n_tokens, dim)
    return jnp.tile(cos_l, (1, repeat)), jnp.tile(sin_l, (1, repeat))


def kernel(x_prompt, x_sample, c, cache_gqa_k, cache_gqa_v, cache_diff_k, cache_diff_v, state_hgrn,
           c_ctx, w_mod, b_mod, norm_g, ffn_w_gate, ffn_w_up, ffn_w_down, w_in, w_out, hgrn_lb_raw,
           hgrn_onorm_g, gqa_qnorm_g, gqa_knorm_g, diff_qnorm_g, diff_knorm_g, diff_lambda,
           diff_subln_g):
    cond = jnp.concatenate(
        [c_ctx[None, :], c, jnp.zeros((COND_PAD - N_COND, D_MODEL), F32)], axis=0)
    mod = _modulation(cond, w_mod, b_mod)

    cos_g, sin_g = _rope_tables(DEC_SEQ, HEAD_DIM, 1)
    cos_d, sin_d = _rope_tables(DEC_SEQ, DIFF_QK_DIM, 2)
    cache_dk = cache_diff_k.reshape(DEC_BATCH, DEPTH, PAST_LEN, DIFF_HEADS, 2 * DIFF_QK_DIM)
    ctx_rows = dict(n_tiles=CTX_TILES, tile0=0)
    lat_rows = dict(n_tiles=LAT_TILES, tile0=CTX_TILES)

    x = None
    states = gqa_kv = diff_kv = None
    for l in range(DEPTH):
        ffn_w = (mod[l], norm_g[l], ffn_w_gate, ffn_w_up, ffn_w_down, l)
        if l == 0:
            x = _ffn(x_prompt.reshape(N_CTX, D_MODEL), *ffn_w, 0, **ctx_rows)
            x = _ffn(x_sample.reshape(N_LAT, D_MODEL), *ffn_w, 0, **lat_rows,
                     out_off=CTX_TILES, o_prev=x)
        else:
            x = _ffn(x, *ffn_w, 0)
        proj = _proj_in(x, mod[l], norm_g[l], w_in, l)

        o_h, states = _hgrn(proj, hgrn_lb_raw, hgrn_onorm_g[l], l, s_prev=states)
        o_h = _hgrn(proj, hgrn_lb_raw, hgrn_onorm_g[l], l, state0=state_hgrn, o_prev=o_h)

        gqa_w = (proj, gqa_qnorm_g[l], gqa_knorm_g[l], l)
        o_g, *gqa_kv = _gqa(*gqa_w, kv_prev=gqa_kv)
        o_g = _gqa(*gqa_w, latent_args=(cache_gqa_k, cache_gqa_v, cos_g, sin_g, o_g))

        diff_w = (proj, diff_qnorm_g[l], diff_knorm_g[l], diff_subln_g[l], diff_lambda[l], l)
        o_d, *diff_kv = _diff(*diff_w, kv_prev=diff_kv)
        o_d = _diff(*diff_w, latent_args=(cache_dk, cache_diff_v, cos_d, sin_d, o_d))

        x = _proj_out(x, mod[l], o_h, o_g, o_d, w_out, l)
        if l < DEPTH - 1:
            x = _ffn(x, *ffn_w, 1)
        else:
            y_prompt = _ffn(x, *ffn_w, 1, **ctx_rows, out_rows=N_CTX)
            y_sample = _ffn(x, *ffn_w, 1, **lat_rows, in_off=CTX_TILES, out_rows=N_LAT)

    return (y_prompt.reshape(BATCH, SEQ, D_MODEL), y_sample.reshape(DEC_BATCH, DEC_SEQ, D_MODEL),
            gqa_kv[0].reshape(BATCH, DEPTH, SEQ, GQA_KV_HEADS, HEAD_DIM),
            gqa_kv[1].reshape(BATCH, DEPTH, SEQ, GQA_KV_HEADS, HEAD_DIM),
            diff_kv[0].reshape(BATCH, DEPTH, SEQ, DIFF_HEADS, 2, DIFF_QK_DIM),
            diff_kv[1].reshape(BATCH, DEPTH, SEQ, DIFF_HEADS, HEAD_DIM),
            states)
```

```python
import functools
import math

import jax
import jax.numpy as jnp
from jax import lax
from jax.experimental import pallas as pl
from jax.experimental.pallas import tpu as pltpu

F32 = jnp.float32
BF16 = jnp.bfloat16

D_MODEL = 2048
BATCH = 16
SEQ = 256
DEPTH = 2
DEC_BATCH = 2
DEC_SEQ = 1024
PAST_LEN = 256
GRID_W = 64
HEAD_DIM = 128
HGRN_HEADS = 4
GQA_Q_HEADS = 6
GQA_KV_HEADS = 2
GQA_GROUP = GQA_Q_HEADS // GQA_KV_HEADS
DIFF_HEADS = 6
DIFF_QK_DIM = 64
FFN_DIM = 5632
N_MOD = 9
IN_WIDTH = 6144
ROPE_BASE = 10000.0
EPS = 1e-6
LOG2E = math.log2(math.e)

N_CTX = BATCH * SEQ
N_LAT = DEC_BATCH * DEC_SEQ
N_ROWS = N_CTX + N_LAT
N_COND = 1 + DEC_BATCH
COND_PAD = 8

COL_HQ, COL_HI, COL_HG, COL_HFF, COL_HFB = 0, 4, 8, 12, 16
COL_GQ, COL_GK, COL_GV = 20, 26, 28
COL_DQ, COL_DK, COL_DV = 30, 36, 42

HGRN_W = HGRN_HEADS * HEAD_DIM
GQA_W = GQA_Q_HEADS * HEAD_DIM
GQA_KV_W = GQA_KV_HEADS * HEAD_DIM
DIFF_W = DIFF_HEADS * HEAD_DIM

VMEM_LIMIT = 60 * 1024 * 1024

TM = 1024
ROW_CHUNK = 128
FFN_TF = 256
FFN_TILES_PER_STEP = 2
FFN_NB = 512
IN_TN = 512
IN_TILES_PER_STEP = 2
OUT_TN = 512
OUT_TILES_PER_STEP = 2
MOD_TN = 1024
HGRN_CHUNK = 64
HGRN_SUB = 8
HGRN_HP = 4
Q_BLOCK = 512
GQA_Q_BLOCK = 256
DIFF_HP_CTX = 6
DIFF_HP_LAT = 2

CTX_TILES = N_CTX // TM
LAT_TILES = N_LAT // TM
ALL_TILES = CTX_TILES + LAT_TILES


def _cond_of_tile(i):
    tiles_per_latent = DEC_SEQ // TM
    return jnp.where(i < CTX_TILES, 0, 1 + (i - CTX_TILES) // tiles_per_latent)


def _silu(x):
    return x * jax.nn.sigmoid(x)


def _dot(a, b):
    return jnp.dot(a, b, preferred_element_type=F32)


def _dot_nt(a, b):
    return lax.dot_general(a, b, (((1,), (1,)), ((), ())), preferred_element_type=F32)


def _dot_tn(a, b):
    return lax.dot_general(a, b, (((0,), (0,)), ((), ())), preferred_element_type=F32)


def _rms(x, g):
    ms = jnp.mean(x * x, axis=-1, keepdims=True)
    return (x * lax.rsqrt(ms + EPS)) * g


def _params(*semantics):
    return pltpu.CompilerParams(dimension_semantics=semantics, vmem_limit_bytes=VMEM_LIMIT)


_ANY = pl.BlockSpec(memory_space=pl.ANY)


def _mod_kernel(cond_ref, w_ref, b_ref, o_ref):
    a = _silu(cond_ref[...]).astype(BF16)
    o_ref[...] = _dot(a, w_ref[...].astype(BF16)) + b_ref[...]


def _modulation(cond, w_mod, b_mod):
    width = N_MOD * D_MODEL
    out = pl.pallas_call(
        _mod_kernel,
        out_shape=jax.ShapeDtypeStruct((DEPTH, COND_PAD, width), F32),
        grid=(DEPTH, width // MOD_TN),
        in_specs=[
            pl.BlockSpec((COND_PAD, D_MODEL), lambda l, j: (0, 0)),
            pl.BlockSpec((None, D_MODEL, MOD_TN), lambda l, j: (l, 0, j)),
            pl.BlockSpec((None, 1, MOD_TN), lambda l, j: (l, 0, j)),
        ],
        out_specs=pl.BlockSpec((None, COND_PAD, MOD_TN), lambda l, j: (l, 0, j)),
        compiler_params=_params("parallel", "parallel"),
        name="modulation",
    )(cond, w_mod, b_mod.reshape(DEPTH, 1, width))
    return out[:, :N_COND].reshape(DEPTH, N_COND, N_MOD, D_MODEL)


def _mod_norm_into(x_ref, mod_ref, g_ref, h_ref, sub, row0=0):
    shift = mod_ref[3 * sub:3 * sub + 1, :]
    gain = g_ref[sub:sub + 1, :] * (1.0 + mod_ref[3 * sub + 1:3 * sub + 2, :])

    def body(r, carry):
        rows = pl.ds(pl.multiple_of(row0 + r * ROW_CHUNK, ROW_CHUNK), ROW_CHUNK)
        h_ref[rows, :] = (_rms(x_ref[rows, :], gain) + shift).astype(BF16)
        return carry

    lax.fori_loop(0, TM // ROW_CHUNK, body, 0)


def _ffn_kernel(x_ref, mod_ref, g_ref, wg_ref, wu_ref, wd_ref, *rest, sub, tile0):
    o_ref, h_ref = rest[-2:]
    j = pl.program_id(1)
    conds = [_cond_of_tile(tile0 + pl.program_id(0) * FFN_TILES_PER_STEP + part)
             for part in range(FFN_TILES_PER_STEP)]

    @pl.when(j == 0)
    def _():
        for part, cond in enumerate(conds):
            _mod_norm_into(x_ref, mod_ref.at[cond], g_ref, h_ref, sub, row0=part * TM)
        o_ref[...] = jnp.zeros_like(o_ref)

    for part in range(FFN_TILES_PER_STEP):
        rows = slice(part * TM, (part + 1) * TM)
        h = h_ref[rows, :]
        gate_act = _dot(h, wg_ref[...].astype(BF16))
        up = _dot(h, wu_ref[...].astype(BF16))
        a = (_silu(gate_act) * up).astype(BF16)
        for n in range(0, D_MODEL, FFN_NB):
            o_ref[rows, n:n + FFN_NB] += _dot(a, wd_ref[:, n:n + FFN_NB].astype(BF16))

    @pl.when(j == pl.num_programs(1) - 1)
    def _():
        for part, cond in enumerate(conds):
            gate = mod_ref[cond, 3 * sub + 2:3 * sub + 3, :]

            def body(r, carry):
                rows = pl.ds(pl.multiple_of(part * TM + r * ROW_CHUNK, ROW_CHUNK), ROW_CHUNK)
                o_ref[rows, :] = x_ref[rows, :] + gate * (0.5 * o_ref[rows, :])
                return carry

            lax.fori_loop(0, TM // ROW_CHUNK, body, 0)


def _ffn(x, mod_l, norm_g_l, w_gate, w_up, w_down, layer, which, *, n_tiles=ALL_TILES, in_off=0,
         out_off=0, tile0=0, out_rows=N_ROWS, o_prev=None):
    sub = 2 * which
    per = FFN_TILES_PER_STEP
    rows = per * TM
    assert n_tiles % per == 0 and in_off % per == 0 and out_off % per == 0
    in_specs = [
        pl.BlockSpec((rows, D_MODEL), lambda i, j: (in_off // per + i, 0), pipeline_mode=pl.Buffered(1)),
        pl.BlockSpec((N_COND, N_MOD, D_MODEL), lambda i, j: (0, 0, 0)),
        pl.BlockSpec((3, D_MODEL), lambda i, j: (0, 0)),
        pl.BlockSpec((None, None, D_MODEL, FFN_TF), lambda i, j: (layer, which, 0, j)),
        pl.BlockSpec((None, None, D_MODEL, FFN_TF), lambda i, j: (layer, which, 0, j)),
        pl.BlockSpec((None, None, FFN_TF, D_MODEL), lambda i, j: (layer, which, j, 0)),
    ]
    args = [x, mod_l, norm_g_l, w_gate, w_up, w_down]
    aliases = {}
    if o_prev is not None:
        in_specs.append(_ANY)
        args.append(o_prev)
        aliases = {len(args) - 1: 0}
    return pl.pallas_call(
        functools.partial(_ffn_kernel, sub=sub, tile0=tile0),
        out_shape=jax.ShapeDtypeStruct((out_rows, D_MODEL), F32),
        grid=(n_tiles // per, FFN_DIM // FFN_TF),
        in_specs=in_specs,
        out_specs=pl.BlockSpec((rows, D_MODEL), lambda i, j: (out_off // per + i, 0),
                               pipeline_mode=pl.Buffered(1)),
        scratch_shapes=[pltpu.VMEM((rows, D_MODEL), BF16)],
        input_output_aliases=aliases,
        compiler_params=_params("parallel", "arbitrary"),
        name=f"ffn_l{layer}_h{which}_t{tile0}n{n_tiles}",
    )(*args)


def _proj_in_kernel(x_ref, mod_ref, g_ref, w_ref, o_ref, h_ref):
    @pl.when(pl.program_id(1) == 0)
    def _():
        for part in range(IN_TILES_PER_STEP):
            cond = _cond_of_tile(pl.program_id(0) * IN_TILES_PER_STEP + part)
            _mod_norm_into(x_ref, mod_ref.at[cond], g_ref, h_ref, 1, row0=part * TM)

    o_ref[...] = _dot(h_ref[...], w_ref[...].astype(BF16))


def _proj_in(x, mod_l, norm_g_l, w_in, layer):
    rows = IN_TILES_PER_STEP * TM
    return pl.pallas_call(
        _proj_in_kernel,
        out_shape=jax.ShapeDtypeStruct((N_ROWS, IN_WIDTH), F32),
        grid=(N_ROWS // rows, IN_WIDTH // IN_TN),
        in_specs=[
            pl.BlockSpec((rows, D_MODEL), lambda i, j: (i, 0), pipeline_mode=pl.Buffered(1)),
            pl.BlockSpec((N_COND, N_MOD, D_MODEL), lambda i, j: (0, 0, 0)),
            pl.BlockSpec((3, D_MODEL), lambda i, j: (0, 0)),
            pl.BlockSpec((None, D_MODEL, IN_TN), lambda i, j: (layer, 0, j)),
        ],
        out_specs=pl.BlockSpec((rows, IN_TN), lambda i, j: (i, j)),
        scratch_shapes=[pltpu.VMEM((rows, D_MODEL), BF16)],
        compiler_params=_params("parallel", "arbitrary"),
        name=f"proj_in_l{layer}",
    )(x, mod_l, norm_g_l, w_in)


def _proj_out_kernel(x_ref, mod_ref, oh_ref, og_ref, od_ref, w_ref, o_ref):
    m = _dot(oh_ref[...], w_ref[0:HGRN_W, :].astype(BF16))
    m += _dot(og_ref[...], w_ref[HGRN_W:HGRN_W + GQA_W, :].astype(BF16))
    m += _dot(od_ref[...], w_ref[HGRN_W + GQA_W:, :].astype(BF16))
    for part in range(OUT_TILES_PER_STEP):
        rows = slice(part * TM, (part + 1) * TM)
        cond = _cond_of_tile(pl.program_id(0) * OUT_TILES_PER_STEP + part)
        o_ref[rows, :] = x_ref[rows, :] + mod_ref[cond, 5:6, :] * m[rows, :]


def _proj_out(x, mod_l, o_h, o_g, o_d, w_out, layer):
    rows = OUT_TILES_PER_STEP * TM
    return pl.pallas_call(
        _proj_out_kernel,
        out_shape=jax.ShapeDtypeStruct((N_ROWS, D_MODEL), F32),
        grid=(N_ROWS // rows, D_MODEL // OUT_TN),
        in_specs=[
            pl.BlockSpec((rows, OUT_TN), lambda i, j: (i, j)),
            pl.BlockSpec((N_COND, N_MOD, OUT_TN), lambda i, j: (0, 0, j)),
            pl.BlockSpec((rows, HGRN_W), lambda i, j: (i, 0)),
            pl.BlockSpec((rows, GQA_W), lambda i, j: (i, 0)),
            pl.BlockSpec((rows, DIFF_W), lambda i, j: (i, 0)),
            pl.BlockSpec((None, D_MODEL, OUT_TN), lambda i, j: (layer, 0, j)),
        ],
        out_specs=pl.BlockSpec((rows, OUT_TN), lambda i, j: (i, j)),
        compiler_params=_params("parallel", "parallel"),
        name=f"proj_out_l{layer}",
    )(x, mod_l, o_h, o_g, o_d, w_out)


def _log2_forget_and_key(z, lb):
    z2 = z * LOG2E
    soft = jnp.log2(1.0 + jnp.exp2(-jnp.abs(z2)))
    log_1mlb = jnp.log1p(-lb) * LOG2E
    a = jnp.log2(lb)
    c = log_1mlb + (jnp.minimum(z2, 0.0) - soft)
    log_f = jnp.maximum(a, c) + jnp.log2(1.0 + jnp.exp2(-jnp.abs(a - c)))
    log_k = log_1mlb + (jnp.minimum(-z2, 0.0) - soft)
    return log_f, log_k


def _cumsum_rows(x, reverse):
    tile = HGRN_SUB
    row = lax.broadcasted_iota(jnp.int32, (tile, 1), 0)
    tiles = []
    for j in range(x.shape[0] // tile):
        y = x[j * tile:(j + 1) * tile, :]
        for sh in (1, 2, 4):
            if reverse:
                y = y + jnp.where(row < tile - sh, pltpu.roll(y, tile - sh, 0), 0.0)
            else:
                y = y + jnp.where(row >= sh, pltpu.roll(y, sh, 0), 0.0)
        tiles.append(y)
    order = range(len(tiles) - 1, -1, -1) if reverse else range(len(tiles))
    carry = None
    for j in order:
        if carry is not None:
            tiles[j] = tiles[j] + carry
        carry = tiles[j][0:1, :] if reverse else tiles[j][tile - 1:tile, :]
    return jnp.concatenate(tiles, axis=0)


def _hgrn_prepare(q, z, lb, reverse):
    lf2, lk2 = _log2_forget_and_key(z, lb)
    b = _cumsum_rows(lf2, reverse)
    return _silu(q), b, b - lk2


def _hgrn_chunk(qs, b, c, v, st, reverse):
    C, SUB = HGRN_CHUNK, HGRN_SUB
    n_sub = C // SUB
    b_tot = b[0:1, :] if reverse else b[C - 1:C, :]

    o = _dot_nt((qs * jnp.exp2(b)).astype(BF16), st.astype(BF16))

    terms = []
    for i in range(n_sub):
        lo = i * SUB
        qi, bi, ci = qs[lo:lo + SUB, :], b[lo:lo + SUB, :], c[lo:lo + SUB, :]
        for s in range(SUB):
            terms.append(qi * jnp.exp2(bi - ci[s:s + 1, :]))
    k_sums = _dot(jnp.concatenate(terms, axis=0).astype(BF16), jnp.ones((HEAD_DIM, C), BF16))

    row = lax.broadcasted_iota(jnp.int32, (SUB, C), 0)
    lane = lax.broadcasted_iota(jnp.int32, (SUB, C), 1)
    lane_s = lane % SUB
    causal = (lane_s >= row) if reverse else (lane_s <= row)
    blocks = []
    for i in range(n_sub):
        lo, hi = i * SUB, (i + 1) * SUB
        diag = k_sums[lo * SUB:(lo + 1) * SUB, :]
        for s in range(1, SUB):
            diag = jnp.where(lane_s == s, k_sums[(lo + s) * SUB:(lo + s + 1) * SUB, :], diag)
        has_off = (i < n_sub - 1) if reverse else (i > 0)
        if has_off:
            ref = b[hi:hi + 1, :] if reverse else b[lo - 1:lo, :]
            qt = (qs[lo:hi, :] * jnp.exp2(b[lo:hi, :] - ref)).astype(BF16)
            if reverse:
                kt = jnp.concatenate([jnp.zeros((hi, HEAD_DIM), F32), jnp.exp2(ref - c[hi:, :])], axis=0)
            else:
                kt = jnp.concatenate([jnp.exp2(ref - c[:lo, :]), jnp.zeros((C - lo, HEAD_DIM), F32)], axis=0)
            off = _dot_nt(qt, kt.astype(BF16))
        else:
            off = jnp.zeros((SUB, C), F32)
        blocks.append(jnp.where((lane // SUB == i) & causal, diag, off))
    a = jnp.concatenate(blocks, axis=0)
    o = o + _dot(a.astype(BF16), v.astype(BF16))

    st_new = jnp.exp2(b_tot) * st + _dot_tn(v.astype(BF16), jnp.exp2(b_tot - c).astype(BF16))
    return o, st_new


def _hgrn_kernel(*refs, layer, seq, latent, n_alias):
    raw_ref, q_ref, v_ref, g_ref, ff_ref, fb_ref, ong_ref = refs[:7]
    s0_ref = refs[7] if latent else None
    outs = refs[7 + (1 if latent else 0) + n_alias:]
    if latent:
        o_ref, of_ref, ob_ref, st_ref, pre_ref = outs
        s_ref = None
    else:
        o_ref, s_ref, of_ref, ob_ref, st_ref, pre_ref = outs
    C = HGRN_CHUNK
    n_chunks = seq // C

    def lower_bound(d, lanes):
        rows = [raw_ref[2 * l + d:2 * l + d + 1, lanes] for l in range(DEPTH)]
        m = functools.reduce(jnp.maximum, rows)
        e = [jnp.exp(r - m) for r in rows]
        tot = functools.reduce(lambda x, y: x + y, e)
        lb = jnp.zeros_like(m)
        for l in range(1, layer + 1):
            lb = lb + e[l] / tot
        return lb

    head_lanes = [slice(hp * HEAD_DIM, (hp + 1) * HEAD_DIM) for hp in range(HGRN_HP)]
    lbs = [[lower_bound(d, lanes) for d in range(2)] for lanes in head_lanes]

    for hp in range(HGRN_HP):
        for d in range(2):
            st_ref[hp, d] = s0_ref[d, hp].T if latent else jnp.zeros((HEAD_DIM, HEAD_DIM), F32)

    def chunk_rows(ci):
        return (pl.ds(pl.multiple_of(ci * C, C), C),
                pl.ds(pl.multiple_of((n_chunks - 1 - ci) * C, C), C))

    def prepare(ci, slot):
        rows = chunk_rows(ci)
        for hp, lanes in enumerate(head_lanes):
            for d, f_ref in enumerate((ff_ref, fb_ref)):
                pre = _hgrn_prepare(q_ref[rows[d], lanes], f_ref[rows[d], lanes], lbs[hp][d], d == 1)
                for n, val in enumerate(pre):
                    pre_ref[slot, 2 * hp + d, n] = val

    prepare(0, 0)

    def body(ci, carry):
        slot = ci % 2
        rows = chunk_rows(ci)
        for hp, lanes in enumerate(head_lanes):
            for d, acc_ref in enumerate((of_ref, ob_ref)):
                qs, b, c = (pre_ref[slot, 2 * hp + d, n] for n in range(3))
                o, st = _hgrn_chunk(qs, b, c, v_ref[rows[d], lanes], st_ref[hp, d], d == 1)
                acc_ref[rows[d], lanes] = o
                st_ref[hp, d] = st
        prepare(jnp.minimum(ci + 1, n_chunks - 1), 1 - slot)
        return carry

    lax.fori_loop(0, n_chunks, body, 0)

    for hp, lanes in enumerate(head_lanes):
        if s_ref is not None:
            for d in range(2):
                s_ref[d, hp] = st_ref[hp, d].T
        o = of_ref[:, lanes] + ob_ref[:, lanes]
        o_ref[:, lanes] = (_rms(o, ong_ref[...]) * _silu(g_ref[:, lanes])).astype(o_ref.dtype)


def _hgrn(proj, lb_raw, onorm_g_l, layer, *, state0=None, o_prev=None, s_prev=None):
    latent = state0 is not None
    n_batch, seq, row0 = (DEC_BATCH, DEC_SEQ, N_CTX // DEC_SEQ) if latent else (BATCH, SEQ, 0)
    width = HGRN_HP * HEAD_DIM

    def col(block):
        return pl.BlockSpec((seq, width), lambda b, h: (row0 + b, block // HGRN_HP + h))

    in_specs = [
        pl.BlockSpec((2 * DEPTH, width), lambda b, h: (0, h)),
        col(COL_HQ), col(COL_HI), col(COL_HG), col(COL_HFF), col(COL_HFB),
        pl.BlockSpec((1, HEAD_DIM), lambda b, h: (0, 0)),
    ]
    args = [lb_raw.reshape(2 * DEPTH, HGRN_W), proj, proj, proj, proj, proj,
            onorm_g_l.reshape(1, HEAD_DIM)]
    o_shape = jax.ShapeDtypeStruct((N_ROWS, HGRN_W), BF16)
    o_spec = pl.BlockSpec((seq, width), lambda b, h: (row0 + b, h))
    aliases = {}
    if latent:
        in_specs += [pl.BlockSpec((None, None, 2, HGRN_HP, HEAD_DIM, HEAD_DIM),
                                  lambda b, h: (b, layer, 0, h, 0, 0)), _ANY]
        args += [state0, o_prev]
        aliases = {len(args) - 1: 0}
        out_shape, out_specs = o_shape, o_spec
    else:
        if s_prev is not None:
            in_specs.append(_ANY)
            args.append(s_prev)
            aliases = {len(args) - 1: 1}
        out_shape = (o_shape, jax.ShapeDtypeStruct(
            (BATCH, DEPTH, 2, HGRN_HEADS, HEAD_DIM, HEAD_DIM), F32))
        out_specs = (o_spec, pl.BlockSpec((None, None, 2, HGRN_HP, HEAD_DIM, HEAD_DIM),
                                          lambda b, h: (b, layer, 0, h, 0, 0)))
    return pl.pallas_call(
        functools.partial(_hgrn_kernel, layer=layer, seq=seq, latent=latent, n_alias=len(aliases)),
        out_shape=out_shape,
        grid=(n_batch, HGRN_HEADS // HGRN_HP),
        in_specs=in_specs,
        out_specs=out_specs,
        scratch_shapes=[pltpu.VMEM((seq, width), F32), pltpu.VMEM((seq, width), F32),
                        pltpu.VMEM((HGRN_HP, 2, HEAD_DIM, HEAD_DIM), F32),
                        pltpu.VMEM((2, 2 * HGRN_HP, 3, HGRN_CHUNK, HEAD_DIM), F32)],
        input_output_aliases=aliases,
        compiler_params=_params("parallel", "parallel"),
        name=f"hgrn_l{layer}_{'lat' if latent else 'ctx'}",
    )(*args)


def _swap_pairs(x, width):
    lanes = x.shape[-1]
    lane = lax.broadcasted_iota(jnp.int32, x.shape, x.ndim - 1)
    from_right = pltpu.roll(x, lanes - width, x.ndim - 1)
    from_left = pltpu.roll(x, width, x.ndim - 1)
    return jnp.where(lane % (2 * width) < width, from_right, from_left)


def _rope(x, cos, sin_signed, quarter):
    return x * cos + _swap_pairs(x, quarter) * sin_signed


def _rms_halves(x, g):
    half = x.shape[-1] // 2
    lane = lax.broadcasted_iota(jnp.int32, x.shape, x.ndim - 1)
    lo = lane < half
    sq = x * x
    ms_lo = jnp.sum(jnp.where(lo, sq, 0.0), axis=-1, keepdims=True) / half
    ms_hi = jnp.sum(jnp.where(lo, 0.0, sq), axis=-1, keepdims=True) / half
    ms = jnp.where(lo, ms_lo, ms_hi)
    return (x * lax.rsqrt(ms + EPS)) * g


def _gqa_kernel(*refs, seq, latent, n_alias):
    qg_ref, kg_ref = refs[:2]
    q_refs = refs[2:2 + GQA_GROUP]
    k_ref, v_ref = refs[2 + GQA_GROUP:4 + GQA_GROUP]
    rest = refs[4 + GQA_GROUP:]
    if latent:
        ck_ref, cv_ref, cos_ref, sin_ref = rest[:4]
        o_ref, kt_ref, vt_ref = rest[4 + n_alias:]
    else:
        o_ref, kn_ref, vo_ref, kt_ref, vt_ref = rest[n_alias:]
    past = PAST_LEN if latent else 0
    quarter = HEAD_DIM // 4

    kn = _rms(k_ref[...], kg_ref[...])
    v = v_ref[...]
    if latent:
        kn = _rope(kn, cos_ref[...], sin_ref[...], quarter)
        kt_ref[0:past, :] = ck_ref[...].astype(BF16)
        vt_ref[0:past, 0:HEAD_DIM] = cv_ref[...].astype(BF16)
    else:
        kn_ref[...] = kn
        vo_ref[...] = v
    kt_ref[past:past + seq, :] = kn.astype(BF16)
    vt_ref[past:past + seq, 0:HEAD_DIM] = v.astype(BF16)
    vt_ref[:, HEAD_DIM:] = jnp.ones((past + seq, HEAD_DIM), BF16)

    q_scale = HEAD_DIM ** -0.5 * LOG2E

    def attend(q_rows):
        s = _dot_nt(jnp.concatenate(q_rows, axis=0), kt_ref[...])
        p = jnp.exp2(s - jnp.max(s, axis=-1, keepdims=True))
        ov = _dot(p.astype(BF16), vt_ref[...])
        o = (ov[:, :HEAD_DIM] / ov[:, HEAD_DIM:]).astype(o_ref.dtype)
        n = q_rows[0].shape[0]
        return [o[i * n:(i + 1) * n, :] for i in range(len(q_rows))]

    def query(q_ref, rows):
        qn = _rms(q_ref[rows, :], qg_ref[...])
        if latent:
            qn = _rope(qn, cos_ref[rows, :], sin_ref[rows, :], quarter)
        return (qn * q_scale).astype(BF16)

    head_lanes = [slice(g * HEAD_DIM, (g + 1) * HEAD_DIM) for g in range(GQA_GROUP)]
    if seq <= GQA_Q_BLOCK:
        rows = slice(0, seq)
        outs = attend([query(q_ref, rows) for q_ref in q_refs])
        for lanes, o in zip(head_lanes, outs):
            o_ref[rows, lanes] = o
    else:
        for lanes, q_ref in zip(head_lanes, q_refs):
            for r in range(seq // GQA_Q_BLOCK):
                rows = slice(r * GQA_Q_BLOCK, (r + 1) * GQA_Q_BLOCK)
                o_ref[rows, lanes] = attend([query(q_ref, rows)])[0]


def _gqa(proj, qnorm_g, knorm_g, layer, *, latent_args=None, kv_prev=None):
    latent = latent_args is not None
    n_batch, seq, row0 = (DEC_BATCH, DEC_SEQ, N_CTX // DEC_SEQ) if latent else (BATCH, SEQ, 0)
    past = PAST_LEN if latent else 0

    def vec():
        return pl.BlockSpec((1, HEAD_DIM), lambda b, h: (0, 0))

    def q_spec(g):
        return pl.BlockSpec((seq, HEAD_DIM), lambda b, h: (row0 + b, COL_GQ + h * GQA_GROUP + g))

    in_specs = [vec(), vec()] + [q_spec(g) for g in range(GQA_GROUP)] + [
        pl.BlockSpec((seq, HEAD_DIM), lambda b, h: (row0 + b, COL_GK + h)),
        pl.BlockSpec((seq, HEAD_DIM), lambda b, h: (row0 + b, COL_GV + h)),
    ]
    args = ([qnorm_g.reshape(1, HEAD_DIM), knorm_g.reshape(1, HEAD_DIM)]
            + [proj] * (GQA_GROUP + 2))
    o_spec = pl.BlockSpec((seq, GQA_GROUP * HEAD_DIM), lambda b, h: (row0 + b, h))
    o_shape = jax.ShapeDtypeStruct((N_ROWS, GQA_W), BF16)
    cache_spec = pl.BlockSpec((None, None, SEQ, HEAD_DIM), lambda b, h: (b, layer, 0, h))
    aliases = {}
    if latent:
        cache_k, cache_v, cos, sin_signed, o_prev = latent_args
        table_spec = pl.BlockSpec((seq, HEAD_DIM), lambda b, h: (0, 0))
        in_specs += [cache_spec, cache_spec, table_spec, table_spec, _ANY]
        args += [cache_k.reshape(DEC_BATCH, DEPTH, PAST_LEN, GQA_KV_W),
                 cache_v.reshape(DEC_BATCH, DEPTH, PAST_LEN, GQA_KV_W), cos, sin_signed, o_prev]
        aliases = {len(args) - 1: 0}
        out_shape, out_specs = o_shape, o_spec
    else:
        if kv_prev is not None:
            in_specs += [_ANY, _ANY]
            args += list(kv_prev)
            aliases = {len(args) - 2: 1, len(args) - 1: 2}
        new_shape = jax.ShapeDtypeStruct((BATCH, DEPTH, SEQ, GQA_KV_W), F32)
        out_shape = (o_shape, new_shape, new_shape)
        out_specs = (o_spec, cache_spec, cache_spec)
    return pl.pallas_call(
        functools.partial(_gqa_kernel, seq=seq, latent=latent, n_alias=len(aliases)),
        out_shape=out_shape,
        grid=(n_batch, GQA_KV_HEADS),
        in_specs=in_specs,
        out_specs=out_specs,
        scratch_shapes=[pltpu.VMEM((past + seq, HEAD_DIM), BF16),
                        pltpu.VMEM((past + seq, 2 * HEAD_DIM), BF16)],
        input_output_aliases=aliases,
        compiler_params=_params("parallel", "parallel"),
        name=f"gqa_l{layer}_{'lat' if latent else 'ctx'}",
    )(*args)


def _diff_kernel(*refs, seq, latent, lam_init, n_alias, heads):
    qg_ref, kg_ref, sg_ref, lam_ref, q_ref, k_ref, v_ref = refs[:7]
    if latent:
        ck_ref, cv_ref, cos_ref, sin_ref = refs[7:11]
        o_ref, kt_ref, vt_ref = refs[11 + n_alias:]
    else:
        o_ref, kn_ref, vo_ref, kt_ref, vt_ref = refs[7 + n_alias:]
    past = PAST_LEN if latent else 0
    quarter = DIFF_QK_DIM // 4
    head_lanes = [slice(hd * HEAD_DIM, (hd + 1) * HEAD_DIM) for hd in range(heads)]

    for hd, lanes in enumerate(head_lanes):
        kn = _rms_halves(k_ref[:, lanes], kg_ref[...])
        v = v_ref[:, lanes]
        if latent:
            kn = _rope(kn, cos_ref[...], sin_ref[...], quarter)
            kt_ref[0:past, lanes] = ck_ref[:, lanes].astype(BF16)
            vt_ref[hd, 0:past, 0:HEAD_DIM] = cv_ref[:, lanes].astype(BF16)
        else:
            kn_ref[:, lanes] = kn
            vo_ref[:, lanes] = v
        kt_ref[past:past + seq, lanes] = kn.astype(BF16)
        vt_ref[hd, past:past + seq, 0:HEAD_DIM] = v.astype(BF16)
        vt_ref[hd, :, HEAD_DIM:] = jnp.ones((past + seq, HEAD_DIM), BF16)

    lv = lam_ref[...]
    lam = (jnp.exp(jnp.sum(lv[0:1, :] * lv[1:2, :], axis=-1, keepdims=True))
           - jnp.exp(jnp.sum(lv[2:3, :] * lv[3:4, :], axis=-1, keepdims=True)) + lam_init)
    q_scale = DIFF_QK_DIM ** -0.5 * LOG2E

    def attend(q_masked, kt, vt):
        s = _dot_nt(q_masked.astype(BF16), kt)
        p = jnp.exp2(s - jnp.max(s, axis=-1, keepdims=True))
        ov = _dot(p.astype(BF16), vt)
        return ov[:, :HEAD_DIM] / ov[:, HEAD_DIM:]

    q_block = min(seq, Q_BLOCK)
    for hd, lanes in enumerate(head_lanes):
        for r in range(seq // q_block):
            rows = slice(r * q_block, (r + 1) * q_block)
            qn = _rms_halves(q_ref[rows, lanes], qg_ref[...])
            if latent:
                qn = _rope(qn, cos_ref[rows, :], sin_ref[rows, :], quarter)
            qn = qn * q_scale
            lo = lax.broadcasted_iota(jnp.int32, qn.shape, 1) < DIFF_QK_DIM
            kt, vt = kt_ref[:, lanes], vt_ref[hd]
            o = attend(jnp.where(lo, qn, 0.0), kt, vt) - lam * attend(jnp.where(lo, 0.0, qn), kt, vt)
            o = _rms(o, sg_ref[...]) * (1.0 - lam_init)
            o_ref[rows, lanes] = o.astype(o_ref.dtype)


def _diff(proj, qnorm_g, knorm_g, subln_g, lam_params, layer, *, latent_args=None, kv_prev=None):
    latent = latent_args is not None
    n_batch, seq, row0 = (DEC_BATCH, DEC_SEQ, N_CTX // DEC_SEQ) if latent else (BATCH, SEQ, 0)
    past = PAST_LEN if latent else 0
    lam_init = 0.8 - 0.6 * math.exp(-0.3 * layer)
    heads = DIFF_HP_LAT if latent else DIFF_HP_CTX
    width = heads * HEAD_DIM

    def vec():
        return pl.BlockSpec((1, HEAD_DIM), lambda b, h: (0, 0))

    def col(block):
        return pl.BlockSpec((seq, width), lambda b, h: (row0 + b, block // heads + h))

    in_specs = [
        vec(), vec(), vec(),
        pl.BlockSpec((4, DIFF_QK_DIM), lambda b, h: (0, 0)),
        col(COL_DQ), col(COL_DK), col(COL_DV),
    ]
    args = [jnp.tile(qnorm_g, 2).reshape(1, HEAD_DIM), jnp.tile(knorm_g, 2).reshape(1, HEAD_DIM),
            subln_g.reshape(1, HEAD_DIM), lam_params, proj, proj, proj]
    o_spec = pl.BlockSpec((seq, width), lambda b, h: (row0 + b, h))
    o_shape = jax.ShapeDtypeStruct((N_ROWS, DIFF_W), BF16)
    cache_spec = pl.BlockSpec((None, None, SEQ, width), lambda b, h: (b, layer, 0, h))
    aliases = {}
    if latent:
        cache_k, cache_v, cos, sin_signed, o_prev = latent_args
        table_spec = pl.BlockSpec((seq, HEAD_DIM), lambda b, h: (0, 0))
        in_specs += [cache_spec, cache_spec, table_spec, table_spec, _ANY]
        args += [cache_k.reshape(DEC_BATCH, DEPTH, PAST_LEN, DIFF_W),
                 cache_v.reshape(DEC_BATCH, DEPTH, PAST_LEN, DIFF_W), cos, sin_signed, o_prev]
        aliases = {len(args) - 1: 0}
        out_shape, out_specs = o_shape, o_spec
    else:
        if kv_prev is not None:
            in_specs += [_ANY, _ANY]
            args += list(kv_prev)
            aliases = {len(args) - 2: 1, len(args) - 1: 2}
        new_shape = jax.ShapeDtypeStruct((BATCH, DEPTH, SEQ, DIFF_W), F32)
        out_shape = (o_shape, new_shape, new_shape)
        out_specs = (o_spec, cache_spec, cache_spec)
    return pl.pallas_call(
        functools.partial(_diff_kernel, seq=seq, latent=latent, lam_init=lam_init,
                          n_alias=len(aliases), heads=heads),
        out_shape=out_shape,
        grid=(n_batch, DIFF_HEADS // heads),
        in_specs=in_specs,
        out_specs=out_specs,
        scratch_shapes=[pltpu.VMEM((past + seq, width), BF16),
                        pltpu.VMEM((heads, past + seq, 2 * HEAD_DIM), BF16)],
        input_output_aliases=aliases,
        compiler_params=_params("parallel", "parallel"),
        name=f"diff_l{layer}_{'lat' if latent else 'ctx'}",
    )(*args)


def _rope_tables(n_tokens, dim, repeat):
    quarter = dim // 4
    t = jnp.arange(n_tokens)
    pos = jnp.stack([t // GRID_W, t % GRID_W], axis=-1).astype(F32)
    inv = ROPE_BASE ** (-jnp.arange(quarter, dtype=F32) / quarter)
    ang = pos[:, :, None] * inv
    cos, sin = jnp.cos(ang), jnp.sin(ang)
    cos_l = jnp.concatenate([cos, cos], axis=-1).reshape(n_tokens, dim)
    sin_l = jnp.concatenate([-sin, sin], axis=-1).reshape(n_tokens, dim)
    return jnp.tile(cos_l, (1, repeat)), jnp.tile(sin_l, (1, repeat))


def kernel(x_prompt, x_sample, c, cache_gqa_k, cache_gqa_v, cache_diff_k, cache_diff_v, state_hgrn,
           c_ctx, w_mod, b_mod, norm_g, ffn_w_gate, ffn_w_up, ffn_w_down, w_in, w_out, hgrn_lb_raw,
           hgrn_onorm_g, gqa_qnorm_g, gqa_knorm_g, diff_qnorm_g, diff_knorm_g, diff_lambda,
           diff_subln_g):
    cond = jnp.concatenate(
        [c_ctx[None, :], c, jnp.zeros((COND_PAD - N_COND, D_MODEL), F32)], axis=0)
    mod = _modulation(cond, w_mod, b_mod)

    cos_g, sin_g = _rope_tables(DEC_SEQ, HEAD_DIM, 1)
    cos_d, sin_d = _rope_tables(DEC_SEQ, DIFF_QK_DIM, 2)
    cache_dk = cache_diff_k.reshape(DEC_BATCH, DEPTH, PAST_LEN, DIFF_HEADS, 2 * DIFF_QK_DIM)
    ctx_rows = dict(n_tiles=CTX_TILES, tile0=0)
    lat_rows = dict(n_tiles=LAT_TILES, tile0=CTX_TILES)

    x = None
    states = gqa_kv = diff_kv = None
    for l in range(DEPTH):
        ffn_w = (mod[l], norm_g[l], ffn_w_gate, ffn_w_up, ffn_w_down, l)
        if l == 0:
            x = _ffn(x_prompt.reshape(N_CTX, D_MODEL), *ffn_w, 0, **ctx_rows)
            x = _ffn(x_sample.reshape(N_LAT, D_MODEL), *ffn_w, 0, **lat_rows,
                     out_off=CTX_TILES, o_prev=x)
        else:
            x = _ffn(x, *ffn_w, 0)
        proj = _proj_in(x, mod[l], norm_g[l], w_in, l)

        o_h, states = _hgrn(proj, hgrn_lb_raw, hgrn_onorm_g[l], l, s_prev=states)
        o_h = _hgrn(proj, hgrn_lb_raw, hgrn_onorm_g[l], l, state0=state_hgrn, o_prev=o_h)

        gqa_w = (proj, gqa_qnorm_g[l], gqa_knorm_g[l], l)
        o_g, *gqa_kv = _gqa(*gqa_w, kv_prev=gqa_kv)
        o_g = _gqa(*gqa_w, latent_args=(cache_gqa_k, cache_gqa_v, cos_g, sin_g, o_g))

        diff_w = (proj, diff_qnorm_g[l], diff_knorm_g[l], diff_subln_g[l], diff_lambda[l], l)
        o_d, *diff_kv = _diff(*diff_w, kv_prev=diff_kv)
        o_d = _diff(*diff_w, latent_args=(cache_dk, cache_diff_v, cos_d, sin_d, o_d))

        x = _proj_out(x, mod[l], o_h, o_g, o_d, w_out, l)
        if l < DEPTH - 1:
            x = _ffn(x, *ffn_w, 1)
        else:
            y_prompt = _ffn(x, *ffn_w, 1, **ctx_rows, out_rows=N_CTX)
            y_sample = _ffn(x, *ffn_w, 1, **lat_rows, in_off=CTX_TILES, out_rows=N_LAT)

    return (y_prompt.reshape(BATCH, SEQ, D_MODEL), y_sample.reshape(DEC_BATCH, DEC_SEQ, D_MODEL),
            gqa_kv[0].reshape(BATCH, DEPTH, SEQ, GQA_KV_HEADS, HEAD_DIM),
            gqa_kv[1].reshape(BATCH, DEPTH, SEQ, GQA_KV_HEADS, HEAD_DIM),
            diff_kv[0].reshape(BATCH, DEPTH, SEQ, DIFF_HEADS, 2, DIFF_QK_DIM),
            diff_kv[1].reshape(BATCH, DEPTH, SEQ, DIFF_HEADS, HEAD_DIM),
            states)
```

```python
import functools
import math

import jax
import jax.numpy as jnp
from jax import lax
from jax.experimental import pallas as pl
from jax.experimental.pallas import tpu as pltpu

F32 = jnp.float32
BF16 = jnp.bfloat16

D_MODEL = 2048
BATCH = 16
SEQ = 256
DEPTH = 2
DEC_BATCH = 2
DEC_SEQ = 1024
PAST_LEN = 256
GRID_W = 64
HEAD_DIM = 128
HGRN_HEADS = 4
GQA_Q_HEADS = 6
GQA_KV_HEADS = 2
GQA_GROUP = GQA_Q_HEADS // GQA_KV_HEADS
DIFF_HEADS = 6
DIFF_QK_DIM = 64
FFN_DIM = 5632
N_MOD = 9
IN_WIDTH = 6144
ROPE_BASE = 10000.0
EPS = 1e-6
LOG2E = math.log2(math.e)

N_CTX = BATCH * SEQ
N_LAT = DEC_BATCH * DEC_SEQ
N_ROWS = N_CTX + N_LAT
N_COND = 1 + DEC_BATCH
COND_PAD = 8

COL_HQ, COL_HI, COL_HG, COL_HFF, COL_HFB = 0, 4, 8, 12, 16
COL_GQ, COL_GK, COL_GV = 20, 26, 28
COL_DQ, COL_DK, COL_DV = 30, 36, 42

HGRN_W = HGRN_HEADS * HEAD_DIM
GQA_W = GQA_Q_HEADS * HEAD_DIM
GQA_KV_W = GQA_KV_HEADS * HEAD_DIM
DIFF_W = DIFF_HEADS * HEAD_DIM

VMEM_LIMIT = 60 * 1024 * 1024

TM = 1024
ROW_CHUNK = 128
FFN_TF = 256
FFN_TILES_PER_STEP = 2
FFN_NB = 512
IN_TN = 512
IN_TILES_PER_STEP = 2
OUT_TN = 512
OUT_TILES_PER_STEP = 2
MOD_TN = 1024
HGRN_CHUNK = 64
HGRN_SUB = 8
HGRN_HP = 4
Q_BLOCK = 512
GQA_Q_BLOCK = 256
DIFF_HP_CTX = 6
DIFF_HP_LAT = 2

CTX_TILES = N_CTX // TM
LAT_TILES = N_LAT // TM
ALL_TILES = CTX_TILES + LAT_TILES


def _cond_of_tile(i):
    tiles_per_latent = DEC_SEQ // TM
    return jnp.where(i < CTX_TILES, 0, 1 + (i - CTX_TILES) // tiles_per_latent)


def _silu(x):
    return x * jax.nn.sigmoid(x)


def _dot(a, b):
    return jnp.dot(a, b, preferred_element_type=F32)


def _dot_nt(a, b):
    return lax.dot_general(a, b, (((1,), (1,)), ((), ())), preferred_element_type=F32)


def _dot_tn(a, b):
    return lax.dot_general(a, b, (((0,), (0,)), ((), ())), preferred_element_type=F32)


def _rms(x, g):
    ms = jnp.mean(x * x, axis=-1, keepdims=True)
    return (x * lax.rsqrt(ms + EPS)) * g


def _params(*semantics):
    return pltpu.CompilerParams(dimension_semantics=semantics, vmem_limit_bytes=VMEM_LIMIT)


_ANY = pl.BlockSpec(memory_space=pl.ANY)


def _mod_kernel(cond_ref, w_ref, b_ref, o_ref):
    a = _silu(cond_ref[...]).astype(BF16)
    o_ref[...] = _dot(a, w_ref[...].astype(BF16)) + b_ref[...]


def _modulation(cond, w_mod, b_mod):
    width = N_MOD * D_MODEL
    out = pl.pallas_call(
        _mod_kernel,
        out_shape=jax.ShapeDtypeStruct((DEPTH, COND_PAD, width), F32),
        grid=(DEPTH, width // MOD_TN),
        in_specs=[
            pl.BlockSpec((COND_PAD, D_MODEL), lambda l, j: (0, 0)),
            pl.BlockSpec((None, D_MODEL, MOD_TN), lambda l, j: (l, 0, j)),
            pl.BlockSpec((None, 1, MOD_TN), lambda l, j: (l, 0, j)),
        ],
        out_specs=pl.BlockSpec((None, COND_PAD, MOD_TN), lambda l, j: (l, 0, j)),
        compiler_params=_params("parallel", "parallel"),
        name="modulation",
    )(cond, w_mod, b_mod.reshape(DEPTH, 1, width))
    return out[:, :N_COND].reshape(DEPTH, N_COND, N_MOD, D_MODEL)


def _mod_norm_into(x_ref, mod_ref, g_ref, h_ref, sub, row0=0):
    shift = mod_ref[3 * sub:3 * sub + 1, :]
    gain = g_ref[sub:sub + 1, :] * (1.0 + mod_ref[3 * sub + 1:3 * sub + 2, :])

    def body(r, carry):
        rows = pl.ds(pl.multiple_of(row0 + r * ROW_CHUNK, ROW_CHUNK), ROW_CHUNK)
        h_ref[rows, :] = (_rms(x_ref[rows, :], gain) + shift).astype(BF16)
        return carry

    lax.fori_loop(0, TM // ROW_CHUNK, body, 0)


def _ffn_kernel(x_ref, mod_ref, g_ref, wg_ref, wu_ref, wd_ref, *rest, sub, tile0):
    o_ref, h_ref = rest[-2:]
    j = pl.program_id(1)
    conds = [_cond_of_tile(tile0 + pl.program_id(0) * FFN_TILES_PER_STEP + part)
             for part in range(FFN_TILES_PER_STEP)]

    @pl.when(j == 0)
    def _():
        for part, cond in enumerate(conds):
            _mod_norm_into(x_ref, mod_ref.at[cond], g_ref, h_ref, sub, row0=part * TM)
        o_ref[...] = jnp.zeros_like(o_ref)

    for part in range(FFN_TILES_PER_STEP):
        rows = slice(part * TM, (part + 1) * TM)
        h = h_ref[rows, :]
        gate_act = _dot(h, wg_ref[...].astype(BF16))
        up = _dot(h, wu_ref[...].astype(BF16))
        a = (_silu(gate_act) * up).astype(BF16)
        for n in range(0, D_MODEL, FFN_NB):
            o_ref[rows, n:n + FFN_NB] += _dot(a, wd_ref[:, n:n + FFN_NB].astype(BF16))

    @pl.when(j == pl.num_programs(1) - 1)
    def _():
        for part, cond in enumerate(conds):
            gate = mod_ref[cond, 3 * sub + 2:3 * sub + 3, :]

            def body(r, carry):
                rows = pl.ds(pl.multiple_of(part * TM + r * ROW_CHUNK, ROW_CHUNK), ROW_CHUNK)
                o_ref[rows, :] = x_ref[rows, :] + gate * (0.5 * o_ref[rows, :])
                return carry

            lax.fori_loop(0, TM // ROW_CHUNK, body, 0)


def _ffn(x, mod_l, norm_g_l, w_gate, w_up, w_down, layer, which, *, n_tiles=ALL_TILES, in_off=0,
         out_off=0, tile0=0, out_rows=N_ROWS, o_prev=None):
    sub = 2 * which
    per = FFN_TILES_PER_STEP
    rows = per * TM
    assert n_tiles % per == 0 and in_off % per == 0 and out_off % per == 0
    in_specs = [
        pl.BlockSpec((rows, D_MODEL), lambda i, j: (in_off // per + i, 0), pipeline_mode=pl.Buffered(1)),
        pl.BlockSpec((N_COND, N_MOD, D_MODEL), lambda i, j: (0, 0, 0)),
        pl.BlockSpec((3, D_MODEL), lambda i, j: (0, 0)),
        pl.BlockSpec((None, None, D_MODEL, FFN_TF), lambda i, j: (layer, which, 0, j)),
        pl.BlockSpec((None, None, D_MODEL, FFN_TF), lambda i, j: (layer, which, 0, j)),
        pl.BlockSpec((None, None, FFN_TF, D_MODEL), lambda i, j: (layer, which, j, 0)),
    ]
    args = [x, mod_l, norm_g_l, w_gate, w_up, w_down]
    aliases = {}
    if o_prev is not None:
        in_specs.append(_ANY)
        args.append(o_prev)
        aliases = {len(args) - 1: 0}
    return pl.pallas_call(
        functools.partial(_ffn_kernel, sub=sub, tile0=tile0),
        out_shape=jax.ShapeDtypeStruct((out_rows, D_MODEL), F32),
        grid=(n_tiles // per, FFN_DIM // FFN_TF),
        in_specs=in_specs,
        out_specs=pl.BlockSpec((rows, D_MODEL), lambda i, j: (out_off // per + i, 0),
                               pipeline_mode=pl.Buffered(1)),
        scratch_shapes=[pltpu.VMEM((rows, D_MODEL), BF16)],
        input_output_aliases=aliases,
        compiler_params=_params("parallel", "arbitrary"),
        name=f"ffn_l{layer}_h{which}_t{tile0}n{n_tiles}",
    )(*args)


def _proj_in_kernel(x_ref, mod_ref, g_ref, w_ref, o_ref, h_ref):
    @pl.when(pl.program_id(1) == 0)
    def _():
        for part in range(IN_TILES_PER_STEP):
            cond = _cond_of_tile(pl.program_id(0) * IN_TILES_PER_STEP + part)
            _mod_norm_into(x_ref, mod_ref.at[cond], g_ref, h_ref, 1, row0=part * TM)

    w = w_ref[...].astype(BF16)
    for part in range(IN_TILES_PER_STEP):
        rows = slice(part * TM, (part + 1) * TM)
        o_ref[rows, :] = _dot(h_ref[rows, :], w)


def _proj_in(x, mod_l, norm_g_l, w_in, layer):
    rows = IN_TILES_PER_STEP * TM
    return pl.pallas_call(
        _proj_in_kernel,
        out_shape=jax.ShapeDtypeStruct((N_ROWS, IN_WIDTH), F32),
        grid=(N_ROWS // rows, IN_WIDTH // IN_TN),
        in_specs=[
            pl.BlockSpec((rows, D_MODEL), lambda i, j: (i, 0)),
            pl.BlockSpec((N_COND, N_MOD, D_MODEL), lambda i, j: (0, 0, 0)),
            pl.BlockSpec((3, D_MODEL), lambda i, j: (0, 0)),
            pl.BlockSpec((None, D_MODEL, IN_TN), lambda i, j: (layer, 0, j)),
        ],
        out_specs=pl.BlockSpec((rows, IN_TN), lambda i, j: (i, j)),
        scratch_shapes=[pltpu.VMEM((rows, D_MODEL), BF16)],
        compiler_params=_params("parallel", "arbitrary"),
        name=f"proj_in_l{layer}",
    )(x, mod_l, norm_g_l, w_in)


def _proj_out_kernel(x_ref, mod_ref, oh_ref, og_ref, od_ref, w_ref, o_ref):
    w_h = w_ref[0:HGRN_W, :].astype(BF16)
    w_g = w_ref[HGRN_W:HGRN_W + GQA_W, :].astype(BF16)
    w_d = w_ref[HGRN_W + GQA_W:, :].astype(BF16)
    for part in range(OUT_TILES_PER_STEP):
        rows = slice(part * TM, (part + 1) * TM)
        cond = _cond_of_tile(pl.program_id(0) * OUT_TILES_PER_STEP + part)
        m = _dot(oh_ref[rows, :], w_h) + _dot(og_ref[rows, :], w_g) + _dot(od_ref[rows, :], w_d)
        o_ref[rows, :] = x_ref[rows, :] + mod_ref[cond, 5:6, :] * m


def _proj_out(x, mod_l, o_h, o_g, o_d, w_out, layer):
    rows = OUT_TILES_PER_STEP * TM
    return pl.pallas_call(
        _proj_out_kernel,
        out_shape=jax.ShapeDtypeStruct((N_ROWS, D_MODEL), F32),
        grid=(N_ROWS // rows, D_MODEL // OUT_TN),
        in_specs=[
            pl.BlockSpec((rows, OUT_TN), lambda i, j: (i, j)),
            pl.BlockSpec((N_COND, N_MOD, OUT_TN), lambda i, j: (0, 0, j)),
            pl.BlockSpec((rows, HGRN_W), lambda i, j: (i, 0)),
            pl.BlockSpec((rows, GQA_W), lambda i, j: (i, 0)),
            pl.BlockSpec((rows, DIFF_W), lambda i, j: (i, 0)),
            pl.BlockSpec((None, D_MODEL, OUT_TN), lambda i, j: (layer, 0, j)),
        ],
        out_specs=pl.BlockSpec((rows, OUT_TN), lambda i, j: (i, j)),
        compiler_params=_params("parallel", "parallel"),
        name=f"proj_out_l{layer}",
    )(x, mod_l, o_h, o_g, o_d, w_out)


def _log2_forget_and_key(z, lb):
    z2 = z * LOG2E
    soft = jnp.log2(1.0 + jnp.exp2(-jnp.abs(z2)))
    log_1mlb = jnp.log1p(-lb) * LOG2E
    a = jnp.log2(lb)
    c = log_1mlb + (jnp.minimum(z2, 0.0) - soft)
    log_f = jnp.maximum(a, c) + jnp.log2(1.0 + jnp.exp2(-jnp.abs(a - c)))
    log_k = log_1mlb + (jnp.minimum(-z2, 0.0) - soft)
    return log_f, log_k


def _cumsum_rows(x, reverse):
    tile = HGRN_SUB
    row = lax.broadcasted_iota(jnp.int32, (tile, 1), 0)
    tiles = []
    for j in range(x.shape[0] // tile):
        y = x[j * tile:(j + 1) * tile, :]
        for sh in (1, 2, 4):
            if reverse:
                y = y + jnp.where(row < tile - sh, pltpu.roll(y, tile - sh, 0), 0.0)
            else:
                y = y + jnp.where(row >= sh, pltpu.roll(y, sh, 0), 0.0)
        tiles.append(y)
    order = range(len(tiles) - 1, -1, -1) if reverse else range(len(tiles))
    carry = None
    for j in order:
        if carry is not None:
            tiles[j] = tiles[j] + carry
        carry = tiles[j][0:1, :] if reverse else tiles[j][tile - 1:tile, :]
    return jnp.concatenate(tiles, axis=0)


def _hgrn_prepare(q, z, lb, reverse):
    lf2, lk2 = _log2_forget_and_key(z, lb)
    b = _cumsum_rows(lf2, reverse)
    return _silu(q), b, b - lk2


def _hgrn_chunk(qs, b, c, v, st, reverse):
    C, SUB = HGRN_CHUNK, HGRN_SUB
    n_sub = C // SUB
    b_tot = b[0:1, :] if reverse else b[C - 1:C, :]

    o = _dot_nt((qs * jnp.exp2(b)).astype(BF16), st.astype(BF16))

    terms = []
    for i in range(n_sub):
        lo = i * SUB
        qi, bi, ci = qs[lo:lo + SUB, :], b[lo:lo + SUB, :], c[lo:lo + SUB, :]
        for s in range(SUB):
            terms.append(qi * jnp.exp2(bi - ci[s:s + 1, :]))
    k_sums = _dot(jnp.concatenate(terms, axis=0).astype(BF16), jnp.ones((HEAD_DIM, C), BF16))

    row = lax.broadcasted_iota(jnp.int32, (SUB, C), 0)
    lane = lax.broadcasted_iota(jnp.int32, (SUB, C), 1)
    lane_s = lane % SUB
    causal = (lane_s >= row) if reverse else (lane_s <= row)
    blocks = []
    for i in range(n_sub):
        lo, hi = i * SUB, (i + 1) * SUB
        diag = k_sums[lo * SUB:(lo + 1) * SUB, :]
        for s in range(1, SUB):
            diag = jnp.where(lane_s == s, k_sums[(lo + s) * SUB:(lo + s + 1) * SUB, :], diag)
        has_off = (i < n_sub - 1) if reverse else (i > 0)
        if has_off:
            ref = b[hi:hi + 1, :] if reverse else b[lo - 1:lo, :]
            qt = (qs[lo:hi, :] * jnp.exp2(b[lo:hi, :] - ref)).astype(BF16)
            if reverse:
                kt = jnp.concatenate([jnp.zeros((hi, HEAD_DIM), F32), jnp.exp2(ref - c[hi:, :])], axis=0)
            else:
                kt = jnp.concatenate([jnp.exp2(ref - c[:lo, :]), jnp.zeros((C - lo, HEAD_DIM), F32)], axis=0)
            off = _dot_nt(qt, kt.astype(BF16))
        else:
            off = jnp.zeros((SUB, C), F32)
        blocks.append(jnp.where((lane // SUB == i) & causal, diag, off))
    a = jnp.concatenate(blocks, axis=0)
    o = o + _dot(a.astype(BF16), v.astype(BF16))

    st_new = jnp.exp2(b_tot) * st + _dot_tn(v.astype(BF16), jnp.exp2(b_tot - c).astype(BF16))
    return o, st_new


def _hgrn_kernel(*refs, layer, seq, latent, n_alias):
    raw_ref, q_ref, v_ref, g_ref, ff_ref, fb_ref, ong_ref = refs[:7]
    s0_ref = refs[7] if latent else None
    outs = refs[7 + (1 if latent else 0) + n_alias:]
    if latent:
        o_ref, of_ref, ob_ref, st_ref, pre_ref = outs
        s_ref = None
    else:
        o_ref, s_ref, of_ref, ob_ref, st_ref, pre_ref = outs
    C = HGRN_CHUNK
    n_chunks = seq // C

    def lower_bound(d, lanes):
        rows = [raw_ref[2 * l + d:2 * l + d + 1, lanes] for l in range(DEPTH)]
        m = functools.reduce(jnp.maximum, rows)
        e = [jnp.exp(r - m) for r in rows]
        tot = functools.reduce(lambda x, y: x + y, e)
        lb = jnp.zeros_like(m)
        for l in range(1, layer + 1):
            lb = lb + e[l] / tot
        return lb

    head_lanes = [slice(hp * HEAD_DIM, (hp + 1) * HEAD_DIM) for hp in range(HGRN_HP)]
    lbs = [[lower_bound(d, lanes) for d in range(2)] for lanes in head_lanes]

    for hp in range(HGRN_HP):
        for d in range(2):
            st_ref[hp, d] = s0_ref[d, hp].T if latent else jnp.zeros((HEAD_DIM, HEAD_DIM), F32)

    def chunk_rows(ci):
        return (pl.ds(pl.multiple_of(ci * C, C), C),
                pl.ds(pl.multiple_of((n_chunks - 1 - ci) * C, C), C))

    def prepare(ci, slot):
        rows = chunk_rows(ci)
        for hp, lanes in enumerate(head_lanes):
            for d, f_ref in enumerate((ff_ref, fb_ref)):
                pre = _hgrn_prepare(q_ref[rows[d], lanes], f_ref[rows[d], lanes], lbs[hp][d], d == 1)
                for n, val in enumerate(pre):
                    pre_ref[slot, 2 * hp + d, n] = val

    prepare(0, 0)

    def body(ci, carry):
        slot = ci % 2
        rows = chunk_rows(ci)
        for hp, lanes in enumerate(head_lanes):
            for d, acc_ref in enumerate((of_ref, ob_ref)):
                qs, b, c = (pre_ref[slot, 2 * hp + d, n] for n in range(3))
                o, st = _hgrn_chunk(qs, b, c, v_ref[rows[d], lanes], st_ref[hp, d], d == 1)
                acc_ref[rows[d], lanes] = o
                st_ref[hp, d] = st
        prepare(jnp.minimum(ci + 1, n_chunks - 1), 1 - slot)
        return carry

    lax.fori_loop(0, n_chunks, body, 0)

    for hp, lanes in enumerate(head_lanes):
        if s_ref is not None:
            for d in range(2):
                s_ref[d, hp] = st_ref[hp, d].T
        o = of_ref[:, lanes] + ob_ref[:, lanes]
        o_ref[:, lanes] = (_rms(o, ong_ref[...]) * _silu(g_ref[:, lanes])).astype(o_ref.dtype)


def _hgrn(proj, lb_raw, onorm_g_l, layer, *, state0=None, o_prev=None, s_prev=None):
    latent = state0 is not None
    n_batch, seq, row0 = (DEC_BATCH, DEC_SEQ, N_CTX // DEC_SEQ) if latent else (BATCH, SEQ, 0)
    width = HGRN_HP * HEAD_DIM

    def col(block):
        return pl.BlockSpec((seq, width), lambda b, h: (row0 + b, block // HGRN_HP + h))

    in_specs = [
        pl.BlockSpec((2 * DEPTH, width), lambda b, h: (0, h)),
        col(COL_HQ), col(COL_HI), col(COL_HG), col(COL_HFF), col(COL_HFB),
        pl.BlockSpec((1, HEAD_DIM), lambda b, h: (0, 0)),
    ]
    args = [lb_raw.reshape(2 * DEPTH, HGRN_W), proj, proj, proj, proj, proj,
            onorm_g_l.reshape(1, HEAD_DIM)]
    o_shape = jax.ShapeDtypeStruct((N_ROWS, HGRN_W), BF16)
    o_spec = pl.BlockSpec((seq, width), lambda b, h: (row0 + b, h))
    aliases = {}
    if latent:
        in_specs += [pl.BlockSpec((None, None, 2, HGRN_HP, HEAD_DIM, HEAD_DIM),
                                  lambda b, h: (b, layer, 0, h, 0, 0)), _ANY]
        args += [state0, o_prev]
        aliases = {len(args) - 1: 0}
        out_shape, out_specs = o_shape, o_spec
    else:
        if s_prev is not None:
            in_specs.append(_ANY)
            args.append(s_prev)
            aliases = {len(args) - 1: 1}
        out_shape = (o_shape, jax.ShapeDtypeStruct(
            (BATCH, DEPTH, 2, HGRN_HEADS, HEAD_DIM, HEAD_DIM), F32))
        out_specs = (o_spec, pl.BlockSpec((None, None, 2, HGRN_HP, HEAD_DIM, HEAD_DIM),
                                          lambda b, h: (b, layer, 0, h, 0, 0)))
    return pl.pallas_call(
        functools.partial(_hgrn_kernel, layer=layer, seq=seq, latent=latent, n_alias=len(aliases)),
        out_shape=out_shape,
        grid=(n_batch, HGRN_HEADS // HGRN_HP),
        in_specs=in_specs,
        out_specs=out_specs,
        scratch_shapes=[pltpu.VMEM((seq, width), F32), pltpu.VMEM((seq, width), F32),
                        pltpu.VMEM((HGRN_HP, 2, HEAD_DIM, HEAD_DIM), F32),
                        pltpu.VMEM((2, 2 * HGRN_HP, 3, HGRN_CHUNK, HEAD_DIM), F32)],
        input_output_aliases=aliases,
        compiler_params=_params("parallel", "parallel"),
        name=f"hgrn_l{layer}_{'lat' if latent else 'ctx'}",
    )(*args)


def _swap_pairs(x, width):
    lanes = x.shape[-1]
    lane = lax.broadcasted_iota(jnp.int32, x.shape, x.ndim - 1)
    from_right = pltpu.roll(x, lanes - width, x.ndim - 1)
    from_left = pltpu.roll(x, width, x.ndim - 1)
    return jnp.where(lane % (2 * width) < width, from_right, from_left)


def _rope(x, cos, sin_signed, quarter):
    return x * cos + _swap_pairs(x, quarter) * sin_signed


def _rms_halves(x, g):
    half = x.shape[-1] // 2
    lane = lax.broadcasted_iota(jnp.int32, x.shape, x.ndim - 1)
    lo = lane < half
    sq = x * x
    ms_lo = jnp.sum(jnp.where(lo, sq, 0.0), axis=-1, keepdims=True) / half
    ms_hi = jnp.sum(jnp.where(lo, 0.0, sq), axis=-1, keepdims=True) / half
    ms = jnp.where(lo, ms_lo, ms_hi)
    return (x * lax.rsqrt(ms + EPS)) * g


def _gqa_kernel(*refs, seq, latent, n_alias):
    qg_ref, kg_ref = refs[:2]
    q_refs = refs[2:2 + GQA_GROUP]
    k_ref, v_ref = refs[2 + GQA_GROUP:4 + GQA_GROUP]
    rest = refs[4 + GQA_GROUP:]
    if latent:
        ck_ref, cv_ref, cos_ref, sin_ref = rest[:4]
        o_ref, kt_ref, vt_ref = rest[4 + n_alias:]
    else:
        o_ref, kn_ref, vo_ref, kt_ref, vt_ref = rest[n_alias:]
    past = PAST_LEN if latent else 0
    quarter = HEAD_DIM // 4

    kn = _rms(k_ref[...], kg_ref[...])
    v = v_ref[...]
    if latent:
        kn = _rope(kn, cos_ref[...], sin_ref[...], quarter)
        kt_ref[0:past, :] = ck_ref[...].astype(BF16)
        vt_ref[0:past, 0:HEAD_DIM] = cv_ref[...].astype(BF16)
    else:
        kn_ref[...] = kn
        vo_ref[...] = v
    kt_ref[past:past + seq, :] = kn.astype(BF16)
    vt_ref[past:past + seq, 0:HEAD_DIM] = v.astype(BF16)
    vt_ref[:, HEAD_DIM:] = jnp.ones((past + seq, HEAD_DIM), BF16)

    q_scale = HEAD_DIM ** -0.5 * LOG2E

    def attend(q_rows):
        s = _dot_nt(jnp.concatenate(q_rows, axis=0), kt_ref[...])
        p = jnp.exp2(s - jnp.max(s, axis=-1, keepdims=True))
        ov = _dot(p.astype(BF16), vt_ref[...])
        o = (ov[:, :HEAD_DIM] / ov[:, HEAD_DIM:]).astype(o_ref.dtype)
        n = q_rows[0].shape[0]
        return [o[i * n:(i + 1) * n, :] for i in range(len(q_rows))]

    def query(q_ref, rows):
        qn = _rms(q_ref[rows, :], qg_ref[...])
        if latent:
            qn = _rope(qn, cos_ref[rows, :], sin_ref[rows, :], quarter)
        return (qn * q_scale).astype(BF16)

    head_lanes = [slice(g * HEAD_DIM, (g + 1) * HEAD_DIM) for g in range(GQA_GROUP)]
    if seq <= GQA_Q_BLOCK:
        rows = slice(0, seq)
        outs = attend([query(q_ref, rows) for q_ref in q_refs])
        for lanes, o in zip(head_lanes, outs):
            o_ref[rows, lanes] = o
    else:
        for lanes, q_ref in zip(head_lanes, q_refs):
            for r in range(seq // GQA_Q_BLOCK):
                rows = slice(r * GQA_Q_BLOCK, (r + 1) * GQA_Q_BLOCK)
                o_ref[rows, lanes] = attend([query(q_ref, rows)])[0]


def _gqa(proj, qnorm_g, knorm_g, layer, *, latent_args=None, kv_prev=None):
    latent = latent_args is not None
    n_batch, seq, row0 = (DEC_BATCH, DEC_SEQ, N_CTX // DEC_SEQ) if latent else (BATCH, SEQ, 0)
    past = PAST_LEN if latent else 0

    def vec():
        return pl.BlockSpec((1, HEAD_DIM), lambda b, h: (0, 0))

    def q_spec(g):
        return pl.BlockSpec((seq, HEAD_DIM), lambda b, h: (row0 + b, COL_GQ + h * GQA_GROUP + g))

    in_specs = [vec(), vec()] + [q_spec(g) for g in range(GQA_GROUP)] + [
        pl.BlockSpec((seq, HEAD_DIM), lambda b, h: (row0 + b, COL_GK + h)),
        pl.BlockSpec((seq, HEAD_DIM), lambda b, h: (row0 + b, COL_GV + h)),
    ]
    args = ([qnorm_g.reshape(1, HEAD_DIM), knorm_g.reshape(1, HEAD_DIM)]
            + [proj] * (GQA_GROUP + 2))
    o_spec = pl.BlockSpec((seq, GQA_GROUP * HEAD_DIM), lambda b, h: (row0 + b, h))
    o_shape = jax.ShapeDtypeStruct((N_ROWS, GQA_W), BF16)
    cache_spec = pl.BlockSpec((None, None, SEQ, HEAD_DIM), lambda b, h: (b, layer, 0, h))
    aliases = {}
    if latent:
        cache_k, cache_v, cos, sin_signed, o_prev = latent_args
        table_spec = pl.BlockSpec((seq, HEAD_DIM), lambda b, h: (0, 0))
        in_specs += [cache_spec, cache_spec, table_spec, table_spec, _ANY]
        args += [cache_k.reshape(DEC_BATCH, DEPTH, PAST_LEN, GQA_KV_W),
                 cache_v.reshape(DEC_BATCH, DEPTH, PAST_LEN, GQA_KV_W), cos, sin_signed, o_prev]
        aliases = {len(args) - 1: 0}
        out_shape, out_specs = o_shape, o_spec
    else:
        if kv_prev is not None:
            in_specs += [_ANY, _ANY]
            args += list(kv_prev)
            aliases = {len(args) - 2: 1, len(args) - 1: 2}
        new_shape = jax.ShapeDtypeStruct((BATCH, DEPTH, SEQ, GQA_KV_W), F32)
        out_shape = (o_shape, new_shape, new_shape)
        out_specs = (o_spec, cache_spec, cache_spec)
    return pl.pallas_call(
        functools.partial(_gqa_kernel, seq=seq, latent=latent, n_alias=len(aliases)),
        out_shape=out_shape,
        grid=(n_batch, GQA_KV_HEADS),
        in_specs=in_specs,
        out_specs=out_specs,
        scratch_shapes=[pltpu.VMEM((past + seq, HEAD_DIM), BF16),
                        pltpu.VMEM((past + seq, 2 * HEAD_DIM), BF16)],
        input_output_aliases=aliases,
        compiler_params=_params("parallel", "parallel"),
        name=f"gqa_l{layer}_{'lat' if latent else 'ctx'}",
    )(*args)


def _diff_kernel(*refs, seq, latent, lam_init, n_alias, heads):
    qg_ref, kg_ref, sg_ref, lam_ref, q_ref, k_ref, v_ref = refs[:7]
    if latent:
        ck_ref, cv_ref, cos_ref, sin_ref = refs[7:11]
        o_ref, kt_ref, vt_ref = refs[11 + n_alias:]
    else:
        o_ref, kn_ref, vo_ref, kt_ref, vt_ref = refs[7 + n_alias:]
    past = PAST_LEN if latent else 0
    quarter = DIFF_QK_DIM // 4
    head_lanes = [slice(hd * HEAD_DIM, (hd + 1) * HEAD_DIM) for hd in range(heads)]

    for hd, lanes in enumerate(head_lanes):
        kn = _rms_halves(k_ref[:, lanes], kg_ref[...])
        v = v_ref[:, lanes]
        if latent:
            kn = _rope(kn, cos_ref[...], sin_ref[...], quarter)
            kt_ref[0:past, lanes] = ck_ref[:, lanes].astype(BF16)
            vt_ref[hd, 0:past, 0:HEAD_DIM] = cv_ref[:, lanes].astype(BF16)
        else:
            kn_ref[:, lanes] = kn
            vo_ref[:, lanes] = v
        kt_ref[past:past + seq, lanes] = kn.astype(BF16)
        vt_ref[hd, past:past + seq, 0:HEAD_DIM] = v.astype(BF16)
        vt_ref[hd, :, HEAD_DIM:] = jnp.ones((past + seq, HEAD_DIM), BF16)

    lv = lam_ref[...]
    lam = (jnp.exp(jnp.sum(lv[0:1, :] * lv[1:2, :], axis=-1, keepdims=True))
           - jnp.exp(jnp.sum(lv[2:3, :] * lv[3:4, :], axis=-1, keepdims=True)) + lam_init)
    q_scale = DIFF_QK_DIM ** -0.5 * LOG2E

    def attend(q_masked, kt, vt):
        s = _dot_nt(q_masked.astype(BF16), kt)
        p = jnp.exp2(s - jnp.max(s, axis=-1, keepdims=True))
        ov = _dot(p.astype(BF16), vt)
        return ov[:, :HEAD_DIM] / ov[:, HEAD_DIM:]

    q_block = min(seq, Q_BLOCK)
    for hd, lanes in enumerate(head_lanes):
        for r in range(seq // q_block):
            rows = slice(r * q_block, (r + 1) * q_block)
            qn = _rms_halves(q_ref[rows, lanes], qg_ref[...])
            if latent:
                qn = _rope(qn, cos_ref[rows, :], sin_ref[rows, :], quarter)
            qn = qn * q_scale
            lo = lax.broadcasted_iota(jnp.int32, qn.shape, 1) < DIFF_QK_DIM
            kt, vt = kt_ref[:, lanes], vt_ref[hd]
            o = attend(jnp.where(lo, qn, 0.0), kt, vt) - lam * attend(jnp.where(lo, 0.0, qn), kt, vt)
            o = _rms(o, sg_ref[...]) * (1.0 - lam_init)
            o_ref[rows, lanes] = o.astype(o_ref.dtype)


def _diff(proj, qnorm_g, knorm_g, subln_g, lam_params, layer, *, latent_args=None, kv_prev=None):
    latent = latent_args is not None
    n_batch, seq, row0 = (DEC_BATCH, DEC_SEQ, N_CTX // DEC_SEQ) if latent else (BATCH, SEQ, 0)
    past = PAST_LEN if latent else 0
    lam_init = 0.8 - 0.6 * math.exp(-0.3 * layer)
    heads = DIFF_HP_LAT if latent else DIFF_HP_CTX
    width = heads * HEAD_DIM

    def vec():
        return pl.BlockSpec((1, HEAD_DIM), lambda b, h: (0, 0))

    def col(block):
        return pl.BlockSpec((seq, width), lambda b, h: (row0 + b, block // heads + h))

    in_specs = [
        vec(), vec(), vec(),
        pl.BlockSpec((4, DIFF_QK_DIM), lambda b, h: (0, 0)),
        col(COL_DQ), col(COL_DK), col(COL_DV),
    ]
    args = [jnp.tile(qnorm_g, 2).reshape(1, HEAD_DIM), jnp.tile(knorm_g, 2).reshape(1, HEAD_DIM),
            subln_g.reshape(1, HEAD_DIM), lam_params, proj, proj, proj]
    o_spec = pl.BlockSpec((seq, width), lambda b, h: (row0 + b, h))
    o_shape = jax.ShapeDtypeStruct((N_ROWS, DIFF_W), BF16)
    cache_spec = pl.BlockSpec((None, None, SEQ, width), lambda b, h: (b, layer, 0, h))
    aliases = {}
    if latent:
        cache_k, cache_v, cos, sin_signed, o_prev = latent_args
        table_spec = pl.BlockSpec((seq, HEAD_DIM), lambda b, h: (0, 0))
        in_specs += [cache_spec, cache_spec, table_spec, table_spec, _ANY]
        args += [cache_k.reshape(DEC_BATCH, DEPTH, PAST_LEN, DIFF_W),
                 cache_v.reshape(DEC_BATCH, DEPTH, PAST_LEN, DIFF_W), cos, sin_signed, o_prev]
        aliases = {len(args) - 1: 0}
        out_shape, out_specs = o_shape, o_spec
    else:
        if kv_prev is not None:
            in_specs += [_ANY, _ANY]
            args += list(kv_prev)
            aliases = {len(args) - 2: 1, len(args) - 1: 2}
        new_shape = jax.ShapeDtypeStruct((BATCH, DEPTH, SEQ, DIFF_W), F32)
        out_shape = (o_shape, new_shape, new_shape)
        out_specs = (o_spec, cache_spec, cache_spec)
    return pl.pallas_call(
        functools.partial(_diff_kernel, seq=seq, latent=latent, lam_init=lam_init,
                          n_alias=len(aliases), heads=heads),
        out_shape=out_shape,
        grid=(n_batch, DIFF_HEADS // heads),
        in_specs=in_specs,
        out_specs=out_specs,
        scratch_shapes=[pltpu.VMEM((past + seq, width), BF16),
                        pltpu.VMEM((heads, past + seq, 2 * HEAD_DIM), BF16)],
        input_output_aliases=aliases,
        compiler_params=_params("parallel", "parallel"),
        name=f"diff_l{layer}_{'lat' if latent else 'ctx'}",
    )(*args)


def _rope_tables(n_tokens, dim, repeat):
    quarter = dim // 4
    t = jnp.arange(n_tokens)
    pos = jnp.stack([t // GRID_W, t % GRID_W], axis=-1).astype(F32)
    inv = ROPE_BASE ** (-jnp.arange(quarter, dtype=F32) / quarter)
    ang = pos[:, :, None] * inv
    cos, sin = jnp.cos(ang), jnp.sin(ang)
    cos_l = jnp.concatenate([cos, cos], axis=-1).reshape(n_tokens, dim)
    sin_l = jnp.concatenate([-sin, sin], axis=-1).reshape(n_tokens, dim)
    return jnp.tile(cos_l, (1, repeat)), jnp.tile(sin_l, (1, repeat))


def kernel(x_prompt, x_sample, c, cache_gqa_k, cache_gqa_v, cache_diff_k, cache_diff_v, state_hgrn,
           c_ctx, w_mod, b_mod, norm_g, ffn_w_gate, ffn_w_up, ffn_w_down, w_in, w_out, hgrn_lb_raw,
           hgrn_onorm_g, gqa_qnorm_g, gqa_knorm_g, diff_qnorm_g, diff_knorm_g, diff_lambda,
           diff_subln_g):
    cond = jnp.concatenate(
        [c_ctx[None, :], c, jnp.zeros((COND_PAD - N_COND, D_MODEL), F32)], axis=0)
    mod = _modulation(cond, w_mod, b_mod)

    cos_g, sin_g = _rope_tables(DEC_SEQ, HEAD_DIM, 1)
    cos_d, sin_d = _rope_tables(DEC_SEQ, DIFF_QK_DIM, 2)
    cache_dk = cache_diff_k.reshape(DEC_BATCH, DEPTH, PAST_LEN, DIFF_HEADS, 2 * DIFF_QK_DIM)
    ctx_rows = dict(n_tiles=CTX_TILES, tile0=0)
    lat_rows = dict(n_tiles=LAT_TILES, tile0=CTX_TILES)

    x = None
    states = gqa_kv = diff_kv = None
    for l in range(DEPTH):
        ffn_w = (mod[l], norm_g[l], ffn_w_gate, ffn_w_up, ffn_w_down, l)
        if l == 0:
            x = _ffn(x_prompt.reshape(N_CTX, D_MODEL), *ffn_w, 0, **ctx_rows)
            x = _ffn(x_sample.reshape(N_LAT, D_MODEL), *ffn_w, 0, **lat_rows,
                     out_off=CTX_TILES, o_prev=x)
        else:
            x = _ffn(x, *ffn_w, 0)
        proj = _proj_in(x, mod[l], norm_g[l], w_in, l)

        o_h, states = _hgrn(proj, hgrn_lb_raw, hgrn_onorm_g[l], l, s_prev=states)
        o_h = _hgrn(proj, hgrn_lb_raw, hgrn_onorm_g[l], l, state0=state_hgrn, o_prev=o_h)

        gqa_w = (proj, gqa_qnorm_g[l], gqa_knorm_g[l], l)
        o_g, *gqa_kv = _gqa(*gqa_w, kv_prev=gqa_kv)
        o_g = _gqa(*gqa_w, latent_args=(cache_gqa_k, cache_gqa_v, cos_g, sin_g, o_g))

        diff_w = (proj, diff_qnorm_g[l], diff_knorm_g[l], diff_subln_g[l], diff_lambda[l], l)
        o_d, *diff_kv = _diff(*diff_w, kv_prev=diff_kv)
        o_d = _diff(*diff_w, latent_args=(cache_dk, cache_diff_v, cos_d, sin_d, o_d))

        x = _proj_out(x, mod[l], o_h, o_g, o_d, w_out, l)
        if l < DEPTH - 1:
            x = _ffn(x, *ffn_w, 1)
        else:
            y_prompt = _ffn(x, *ffn_w, 1, **ctx_rows, out_rows=N_CTX)
            y_sample = _ffn(x, *ffn_w, 1, **lat_rows, in_off=CTX_TILES, out_rows=N_LAT)

    return (y_prompt.reshape(BATCH, SEQ, D_MODEL), y_sample.reshape(DEC_BATCH, DEC_SEQ, D_MODEL),
            gqa_kv[0].reshape(BATCH, DEPTH, SEQ, GQA_KV_HEADS, HEAD_DIM),
            gqa_kv[1].reshape(BATCH, DEPTH, SEQ, GQA_KV_HEADS, HEAD_DIM),
            diff_kv[0].reshape(BATCH, DEPTH, SEQ, DIFF_HEADS, 2, DIFF_QK_DIM),
            diff_kv[1].reshape(BATCH, DEPTH, SEQ, DIFF_HEADS, HEAD_DIM),
            states)
```

```python
import functools
import math

import jax
import jax.numpy as jnp
from jax import lax
from jax.experimental import pallas as pl
from jax.experimental.pallas import tpu as pltpu

F32 = jnp.float32
BF16 = jnp.bfloat16

D_MODEL = 2048
BATCH = 16
SEQ = 256
DEPTH = 2
DEC_BATCH = 2
DEC_SEQ = 1024
PAST_LEN = 256
GRID_W = 64
HEAD_DIM = 128
HGRN_HEADS = 4
GQA_Q_HEADS = 6
GQA_KV_HEADS = 2
GQA_GROUP = GQA_Q_HEADS // GQA_KV_HEADS
DIFF_HEADS = 6
DIFF_QK_DIM = 64
FFN_DIM = 5632
N_MOD = 9
IN_WIDTH = 6144
ROPE_BASE = 10000.0
EPS = 1e-6
LOG2E = math.log2(math.e)

N_CTX = BATCH * SEQ
N_LAT = DEC_BATCH * DEC_SEQ
N_ROWS = N_CTX + N_LAT
N_COND = 1 + DEC_BATCH
COND_PAD = 8

COL_HQ, COL_HI, COL_HG, COL_HFF, COL_HFB = 0, 4, 8, 12, 16
COL_GQ, COL_GK, COL_GV = 20, 26, 28
COL_DQ, COL_DK, COL_DV = 30, 36, 42

HGRN_W = HGRN_HEADS * HEAD_DIM
GQA_W = GQA_Q_HEADS * HEAD_DIM
GQA_KV_W = GQA_KV_HEADS * HEAD_DIM
DIFF_W = DIFF_HEADS * HEAD_DIM

VMEM_LIMIT = 60 * 1024 * 1024

TM = 1024
ROW_CHUNK = 128
FFN_TF = 256
FFN_TILES_PER_STEP = 2
FFN_NB = 512
IN_TN = 512
IN_TILES_PER_STEP = 2
OUT_TN = 512
OUT_TILES_PER_STEP = 2
MOD_TN = 2048
HGRN_CHUNK = 64
HGRN_SUB = 8
HGRN_HP = 4
Q_BLOCK = 512
GQA_Q_BLOCK = 256
DIFF_HP_CTX = 6
DIFF_HP_LAT = 2

CTX_TILES = N_CTX // TM
LAT_TILES = N_LAT // TM
ALL_TILES = CTX_TILES + LAT_TILES


def _cond_of_tile(i):
    tiles_per_latent = DEC_SEQ // TM
    return jnp.where(i < CTX_TILES, 0, 1 + (i - CTX_TILES) // tiles_per_latent)


def _silu(x):
    return x * jax.nn.sigmoid(x)


def _dot(a, b):
    return jnp.dot(a, b, preferred_element_type=F32)


def _dot_nt(a, b):
    return lax.dot_general(a, b, (((1,), (1,)), ((), ())), preferred_element_type=F32)


def _dot_tn(a, b):
    return lax.dot_general(a, b, (((0,), (0,)), ((), ())), preferred_element_type=F32)


def _rms(x, g):
    ms = jnp.mean(x * x, axis=-1, keepdims=True)
    return (x * lax.rsqrt(ms + EPS)) * g


def _params(*semantics):
    return pltpu.CompilerParams(dimension_semantics=semantics, vmem_limit_bytes=VMEM_LIMIT)


_ANY = pl.BlockSpec(memory_space=pl.ANY)


def _mod_kernel(cond_ref, w_ref, b_ref, o_ref):
    a = _silu(cond_ref[...]).astype(BF16)
    o_ref[...] = _dot(a, w_ref[...].astype(BF16)) + b_ref[...]


def _modulation(cond, w_mod, b_mod):
    width = N_MOD * D_MODEL
    out = pl.pallas_call(
        _mod_kernel,
        out_shape=jax.ShapeDtypeStruct((DEPTH, COND_PAD, width), F32),
        grid=(DEPTH, width // MOD_TN),
        in_specs=[
            pl.BlockSpec((COND_PAD, D_MODEL), lambda l, j: (0, 0)),
            pl.BlockSpec((None, D_MODEL, MOD_TN), lambda l, j: (l, 0, j)),
            pl.BlockSpec((None, 1, MOD_TN), lambda l, j: (l, 0, j)),
        ],
        out_specs=pl.BlockSpec((None, COND_PAD, MOD_TN), lambda l, j: (l, 0, j)),
        compiler_params=_params("parallel", "parallel"),
        name="modulation",
    )(cond, w_mod, b_mod.reshape(DEPTH, 1, width))
    return out[:, :N_COND].reshape(DEPTH, N_COND, N_MOD, D_MODEL)


def _mod_norm_into(x_ref, mod_ref, g_ref, h_ref, sub, row0=0):
    shift = mod_ref[3 * sub:3 * sub + 1, :]
    gain = g_ref[sub:sub + 1, :] * (1.0 + mod_ref[3 * sub + 1:3 * sub + 2, :])

    def body(r, carry):
        rows = pl.ds(pl.multiple_of(row0 + r * ROW_CHUNK, ROW_CHUNK), ROW_CHUNK)
        h_ref[rows, :] = (_rms(x_ref[rows, :], gain) + shift).astype(BF16)
        return carry

    lax.fori_loop(0, TM // ROW_CHUNK, body, 0)


def _ffn_kernel(x_ref, mod_ref, g_ref, wg_ref, wu_ref, wd_ref, *rest, sub, tile0):
    o_ref, h_ref = rest[-2:]
    j = pl.program_id(1)
    conds = [_cond_of_tile(tile0 + pl.program_id(0) * FFN_TILES_PER_STEP + part)
             for part in range(FFN_TILES_PER_STEP)]

    @pl.when(j == 0)
    def _():
        for part, cond in enumerate(conds):
            _mod_norm_into(x_ref, mod_ref.at[cond], g_ref, h_ref, sub, row0=part * TM)
        o_ref[...] = jnp.zeros_like(o_ref)

    for part in range(FFN_TILES_PER_STEP):
        rows = slice(part * TM, (part + 1) * TM)
        h = h_ref[rows, :]
        gate_act = _dot(h, wg_ref[...].astype(BF16))
        up = _dot(h, wu_ref[...].astype(BF16))
        a = (_silu(gate_act) * up).astype(BF16)
        for n in range(0, D_MODEL, FFN_NB):
            o_ref[rows, n:n + FFN_NB] += _dot(a, wd_ref[:, n:n + FFN_NB].astype(BF16))

    @pl.when(j == pl.num_programs(1) - 1)
    def _():
        for part, cond in enumerate(conds):
            gate = mod_ref[cond, 3 * sub + 2:3 * sub + 3, :]

            def body(r, carry):
                rows = pl.ds(pl.multiple_of(part * TM + r * ROW_CHUNK, ROW_CHUNK), ROW_CHUNK)
                o_ref[rows, :] = x_ref[rows, :] + gate * (0.5 * o_ref[rows, :])
                return carry

            lax.fori_loop(0, TM // ROW_CHUNK, body, 0)


def _ffn(x, mod_l, norm_g_l, w_gate, w_up, w_down, layer, which, *, n_tiles=ALL_TILES, in_off=0,
         out_off=0, tile0=0, out_rows=N_ROWS, o_prev=None):
    sub = 2 * which
    per = FFN_TILES_PER_STEP
    rows = per * TM
    assert n_tiles % per == 0 and in_off % per == 0 and out_off % per == 0
    in_specs = [
        pl.BlockSpec((rows, D_MODEL), lambda i, j: (in_off // per + i, 0), pipeline_mode=pl.Buffered(1)),
        pl.BlockSpec((N_COND, N_MOD, D_MODEL), lambda i, j: (0, 0, 0)),
        pl.BlockSpec((3, D_MODEL), lambda i, j: (0, 0)),
        pl.BlockSpec((None, None, D_MODEL, FFN_TF), lambda i, j: (layer, which, 0, j)),
        pl.BlockSpec((None, None, D_MODEL, FFN_TF), lambda i, j: (layer, which, 0, j)),
        pl.BlockSpec((None, None, FFN_TF, D_MODEL), lambda i, j: (layer, which, j, 0)),
    ]
    args = [x, mod_l, norm_g_l, w_gate, w_up, w_down]
    aliases = {}
    if o_prev is not None:
        in_specs.append(_ANY)
        args.append(o_prev)
        aliases = {len(args) - 1: 0}
    return pl.pallas_call(
        functools.partial(_ffn_kernel, sub=sub, tile0=tile0),
        out_shape=jax.ShapeDtypeStruct((out_rows, D_MODEL), F32),
        grid=(n_tiles // per, FFN_DIM // FFN_TF),
        in_specs=in_specs,
        out_specs=pl.BlockSpec((rows, D_MODEL), lambda i, j: (out_off // per + i, 0),
                               pipeline_mode=pl.Buffered(1)),
        scratch_shapes=[pltpu.VMEM((rows, D_MODEL), BF16)],
        input_output_aliases=aliases,
        compiler_params=_params("parallel", "arbitrary"),
        name=f"ffn_l{layer}_h{which}_t{tile0}n{n_tiles}",
    )(*args)


def _proj_in_kernel(x_ref, mod_ref, g_ref, w_ref, o_ref, h_ref):
    @pl.when(pl.program_id(1) == 0)
    def _():
        for part in range(IN_TILES_PER_STEP):
            cond = _cond_of_tile(pl.program_id(0) * IN_TILES_PER_STEP + part)
            _mod_norm_into(x_ref, mod_ref.at[cond], g_ref, h_ref, 1, row0=part * TM)

    w = w_ref[...].astype(BF16)
    for part in range(IN_TILES_PER_STEP):
        rows = slice(part * TM, (part + 1) * TM)
        o_ref[rows, :] = _dot(h_ref[rows, :], w)


def _proj_in(x, mod_l, norm_g_l, w_in, layer):
    rows = IN_TILES_PER_STEP * TM
    return pl.pallas_call(
        _proj_in_kernel,
        out_shape=jax.ShapeDtypeStruct((N_ROWS, IN_WIDTH), F32),
        grid=(N_ROWS // rows, IN_WIDTH // IN_TN),
        in_specs=[
            pl.BlockSpec((rows, D_MODEL), lambda i, j: (i, 0)),
            pl.BlockSpec((N_COND, N_MOD, D_MODEL), lambda i, j: (0, 0, 0)),
            pl.BlockSpec((3, D_MODEL), lambda i, j: (0, 0)),
            pl.BlockSpec((None, D_MODEL, IN_TN), lambda i, j: (layer, 0, j)),
        ],
        out_specs=pl.BlockSpec((rows, IN_TN), lambda i, j: (i, j)),
        scratch_shapes=[pltpu.VMEM((rows, D_MODEL), BF16)],
        compiler_params=_params("parallel", "arbitrary"),
        name=f"proj_in_l{layer}",
    )(x, mod_l, norm_g_l, w_in)


def _proj_out_kernel(x_ref, mod_ref, oh_ref, og_ref, od_ref, w_ref, o_ref):
    w_h = w_ref[0:HGRN_W, :].astype(BF16)
    w_g = w_ref[HGRN_W:HGRN_W + GQA_W, :].astype(BF16)
    w_d = w_ref[HGRN_W + GQA_W:, :].astype(BF16)
    for part in range(OUT_TILES_PER_STEP):
        rows = slice(part * TM, (part + 1) * TM)
        cond = _cond_of_tile(pl.program_id(0) * OUT_TILES_PER_STEP + part)
        m = _dot(oh_ref[rows, :], w_h) + _dot(og_ref[rows, :], w_g) + _dot(od_ref[rows, :], w_d)
        o_ref[rows, :] = x_ref[rows, :] + mod_ref[cond, 5:6, :] * m


def _proj_out(x, mod_l, o_h, o_g, o_d, w_out, layer):
    rows = OUT_TILES_PER_STEP * TM
    return pl.pallas_call(
        _proj_out_kernel,
        out_shape=jax.ShapeDtypeStruct((N_ROWS, D_MODEL), F32),
        grid=(N_ROWS // rows, D_MODEL // OUT_TN),
        in_specs=[
            pl.BlockSpec((rows, OUT_TN), lambda i, j: (i, j)),
            pl.BlockSpec((N_COND, N_MOD, OUT_TN), lambda i, j: (0, 0, j)),
            pl.BlockSpec((rows, HGRN_W), lambda i, j: (i, 0)),
            pl.BlockSpec((rows, GQA_W), lambda i, j: (i, 0)),
            pl.BlockSpec((rows, DIFF_W), lambda i, j: (i, 0)),
            pl.BlockSpec((None, D_MODEL, OUT_TN), lambda i, j: (layer, 0, j)),
        ],
        out_specs=pl.BlockSpec((rows, OUT_TN), lambda i, j: (i, j)),
        compiler_params=_params("parallel", "parallel"),
        name=f"proj_out_l{layer}",
    )(x, mod_l, o_h, o_g, o_d, w_out)


def _log2_forget_and_key(z, lb):
    z2 = z * LOG2E
    soft = jnp.log2(1.0 + jnp.exp2(-jnp.abs(z2)))
    log_1mlb = jnp.log1p(-lb) * LOG2E
    a = jnp.log2(lb)
    c = log_1mlb + (jnp.minimum(z2, 0.0) - soft)
    log_f = jnp.maximum(a, c) + jnp.log2(1.0 + jnp.exp2(-jnp.abs(a - c)))
    log_k = log_1mlb + (jnp.minimum(-z2, 0.0) - soft)
    return log_f, log_k


def _cumsum_rows(x, reverse):
    tile = HGRN_SUB
    row = lax.broadcasted_iota(jnp.int32, (tile, 1), 0)
    tiles = []
    for j in range(x.shape[0] // tile):
        y = x[j * tile:(j + 1) * tile, :]
        for sh in (1, 2, 4):
            if reverse:
                y = y + jnp.where(row < tile - sh, pltpu.roll(y, tile - sh, 0), 0.0)
            else:
                y = y + jnp.where(row >= sh, pltpu.roll(y, sh, 0), 0.0)
        tiles.append(y)
    order = range(len(tiles) - 1, -1, -1) if reverse else range(len(tiles))
    carry = None
    for j in order:
        if carry is not None:
            tiles[j] = tiles[j] + carry
        carry = tiles[j][0:1, :] if reverse else tiles[j][tile - 1:tile, :]
    return jnp.concatenate(tiles, axis=0)


def _hgrn_prepare(q, z, lb, reverse):
    lf2, lk2 = _log2_forget_and_key(z, lb)
    b = _cumsum_rows(lf2, reverse)
    return _silu(q), b, b - lk2


def _hgrn_chunk(qs, b, c, v, st, reverse):
    C, SUB = HGRN_CHUNK, HGRN_SUB
    n_sub = C // SUB
    b_tot = b[0:1, :] if reverse else b[C - 1:C, :]

    o = _dot_nt((qs * jnp.exp2(b)).astype(BF16), st.astype(BF16))

    terms = []
    for i in range(n_sub):
        lo = i * SUB
        qi, bi, ci = qs[lo:lo + SUB, :], b[lo:lo + SUB, :], c[lo:lo + SUB, :]
        for s in range(SUB):
            terms.append(qi * jnp.exp2(bi - ci[s:s + 1, :]))
    k_sums = _dot(jnp.concatenate(terms, axis=0).astype(BF16), jnp.ones((HEAD_DIM, C), BF16))

    row = lax.broadcasted_iota(jnp.int32, (SUB, C), 0)
    lane = lax.broadcasted_iota(jnp.int32, (SUB, C), 1)
    lane_s = lane % SUB
    causal = (lane_s >= row) if reverse else (lane_s <= row)
    blocks = []
    for i in range(n_sub):
        lo, hi = i * SUB, (i + 1) * SUB
        diag = k_sums[lo * SUB:(lo + 1) * SUB, :]
        for s in range(1, SUB):
            diag = jnp.where(lane_s == s, k_sums[(lo + s) * SUB:(lo + s + 1) * SUB, :], diag)
        has_off = (i < n_sub - 1) if reverse else (i > 0)
        if has_off:
            ref = b[hi:hi + 1, :] if reverse else b[lo - 1:lo, :]
            qt = (qs[lo:hi, :] * jnp.exp2(b[lo:hi, :] - ref)).astype(BF16)
            if reverse:
                kt = jnp.concatenate([jnp.zeros((hi, HEAD_DIM), F32), jnp.exp2(ref - c[hi:, :])], axis=0)
            else:
                kt = jnp.concatenate([jnp.exp2(ref - c[:lo, :]), jnp.zeros((C - lo, HEAD_DIM), F32)], axis=0)
            off = _dot_nt(qt, kt.astype(BF16))
        else:
            off = jnp.zeros((SUB, C), F32)
        blocks.append(jnp.where((lane // SUB == i) & causal, diag, off))
    a = jnp.concatenate(blocks, axis=0)
    o = o + _dot(a.astype(BF16), v.astype(BF16))

    st_new = jnp.exp2(b_tot) * st + _dot_tn(v.astype(BF16), jnp.exp2(b_tot - c).astype(BF16))
    return o, st_new


def _hgrn_kernel(*refs, layer, seq, latent, n_alias):
    raw_ref, q_ref, v_ref, g_ref, ff_ref, fb_ref, ong_ref = refs[:7]
    s0_ref = refs[7] if latent else None
    outs = refs[7 + (1 if latent else 0) + n_alias:]
    if latent:
        o_ref, of_ref, ob_ref, st_ref, pre_ref = outs
        s_ref = None
    else:
        o_ref, s_ref, of_ref, ob_ref, st_ref, pre_ref = outs
    C = HGRN_CHUNK
    n_chunks = seq // C

    def lower_bound(d, lanes):
        rows = [raw_ref[2 * l + d:2 * l + d + 1, lanes] for l in range(DEPTH)]
        m = functools.reduce(jnp.maximum, rows)
        e = [jnp.exp(r - m) for r in rows]
        tot = functools.reduce(lambda x, y: x + y, e)
        lb = jnp.zeros_like(m)
        for l in range(1, layer + 1):
            lb = lb + e[l] / tot
        return lb

    head_lanes = [slice(hp * HEAD_DIM, (hp + 1) * HEAD_DIM) for hp in range(HGRN_HP)]
    lbs = [[lower_bound(d, lanes) for d in range(2)] for lanes in head_lanes]

    for hp in range(HGRN_HP):
        for d in range(2):
            st_ref[hp, d] = s0_ref[d, hp].T if latent else jnp.zeros((HEAD_DIM, HEAD_DIM), F32)

    def chunk_rows(ci):
        if isinstance(ci, int):
            return pl.ds(ci * C, C), pl.ds((n_chunks - 1 - ci) * C, C)
        return (pl.ds(pl.multiple_of(ci * C, C), C),
                pl.ds(pl.multiple_of((n_chunks - 1 - ci) * C, C), C))

    def prepare(ci, slot):
        rows = chunk_rows(ci)
        for hp, lanes in enumerate(head_lanes):
            for d, f_ref in enumerate((ff_ref, fb_ref)):
                pre = _hgrn_prepare(q_ref[rows[d], lanes], f_ref[rows[d], lanes], lbs[hp][d], d == 1)
                for n, val in enumerate(pre):
                    pre_ref[slot, 2 * hp + d, n] = val

    prepare(0, 0)

    def consume(ci, slot):
        rows = chunk_rows(ci)
        for hp, lanes in enumerate(head_lanes):
            for d, acc_ref in enumerate((of_ref, ob_ref)):
                qs, b, c = (pre_ref[slot, 2 * hp + d, n] for n in range(3))
                o, st = _hgrn_chunk(qs, b, c, v_ref[rows[d], lanes], st_ref[hp, d], d == 1)
                acc_ref[rows[d], lanes] = o
                st_ref[hp, d] = st

    def body(ci, carry):
        slot = ci % 2
        consume(ci, slot)
        prepare(ci + 1, 1 - slot)
        return carry

    lax.fori_loop(0, n_chunks - 1, body, 0)
    consume(n_chunks - 1, (n_chunks - 1) % 2)

    for hp, lanes in enumerate(head_lanes):
        if s_ref is not None:
            for d in range(2):
                s_ref[d, hp] = st_ref[hp, d].T
        o = of_ref[:, lanes] + ob_ref[:, lanes]
        o_ref[:, lanes] = (_rms(o, ong_ref[...]) * _silu(g_ref[:, lanes])).astype(o_ref.dtype)


def _hgrn(proj, lb_raw, onorm_g_l, layer, *, state0=None, o_prev=None, s_prev=None):
    latent = state0 is not None
    n_batch, seq, row0 = (DEC_BATCH, DEC_SEQ, N_CTX // DEC_SEQ) if latent else (BATCH, SEQ, 0)
    width = HGRN_HP * HEAD_DIM

    def col(block):
        return pl.BlockSpec((seq, width), lambda b, h: (row0 + b, block // HGRN_HP + h))

    in_specs = [
        pl.BlockSpec((2 * DEPTH, width), lambda b, h: (0, h)),
        col(COL_HQ), col(COL_HI), col(COL_HG), col(COL_HFF), col(COL_HFB),
        pl.BlockSpec((1, HEAD_DIM), lambda b, h: (0, 0)),
    ]
    args = [lb_raw.reshape(2 * DEPTH, HGRN_W), proj, proj, proj, proj, proj,
            onorm_g_l.reshape(1, HEAD_DIM)]
    o_shape = jax.ShapeDtypeStruct((N_ROWS, HGRN_W), BF16)
    o_spec = pl.BlockSpec((seq, width), lambda b, h: (row0 + b, h))
    aliases = {}
    if latent:
        in_specs += [pl.BlockSpec((None, None, 2, HGRN_HP, HEAD_DIM, HEAD_DIM),
                                  lambda b, h: (b, layer, 0, h, 0, 0)), _ANY]
        args += [state0, o_prev]
        aliases = {len(args) - 1: 0}
        out_shape, out_specs = o_shape, o_spec
    else:
        if s_prev is not None:
            in_specs.append(_ANY)
            args.append(s_prev)
            aliases = {len(args) - 1: 1}
        out_shape = (o_shape, jax.ShapeDtypeStruct(
            (BATCH, DEPTH, 2, HGRN_HEADS, HEAD_DIM, HEAD_DIM), F32))
        out_specs = (o_spec, pl.BlockSpec((None, None, 2, HGRN_HP, HEAD_DIM, HEAD_DIM),
                                          lambda b, h: (b, layer, 0, h, 0, 0)))
    return pl.pallas_call(
        functools.partial(_hgrn_kernel, layer=layer, seq=seq, latent=latent, n_alias=len(aliases)),
        out_shape=out_shape,
        grid=(n_batch, HGRN_HEADS // HGRN_HP),
        in_specs=in_specs,
        out_specs=out_specs,
        scratch_shapes=[pltpu.VMEM((seq, width), F32), pltpu.VMEM((seq, width), F32),
                        pltpu.VMEM((HGRN_HP, 2, HEAD_DIM, HEAD_DIM), F32),
                        pltpu.VMEM((2, 2 * HGRN_HP, 3, HGRN_CHUNK, HEAD_DIM), F32)],
        input_output_aliases=aliases,
        compiler_params=_params("parallel", "parallel"),
        name=f"hgrn_l{layer}_{'lat' if latent else 'ctx'}",
    )(*args)


def _swap_pairs(x, width):
    lanes = x.shape[-1]
    lane = lax.broadcasted_iota(jnp.int32, x.shape, x.ndim - 1)
    from_right = pltpu.roll(x, lanes - width, x.ndim - 1)
    from_left = pltpu.roll(x, width, x.ndim - 1)
    return jnp.where(lane % (2 * width) < width, from_right, from_left)


def _rope(x, cos, sin_signed, quarter):
    return x * cos + _swap_pairs(x, quarter) * sin_signed


def _rms_halves(x, g):
    half = x.shape[-1] // 2
    lane = lax.broadcasted_iota(jnp.int32, x.shape, x.ndim - 1)
    lo = lane < half
    sq = x * x
    ms_lo = jnp.sum(jnp.where(lo, sq, 0.0), axis=-1, keepdims=True) / half
    ms_hi = jnp.sum(jnp.where(lo, 0.0, sq), axis=-1, keepdims=True) / half
    ms = jnp.where(lo, ms_lo, ms_hi)
    return (x * lax.rsqrt(ms + EPS)) * g


def _gqa_kernel(*refs, seq, latent, n_alias):
    qg_ref, kg_ref = refs[:2]
    q_refs = refs[2:2 + GQA_GROUP]
    k_ref, v_ref = refs[2 + GQA_GROUP:4 + GQA_GROUP]
    rest = refs[4 + GQA_GROUP:]
    if latent:
        ck_ref, cv_ref, cos_ref, sin_ref = rest[:4]
        o_ref, kt_ref, vt_ref = rest[4 + n_alias:]
    else:
        o_ref, kn_ref, vo_ref, kt_ref, vt_ref = rest[n_alias:]
    past = PAST_LEN if latent else 0
    quarter = HEAD_DIM // 4

    kn = _rms(k_ref[...], kg_ref[...])
    v = v_ref[...]
    if latent:
        kn = _rope(kn, cos_ref[...], sin_ref[...], quarter)
        kt_ref[0:past, :] = ck_ref[...].astype(BF16)
        vt_ref[0:past, 0:HEAD_DIM] = cv_ref[...].astype(BF16)
    else:
        kn_ref[...] = kn
        vo_ref[...] = v
    kt_ref[past:past + seq, :] = kn.astype(BF16)
    vt_ref[past:past + seq, 0:HEAD_DIM] = v.astype(BF16)
    vt_ref[:, HEAD_DIM:] = jnp.ones((past + seq, HEAD_DIM), BF16)

    q_scale = HEAD_DIM ** -0.5 * LOG2E

    def attend(q_rows):
        s = _dot_nt(jnp.concatenate(q_rows, axis=0), kt_ref[...])
        p = jnp.exp2(s - jnp.max(s, axis=-1, keepdims=True))
        ov = _dot(p.astype(BF16), vt_ref[...])
        o = (ov[:, :HEAD_DIM] / ov[:, HEAD_DIM:]).astype(o_ref.dtype)
        n = q_rows[0].shape[0]
        return [o[i * n:(i + 1) * n, :] for i in range(len(q_rows))]

    def query(q_ref, rows):
        qn = _rms(q_ref[rows, :], qg_ref[...])
        if latent:
            qn = _rope(qn, cos_ref[rows, :], sin_ref[rows, :], quarter)
        return (qn * q_scale).astype(BF16)

    head_lanes = [slice(g * HEAD_DIM, (g + 1) * HEAD_DIM) for g in range(GQA_GROUP)]
    if seq <= GQA_Q_BLOCK:
        rows = slice(0, seq)
        outs = attend([query(q_ref, rows) for q_ref in q_refs])
        for lanes, o in zip(head_lanes, outs):
            o_ref[rows, lanes] = o
    else:
        for lanes, q_ref in zip(head_lanes, q_refs):
            for r in range(seq // GQA_Q_BLOCK):
                rows = slice(r * GQA_Q_BLOCK, (r + 1) * GQA_Q_BLOCK)
                o_ref[rows, lanes] = attend([query(q_ref, rows)])[0]


def _gqa(proj, qnorm_g, knorm_g, layer, *, latent_args=None, kv_prev=None):
    latent = latent_args is not None
    n_batch, seq, row0 = (DEC_BATCH, DEC_SEQ, N_CTX // DEC_SEQ) if latent else (BATCH, SEQ, 0)
    past = PAST_LEN if latent else 0

    def vec():
        return pl.BlockSpec((1, HEAD_DIM), lambda b, h: (0, 0))

    def q_spec(g):
        return pl.BlockSpec((seq, HEAD_DIM), lambda b, h: (row0 + b, COL_GQ + h * GQA_GROUP + g))

    in_specs = [vec(), vec()] + [q_spec(g) for g in range(GQA_GROUP)] + [
        pl.BlockSpec((seq, HEAD_DIM), lambda b, h: (row0 + b, COL_GK + h)),
        pl.BlockSpec((seq, HEAD_DIM), lambda b, h: (row0 + b, COL_GV + h)),
    ]
    args = ([qnorm_g.reshape(1, HEAD_DIM), knorm_g.reshape(1, HEAD_DIM)]
            + [proj] * (GQA_GROUP + 2))
    o_spec = pl.BlockSpec((seq, GQA_GROUP * HEAD_DIM), lambda b, h: (row0 + b, h))
    o_shape = jax.ShapeDtypeStruct((N_ROWS, GQA_W), BF16)
    cache_spec = pl.BlockSpec((None, None, SEQ, HEAD_DIM), lambda b, h: (b, layer, 0, h))
    aliases = {}
    if latent:
        cache_k, cache_v, cos, sin_signed, o_prev = latent_args
        table_spec = pl.BlockSpec((seq, HEAD_DIM), lambda b, h: (0, 0))
        in_specs += [cache_spec, cache_spec, table_spec, table_spec, _ANY]
        args += [cache_k.reshape(DEC_BATCH, DEPTH, PAST_LEN, GQA_KV_W),
                 cache_v.reshape(DEC_BATCH, DEPTH, PAST_LEN, GQA_KV_W), cos, sin_signed, o_prev]
        aliases = {len(args) - 1: 0}
        out_shape, out_specs = o_shape, o_spec
    else:
        if kv_prev is not None:
            in_specs += [_ANY, _ANY]
            args += list(kv_prev)
            aliases = {len(args) - 2: 1, len(args) - 1: 2}
        new_shape = jax.ShapeDtypeStruct((BATCH, DEPTH, SEQ, GQA_KV_W), F32)
        out_shape = (o_shape, new_shape, new_shape)
        out_specs = (o_spec, cache_spec, cache_spec)
    return pl.pallas_call(
        functools.partial(_gqa_kernel, seq=seq, latent=latent, n_alias=len(aliases)),
        out_shape=out_shape,
        grid=(n_batch, GQA_KV_HEADS),
        in_specs=in_specs,
        out_specs=out_specs,
        scratch_shapes=[pltpu.VMEM((past + seq, HEAD_DIM), BF16),
                        pltpu.VMEM((past + seq, 2 * HEAD_DIM), BF16)],
        input_output_aliases=aliases,
        compiler_params=_params("parallel", "parallel"),
        name=f"gqa_l{layer}_{'lat' if latent else 'ctx'}",
    )(*args)


def _diff_kernel(*refs, seq, latent, lam_init, n_alias, heads):
    qg_ref, kg_ref, sg_ref, lam_ref, q_ref, k_ref, v_ref = refs[:7]
    if latent:
        ck_ref, cv_ref, cos_ref, sin_ref = refs[7:11]
        o_ref, kt_ref, vt_ref = refs[11 + n_alias:]
    else:
        o_ref, kn_ref, vo_ref, kt_ref, vt_ref = refs[7 + n_alias:]
    past = PAST_LEN if latent else 0
    quarter = DIFF_QK_DIM // 4
    head_lanes = [slice(hd * HEAD_DIM, (hd + 1) * HEAD_DIM) for hd in range(heads)]

    for hd, lanes in enumerate(head_lanes):
        kn = _rms_halves(k_ref[:, lanes], kg_ref[...])
        v = v_ref[:, lanes]
        if latent:
            kn = _rope(kn, cos_ref[...], sin_ref[...], quarter)
            kt_ref[0:past, lanes] = ck_ref[:, lanes].astype(BF16)
            vt_ref[hd, 0:past, 0:HEAD_DIM] = cv_ref[:, lanes].astype(BF16)
        else:
            kn_ref[:, lanes] = kn
            vo_ref[:, lanes] = v
        kt_ref[past:past + seq, lanes] = kn.astype(BF16)
        vt_ref[hd, past:past + seq, 0:HEAD_DIM] = v.astype(BF16)
        vt_ref[hd, :, HEAD_DIM:] = jnp.ones((past + seq, HEAD_DIM), BF16)

    lv = lam_ref[...]
    lam = (jnp.exp(jnp.sum(lv[0:1, :] * lv[1:2, :], axis=-1, keepdims=True))
           - jnp.exp(jnp.sum(lv[2:3, :] * lv[3:4, :], axis=-1, keepdims=True)) + lam_init)
    q_scale = DIFF_QK_DIM ** -0.5 * LOG2E

    def attend(q_masked, kt, vt):
        s = _dot_nt(q_masked.astype(BF16), kt)
        p = jnp.exp2(s - jnp.max(s, axis=-1, keepdims=True))
        ov = _dot(p.astype(BF16), vt)
        return ov[:, :HEAD_DIM] / ov[:, HEAD_DIM:]

    q_block = min(seq, Q_BLOCK)
    for hd, lanes in enumerate(head_lanes):
        for r in range(seq // q_block):
            rows = slice(r * q_block, (r + 1) * q_block)
            qn = _rms_halves(q_ref[rows, lanes], qg_ref[...])
            if latent:
                qn = _rope(qn, cos_ref[rows, :], sin_ref[rows, :], quarter)
            qn = qn * q_scale
            lo = lax.broadcasted_iota(jnp.int32, qn.shape, 1) < DIFF_QK_DIM
            kt, vt = kt_ref[:, lanes], vt_ref[hd]
            o = attend(jnp.where(lo, qn, 0.0), kt, vt) - lam * attend(jnp.where(lo, 0.0, qn), kt, vt)
            o = _rms(o, sg_ref[...]) * (1.0 - lam_init)
            o_ref[rows, lanes] = o.astype(o_ref.dtype)


def _diff(proj, qnorm_g, knorm_g, subln_g, lam_params, layer, *, latent_args=None, kv_prev=None):
    latent = latent_args is not None
    n_batch, seq, row0 = (DEC_BATCH, DEC_SEQ, N_CTX // DEC_SEQ) if latent else (BATCH, SEQ, 0)
    past = PAST_LEN if latent else 0
    lam_init = 0.8 - 0.6 * math.exp(-0.3 * layer)
    heads = DIFF_HP_LAT if latent else DIFF_HP_CTX
    width = heads * HEAD_DIM

    def vec():
        return pl.BlockSpec((1, HEAD_DIM), lambda b, h: (0, 0))

    def col(block):
        return pl.BlockSpec((seq, width), lambda b, h: (row0 + b, block // heads + h))

    in_specs = [
        vec(), vec(), vec(),
        pl.BlockSpec((4, DIFF_QK_DIM), lambda b, h: (0, 0)),
        col(COL_DQ), col(COL_DK), col(COL_DV),
    ]
    args = [jnp.tile(qnorm_g, 2).reshape(1, HEAD_DIM), jnp.tile(knorm_g, 2).reshape(1, HEAD_DIM),
            subln_g.reshape(1, HEAD_DIM), lam_params, proj, proj, proj]
    o_spec = pl.BlockSpec((seq, width), lambda b, h: (row0 + b, h))
    o_shape = jax.ShapeDtypeStruct((N_ROWS, DIFF_W), BF16)
    cache_spec = pl.BlockSpec((None, None, SEQ, width), lambda b, h: (b, layer, 0, h))
    aliases = {}
    if latent:
        cache_k, cache_v, cos, sin_signed, o_prev = latent_args
        table_spec = pl.BlockSpec((seq, HEAD_DIM), lambda b, h: (0, 0))
        in_specs += [cache_spec, cache_spec, table_spec, table_spec, _ANY]
        args += [cache_k.reshape(DEC_BATCH, DEPTH, PAST_LEN, DIFF_W),
                 cache_v.reshape(DEC_BATCH, DEPTH, PAST_LEN, DIFF_W), cos, sin_signed, o_prev]
        aliases = {len(args) - 1: 0}
        out_shape, out_specs = o_shape, o_spec
    else:
        if kv_prev is not None:
            in_specs += [_ANY, _ANY]
            args += list(kv_prev)
            aliases = {len(args) - 2: 1, len(args) - 1: 2}
        new_shape = jax.ShapeDtypeStruct((BATCH, DEPTH, SEQ, DIFF_W), F32)
        out_shape = (o_shape, new_shape, new_shape)
        out_specs = (o_spec, cache_spec, cache_spec)
    return pl.pallas_call(
        functools.partial(_diff_kernel, seq=seq, latent=latent, lam_init=lam_init,
                          n_alias=len(aliases), heads=heads),
        out_shape=out_shape,
        grid=(n_batch, DIFF_HEADS // heads),
        in_specs=in_specs,
        out_specs=out_specs,
        scratch_shapes=[pltpu.VMEM((past + seq, width), BF16),
                        pltpu.VMEM((heads, past + seq, 2 * HEAD_DIM), BF16)],
        input_output_aliases=aliases,
        compiler_params=_params("parallel", "parallel"),
        name=f"diff_l{layer}_{'lat' if latent else 'ctx'}",
    )(*args)


def _rope_tables(n_tokens, dim, repeat):
    quarter = dim // 4
    t = jnp.arange(n_tokens)
    pos = jnp.stack([t // GRID_W, t % GRID_W], axis=-1).astype(F32)
    inv = ROPE_BASE ** (-jnp.arange(quarter, dtype=F32) / quarter)
    ang = pos[:, :, None] * inv
    cos, sin = jnp.cos(ang), jnp.sin(ang)
    cos_l = jnp.concatenate([cos, cos], axis=-1).reshape(n_tokens, dim)
    sin_l = jnp.concatenate([-sin, sin], axis=-1).reshape(n_tokens, dim)
    return jnp.tile(cos_l, (1, repeat)), jnp.tile(sin_l, (1, repeat))


def kernel(x_prompt, x_sample, c, cache_gqa_k, cache_gqa_v, cache_diff_k, cache_diff_v, state_hgrn,
           c_ctx, w_mod, b_mod, norm_g, ffn_w_gate, ffn_w_up, ffn_w_down, w_in, w_out, hgrn_lb_raw,
           hgrn_onorm_g, gqa_qnorm_g, gqa_knorm_g, diff_qnorm_g, diff_knorm_g, diff_lambda,
           diff_subln_g):
    cond = jnp.concatenate(
        [c_ctx[None, :], c, jnp.zeros((COND_PAD - N_COND, D_MODEL), F32)], axis=0)
    mod = _modulation(cond, w_mod, b_mod)

    cos_g, sin_g = _rope_tables(DEC_SEQ, HEAD_DIM, 1)
    cos_d, sin_d = _rope_tables(DEC_SEQ, DIFF_QK_DIM, 2)
    cache_dk = cache_diff_k.reshape(DEC_BATCH, DEPTH, PAST_LEN, DIFF_HEADS, 2 * DIFF_QK_DIM)
    ctx_rows = dict(n_tiles=CTX_TILES, tile0=0)
    lat_rows = dict(n_tiles=LAT_TILES, tile0=CTX_TILES)

    x = None
    states = gqa_kv = diff_kv = None
    for l in range(DEPTH):
        ffn_w = (mod[l], norm_g[l], ffn_w_gate, ffn_w_up, ffn_w_down, l)
        if l == 0:
            x = _ffn(x_prompt.reshape(N_CTX, D_MODEL), *ffn_w, 0, **ctx_rows)
            x = _ffn(x_sample.reshape(N_LAT, D_MODEL), *ffn_w, 0, **lat_rows,
                     out_off=CTX_TILES, o_prev=x)
        else:
            x = _ffn(x, *ffn_w, 0)
        proj = _proj_in(x, mod[l], norm_g[l], w_in, l)

        o_h, states = _hgrn(proj, hgrn_lb_raw, hgrn_onorm_g[l], l, s_prev=states)
        o_h = _hgrn(proj, hgrn_lb_raw, hgrn_onorm_g[l], l, state0=state_hgrn, o_prev=o_h)

        gqa_w = (proj, gqa_qnorm_g[l], gqa_knorm_g[l], l)
        o_g, *gqa_kv = _gqa(*gqa_w, kv_prev=gqa_kv)
        o_g = _gqa(*gqa_w, latent_args=(cache_gqa_k, cache_gqa_v, cos_g, sin_g, o_g))

        diff_w = (proj, diff_qnorm_g[l], diff_knorm_g[l], diff_subln_g[l], diff_lambda[l], l)
        o_d, *diff_kv = _diff(*diff_w, kv_prev=diff_kv)
        o_d = _diff(*diff_w, latent_args=(cache_dk, cache_diff_v, cos_d, sin_d, o_d))

        x = _proj_out(x, mod[l], o_h, o_g, o_d, w_out, l)
        if l < DEPTH - 1:
            x = _ffn(x, *ffn_w, 1)
        else:
            y_prompt = _ffn(x, *ffn_w, 1, **ctx_rows, out_rows=N_CTX)
            y_sample = _ffn(x, *ffn_w, 1, **lat_rows, in_off=CTX_TILES, out_rows=N_LAT)

    return (y_prompt.reshape(BATCH, SEQ, D_MODEL), y_sample.reshape(DEC_BATCH, DEC_SEQ, D_MODEL),
            gqa_kv[0].reshape(BATCH, DEPTH, SEQ, GQA_KV_HEADS, HEAD_DIM),
            gqa_kv[1].reshape(BATCH, DEPTH, SEQ, GQA_KV_HEADS, HEAD_DIM),
            diff_kv[0].reshape(BATCH, DEPTH, SEQ, DIFF_HEADS, 2, DIFF_QK_DIM),
            diff_kv[1].reshape(BATCH, DEPTH, SEQ, DIFF_HEADS, HEAD_DIM),
            states)
```

```python
import functools
import math

import jax
import jax.numpy as jnp
from jax import lax
from jax.experimental import pallas as pl
from jax.experimental.pallas import tpu as pltpu

F32 = jnp.float32
BF16 = jnp.bfloat16

D_MODEL = 2048
BATCH = 16
SEQ = 256
DEPTH = 2
DEC_BATCH = 2
DEC_SEQ = 1024
PAST_LEN = 256
GRID_W = 64
HEAD_DIM = 128
HGRN_HEADS = 4
GQA_Q_HEADS = 6
GQA_KV_HEADS = 2
GQA_GROUP = GQA_Q_HEADS // GQA_KV_HEADS
DIFF_HEADS = 6
DIFF_QK_DIM = 64
FFN_DIM = 5632
N_MOD = 9
IN_WIDTH = 6144
ROPE_BASE = 10000.0
EPS = 1e-6
LOG2E = math.log2(math.e)

N_CTX = BATCH * SEQ
N_LAT = DEC_BATCH * DEC_SEQ
N_ROWS = N_CTX + N_LAT
N_COND = 1 + DEC_BATCH
COND_PAD = 8

COL_HQ, COL_HI, COL_HG, COL_HFF, COL_HFB = 0, 4, 8, 12, 16
COL_GQ, COL_GK, COL_GV = 20, 26, 28
COL_DQ, COL_DK, COL_DV = 30, 36, 42

HGRN_W = HGRN_HEADS * HEAD_DIM
GQA_W = GQA_Q_HEADS * HEAD_DIM
GQA_KV_W = GQA_KV_HEADS * HEAD_DIM
DIFF_W = DIFF_HEADS * HEAD_DIM

VMEM_LIMIT = 60 * 1024 * 1024

TM = 1024
ROW_CHUNK = 128
FFN_TF = 256
FFN_TILES_PER_STEP = 2
FFN_NB = 512
IN_TN = 512
IN_TILES_PER_STEP = 2
OUT_TN = 512
OUT_TILES_PER_STEP = 2
MOD_TN = 2048
HGRN_CHUNK = 64
HGRN_SUB = 8
HGRN_HP = 4
Q_BLOCK = 256
GQA_Q_BLOCK = 256
DIFF_HP_CTX = 6
DIFF_HP_LAT = 2

CTX_TILES = N_CTX // TM
LAT_TILES = N_LAT // TM
ALL_TILES = CTX_TILES + LAT_TILES


def _cond_of_tile(i):
    tiles_per_latent = DEC_SEQ // TM
    return jnp.where(i < CTX_TILES, 0, 1 + (i - CTX_TILES) // tiles_per_latent)


def _silu(x):
    return x * jax.nn.sigmoid(x)


def _dot(a, b):
    return jnp.dot(a, b, preferred_element_type=F32)


def _dot_nt(a, b):
    return lax.dot_general(a, b, (((1,), (1,)), ((), ())), preferred_element_type=F32)


def _dot_tn(a, b):
    return lax.dot_general(a, b, (((0,), (0,)), ((), ())), preferred_element_type=F32)


def _rms(x, g):
    ms = jnp.mean(x * x, axis=-1, keepdims=True)
    return (x * lax.rsqrt(ms + EPS)) * g


def _params(*semantics):
    return pltpu.CompilerParams(dimension_semantics=semantics, vmem_limit_bytes=VMEM_LIMIT)


_ANY = pl.BlockSpec(memory_space=pl.ANY)


def _mod_kernel(cond_ref, w_ref, b_ref, o_ref):
    a = _silu(cond_ref[...]).astype(BF16)
    o_ref[...] = _dot(a, w_ref[...].astype(BF16)) + b_ref[...]


def _modulation(cond, w_mod, b_mod):
    width = N_MOD * D_MODEL
    out = pl.pallas_call(
        _mod_kernel,
        out_shape=jax.ShapeDtypeStruct((DEPTH, COND_PAD, width), F32),
        grid=(DEPTH, width // MOD_TN),
        in_specs=[
            pl.BlockSpec((COND_PAD, D_MODEL), lambda l, j: (0, 0)),
            pl.BlockSpec((None, D_MODEL, MOD_TN), lambda l, j: (l, 0, j)),
            pl.BlockSpec((None, 1, MOD_TN), lambda l, j: (l, 0, j)),
        ],
        out_specs=pl.BlockSpec((None, COND_PAD, MOD_TN), lambda l, j: (l, 0, j)),
        compiler_params=_params("parallel", "parallel"),
        name="modulation",
    )(cond, w_mod, b_mod.reshape(DEPTH, 1, width))
    return out[:, :N_COND].reshape(DEPTH, N_COND, N_MOD, D_MODEL)


def _mod_norm_into(x_ref, mod_ref, g_ref, h_ref, sub, row0=0):
    shift = mod_ref[3 * sub:3 * sub + 1, :]
    gain = g_ref[sub:sub + 1, :] * (1.0 + mod_ref[3 * sub + 1:3 * sub + 2, :])

    def body(r, carry):
        rows = pl.ds(pl.multiple_of(row0 + r * ROW_CHUNK, ROW_CHUNK), ROW_CHUNK)
        h_ref[rows, :] = (_rms(x_ref[rows, :], gain) + shift).astype(BF16)
        return carry

    lax.fori_loop(0, TM // ROW_CHUNK, body, 0)


def _ffn_kernel(x_ref, mod_ref, g_ref, wg_ref, wu_ref, wd_ref, *rest, sub, tile0):
    o_ref, h_ref = rest[-2:]
    j = pl.program_id(1)
    conds = [_cond_of_tile(tile0 + pl.program_id(0) * FFN_TILES_PER_STEP + part)
             for part in range(FFN_TILES_PER_STEP)]

    @pl.when(j == 0)
    def _():
        for part, cond in enumerate(conds):
            _mod_norm_into(x_ref, mod_ref.at[cond], g_ref, h_ref, sub, row0=part * TM)
        o_ref[...] = jnp.zeros_like(o_ref)

    for part in range(FFN_TILES_PER_STEP):
        rows = slice(part * TM, (part + 1) * TM)
        h = h_ref[rows, :]
        gate_act = _dot(h, wg_ref[...].astype(BF16))
        up = _dot(h, wu_ref[...].astype(BF16))
        a = (_silu(gate_act) * up).astype(BF16)
        for n in range(0, D_MODEL, FFN_NB):
            o_ref[rows, n:n + FFN_NB] += _dot(a, wd_ref[:, n:n + FFN_NB].astype(BF16))

    @pl.when(j == pl.num_programs(1) - 1)
    def _():
        for part, cond in enumerate(conds):
            gate = mod_ref[cond, 3 * sub + 2:3 * sub + 3, :]

            def body(r, carry):
                rows = pl.ds(pl.multiple_of(part * TM + r * ROW_CHUNK, ROW_CHUNK), ROW_CHUNK)
                o_ref[rows, :] = x_ref[rows, :] + gate * (0.5 * o_ref[rows, :])
                return carry

            lax.fori_loop(0, TM // ROW_CHUNK, body, 0)


def _ffn(x, mod_l, norm_g_l, w_gate, w_up, w_down, layer, which, *, n_tiles=ALL_TILES, in_off=0,
         out_off=0, tile0=0, out_rows=N_ROWS, o_prev=None):
    sub = 2 * which
    per = FFN_TILES_PER_STEP
    rows = per * TM
    assert n_tiles % per == 0 and in_off % per == 0 and out_off % per == 0
    in_specs = [
        pl.BlockSpec((rows, D_MODEL), lambda i, j: (in_off // per + i, 0), pipeline_mode=pl.Buffered(1)),
        pl.BlockSpec((N_COND, N_MOD, D_MODEL), lambda i, j: (0, 0, 0)),
        pl.BlockSpec((3, D_MODEL), lambda i, j: (0, 0)),
        pl.BlockSpec((None, None, D_MODEL, FFN_TF), lambda i, j: (layer, which, 0, j)),
        pl.BlockSpec((None, None, D_MODEL, FFN_TF), lambda i, j: (layer, which, 0, j)),
        pl.BlockSpec((None, None, FFN_TF, D_MODEL), lambda i, j: (layer, which, j, 0)),
    ]
    args = [x, mod_l, norm_g_l, w_gate, w_up, w_down]
    aliases = {}
    if o_prev is not None:
        in_specs.append(_ANY)
        args.append(o_prev)
        aliases = {len(args) - 1: 0}
    return pl.pallas_call(
        functools.partial(_ffn_kernel, sub=sub, tile0=tile0),
        out_shape=jax.ShapeDtypeStruct((out_rows, D_MODEL), F32),
        grid=(n_tiles // per, FFN_DIM // FFN_TF),
        in_specs=in_specs,
        out_specs=pl.BlockSpec((rows, D_MODEL), lambda i, j: (out_off // per + i, 0),
                               pipeline_mode=pl.Buffered(1)),
        scratch_shapes=[pltpu.VMEM((rows, D_MODEL), BF16)],
        input_output_aliases=aliases,
        compiler_params=_params("parallel", "arbitrary"),
        name=f"ffn_l{layer}_h{which}_t{tile0}n{n_tiles}",
    )(*args)


def _proj_in_kernel(x_ref, mod_ref, g_ref, w_ref, o_ref, h_ref):
    @pl.when(pl.program_id(1) == 0)
    def _():
        for part in range(IN_TILES_PER_STEP):
            cond = _cond_of_tile(pl.program_id(0) * IN_TILES_PER_STEP + part)
            _mod_norm_into(x_ref, mod_ref.at[cond], g_ref, h_ref, 1, row0=part * TM)

    w = w_ref[...].astype(BF16)
    for part in range(IN_TILES_PER_STEP):
        rows = slice(part * TM, (part + 1) * TM)
        o_ref[rows, :] = _dot(h_ref[rows, :], w)


def _proj_in(x, mod_l, norm_g_l, w_in, layer):
    rows = IN_TILES_PER_STEP * TM
    return pl.pallas_call(
        _proj_in_kernel,
        out_shape=jax.ShapeDtypeStruct((N_ROWS, IN_WIDTH), F32),
        grid=(N_ROWS // rows, IN_WIDTH // IN_TN),
        in_specs=[
            pl.BlockSpec((rows, D_MODEL), lambda i, j: (i, 0)),
            pl.BlockSpec((N_COND, N_MOD, D_MODEL), lambda i, j: (0, 0, 0)),
            pl.BlockSpec((3, D_MODEL), lambda i, j: (0, 0)),
            pl.BlockSpec((None, D_MODEL, IN_TN), lambda i, j: (layer, 0, j)),
        ],
        out_specs=pl.BlockSpec((rows, IN_TN), lambda i, j: (i, j)),
        scratch_shapes=[pltpu.VMEM((rows, D_MODEL), BF16)],
        compiler_params=_params("parallel", "arbitrary"),
        name=f"proj_in_l{layer}",
    )(x, mod_l, norm_g_l, w_in)


def _proj_out_kernel(x_ref, mod_ref, oh_ref, og_ref, od_ref, w_ref, o_ref):
    w_h = w_ref[0:HGRN_W, :].astype(BF16)
    w_g = w_ref[HGRN_W:HGRN_W + GQA_W, :].astype(BF16)
    w_d = w_ref[HGRN_W + GQA_W:, :].astype(BF16)
    for part in range(OUT_TILES_PER_STEP):
        rows = slice(part * TM, (part + 1) * TM)
        cond = _cond_of_tile(pl.program_id(0) * OUT_TILES_PER_STEP + part)
        m = _dot(oh_ref[rows, :], w_h) + _dot(og_ref[rows, :], w_g) + _dot(od_ref[rows, :], w_d)
        o_ref[rows, :] = x_ref[rows, :] + mod_ref[cond, 5:6, :] * m


def _proj_out(x, mod_l, o_h, o_g, o_d, w_out, layer):
    rows = OUT_TILES_PER_STEP * TM
    return pl.pallas_call(
        _proj_out_kernel,
        out_shape=jax.ShapeDtypeStruct((N_ROWS, D_MODEL), F32),
        grid=(N_ROWS // rows, D_MODEL // OUT_TN),
        in_specs=[
            pl.BlockSpec((rows, OUT_TN), lambda i, j: (i, j)),
            pl.BlockSpec((N_COND, N_MOD, OUT_TN), lambda i, j: (0, 0, j)),
            pl.BlockSpec((rows, HGRN_W), lambda i, j: (i, 0)),
            pl.BlockSpec((rows, GQA_W), lambda i, j: (i, 0)),
            pl.BlockSpec((rows, DIFF_W), lambda i, j: (i, 0)),
            pl.BlockSpec((None, D_MODEL, OUT_TN), lambda i, j: (layer, 0, j)),
        ],
        out_specs=pl.BlockSpec((rows, OUT_TN), lambda i, j: (i, j)),
        compiler_params=_params("parallel", "parallel"),
        name=f"proj_out_l{layer}",
    )(x, mod_l, o_h, o_g, o_d, w_out)


def _log2_forget_and_key(z, lb):
    z2 = z * LOG2E
    soft = jnp.log2(1.0 + jnp.exp2(-jnp.abs(z2)))
    log_1mlb = jnp.log1p(-lb) * LOG2E
    a = jnp.log2(lb)
    c = log_1mlb + (jnp.minimum(z2, 0.0) - soft)
    log_f = jnp.maximum(a, c) + jnp.log2(1.0 + jnp.exp2(-jnp.abs(a - c)))
    log_k = log_1mlb + (jnp.minimum(-z2, 0.0) - soft)
    return log_f, log_k


def _cumsum_rows(x, reverse):
    tile = HGRN_SUB
    row = lax.broadcasted_iota(jnp.int32, (tile, 1), 0)
    tiles = []
    for j in range(x.shape[0] // tile):
        y = x[j * tile:(j + 1) * tile, :]
        for sh in (1, 2, 4):
            if reverse:
                y = y + jnp.where(row < tile - sh, pltpu.roll(y, tile - sh, 0), 0.0)
            else:
                y = y + jnp.where(row >= sh, pltpu.roll(y, sh, 0), 0.0)
        tiles.append(y)
    order = range(len(tiles) - 1, -1, -1) if reverse else range(len(tiles))
    carry = None
    for j in order:
        if carry is not None:
            tiles[j] = tiles[j] + carry
        carry = tiles[j][0:1, :] if reverse else tiles[j][tile - 1:tile, :]
    return jnp.concatenate(tiles, axis=0)


def _hgrn_prepare(q, z, lb, reverse):
    lf2, lk2 = _log2_forget_and_key(z, lb)
    b = _cumsum_rows(lf2, reverse)
    return _silu(q), b, b - lk2


def _hgrn_chunk(qs, b, c, v, st, reverse):
    C, SUB = HGRN_CHUNK, HGRN_SUB
    n_sub = C // SUB
    b_tot = b[0:1, :] if reverse else b[C - 1:C, :]

    o = _dot_nt((qs * jnp.exp2(b)).astype(BF16), st.astype(BF16))

    terms = []
    for i in range(n_sub):
        lo = i * SUB
        qi, bi, ci = qs[lo:lo + SUB, :], b[lo:lo + SUB, :], c[lo:lo + SUB, :]
        for s in range(SUB):
            terms.append(qi * jnp.exp2(bi - ci[s:s + 1, :]))
    k_sums = _dot(jnp.concatenate(terms, axis=0).astype(BF16), jnp.ones((HEAD_DIM, C), BF16))

    row = lax.broadcasted_iota(jnp.int32, (SUB, C), 0)
    lane = lax.broadcasted_iota(jnp.int32, (SUB, C), 1)
    lane_s = lane % SUB
    causal = (lane_s >= row) if reverse else (lane_s <= row)
    blocks = []
    for i in range(n_sub):
        lo, hi = i * SUB, (i + 1) * SUB
        diag = k_sums[lo * SUB:(lo + 1) * SUB, :]
        for s in range(1, SUB):
            diag = jnp.where(lane_s == s, k_sums[(lo + s) * SUB:(lo + s + 1) * SUB, :], diag)
        has_off = (i < n_sub - 1) if reverse else (i > 0)
        if has_off:
            ref = b[hi:hi + 1, :] if reverse else b[lo - 1:lo, :]
            qt = (qs[lo:hi, :] * jnp.exp2(b[lo:hi, :] - ref)).astype(BF16)
            if reverse:
                kt = jnp.concatenate([jnp.zeros((hi, HEAD_DIM), F32), jnp.exp2(ref - c[hi:, :])], axis=0)
            else:
                kt = jnp.concatenate([jnp.exp2(ref - c[:lo, :]), jnp.zeros((C - lo, HEAD_DIM), F32)], axis=0)
            off = _dot_nt(qt, kt.astype(BF16))
        else:
            off = jnp.zeros((SUB, C), F32)
        blocks.append(jnp.where((lane // SUB == i) & causal, diag, off))
    a = jnp.concatenate(blocks, axis=0)
    o = o + _dot(a.astype(BF16), v.astype(BF16))

    st_new = jnp.exp2(b_tot) * st + _dot_tn(v.astype(BF16), jnp.exp2(b_tot - c).astype(BF16))
    return o, st_new


def _hgrn_kernel(*refs, layer, seq, latent, n_alias):
    raw_ref, q_ref, v_ref, g_ref, ff_ref, fb_ref, ong_ref = refs[:7]
    s0_ref = refs[7] if latent else None
    outs = refs[7 + (1 if latent else 0) + n_alias:]
    if latent:
        o_ref, of_ref, ob_ref, st_ref, pre_ref = outs
        s_ref = None
    else:
        o_ref, s_ref, of_ref, ob_ref, st_ref, pre_ref = outs
    C = HGRN_CHUNK
    n_chunks = seq // C

    def lower_bound(d, lanes):
        rows = [raw_ref[2 * l + d:2 * l + d + 1, lanes] for l in range(DEPTH)]
        m = functools.reduce(jnp.maximum, rows)
        e = [jnp.exp(r - m) for r in rows]
        tot = functools.reduce(lambda x, y: x + y, e)
        lb = jnp.zeros_like(m)
        for l in range(1, layer + 1):
            lb = lb + e[l] / tot
        return lb

    head_lanes = [slice(hp * HEAD_DIM, (hp + 1) * HEAD_DIM) for hp in range(HGRN_HP)]
    lbs = [[lower_bound(d, lanes) for d in range(2)] for lanes in head_lanes]

    for hp in range(HGRN_HP):
        for d in range(2):
            st_ref[hp, d] = s0_ref[d, hp].T if latent else jnp.zeros((HEAD_DIM, HEAD_DIM), F32)

    def chunk_rows(ci):
        if isinstance(ci, int):
            return pl.ds(ci * C, C), pl.ds((n_chunks - 1 - ci) * C, C)
        return (pl.ds(pl.multiple_of(ci * C, C), C),
                pl.ds(pl.multiple_of((n_chunks - 1 - ci) * C, C), C))

    def prepare(ci, slot):
        rows = chunk_rows(ci)
        for hp, lanes in enumerate(head_lanes):
            for d, f_ref in enumerate((ff_ref, fb_ref)):
                pre = _hgrn_prepare(q_ref[rows[d], lanes], f_ref[rows[d], lanes], lbs[hp][d], d == 1)
                for n, val in enumerate(pre):
                    pre_ref[slot, 2 * hp + d, n] = val

    prepare(0, 0)

    def consume(ci, slot):
        rows = chunk_rows(ci)
        for hp, lanes in enumerate(head_lanes):
            for d, acc_ref in enumerate((of_ref, ob_ref)):
                qs, b, c = (pre_ref[slot, 2 * hp + d, n] for n in range(3))
                o, st = _hgrn_chunk(qs, b, c, v_ref[rows[d], lanes], st_ref[hp, d], d == 1)
                acc_ref[rows[d], lanes] = o
                st_ref[hp, d] = st

    def body(ci, carry):
        slot = ci % 2
        consume(ci, slot)
        prepare(ci + 1, 1 - slot)
        return carry

    lax.fori_loop(0, n_chunks - 1, body, 0)
    consume(n_chunks - 1, (n_chunks - 1) % 2)

    for hp, lanes in enumerate(head_lanes):
        if s_ref is not None:
            for d in range(2):
                s_ref[d, hp] = st_ref[hp, d].T
        o = of_ref[:, lanes] + ob_ref[:, lanes]
        o_ref[:, lanes] = (_rms(o, ong_ref[...]) * _silu(g_ref[:, lanes])).astype(o_ref.dtype)


def _hgrn(proj, lb_raw, onorm_g_l, layer, *, state0=None, o_prev=None, s_prev=None):
    latent = state0 is not None
    n_batch, seq, row0 = (DEC_BATCH, DEC_SEQ, N_CTX // DEC_SEQ) if latent else (BATCH, SEQ, 0)
    width = HGRN_HP * HEAD_DIM

    def col(block):
        return pl.BlockSpec((seq, width), lambda b, h: (row0 + b, block // HGRN_HP + h))

    in_specs = [
        pl.BlockSpec((2 * DEPTH, width), lambda b, h: (0, h)),
        col(COL_HQ), col(COL_HI), col(COL_HG), col(COL_HFF), col(COL_HFB),
        pl.BlockSpec((1, HEAD_DIM), lambda b, h: (0, 0)),
    ]
    args = [lb_raw.reshape(2 * DEPTH, HGRN_W), proj, proj, proj, proj, proj,
            onorm_g_l.reshape(1, HEAD_DIM)]
    o_shape = jax.ShapeDtypeStruct((N_ROWS, HGRN_W), BF16)
    o_spec = pl.BlockSpec((seq, width), lambda b, h: (row0 + b, h))
    aliases = {}
    if latent:
        in_specs += [pl.BlockSpec((None, None, 2, HGRN_HP, HEAD_DIM, HEAD_DIM),
                                  lambda b, h: (b, layer, 0, h, 0, 0)), _ANY]
        args += [state0, o_prev]
        aliases = {len(args) - 1: 0}
        out_shape, out_specs = o_shape, o_spec
    else:
        if s_prev is not None:
            in_specs.append(_ANY)
            args.append(s_prev)
            aliases = {len(args) - 1: 1}
        out_shape = (o_shape, jax.ShapeDtypeStruct(
            (BATCH, DEPTH, 2, HGRN_HEADS, HEAD_DIM, HEAD_DIM), F32))
        out_specs = (o_spec, pl.BlockSpec((None, None, 2, HGRN_HP, HEAD_DIM, HEAD_DIM),
                                          lambda b, h: (b, layer, 0, h, 0, 0)))
    return pl.pallas_call(
        functools.partial(_hgrn_kernel, layer=layer, seq=seq, latent=latent, n_alias=len(aliases)),
        out_shape=out_shape,
        grid=(n_batch, HGRN_HEADS // HGRN_HP),
        in_specs=in_specs,
        out_specs=out_specs,
        scratch_shapes=[pltpu.VMEM((seq, width), F32), pltpu.VMEM((seq, width), F32),
                        pltpu.VMEM((HGRN_HP, 2, HEAD_DIM, HEAD_DIM), F32),
                        pltpu.VMEM((2, 2 * HGRN_HP, 3, HGRN_CHUNK, HEAD_DIM), F32)],
        input_output_aliases=aliases,
        compiler_params=_params("parallel", "parallel"),
        name=f"hgrn_l{layer}_{'lat' if latent else 'ctx'}",
    )(*args)


def _swap_pairs(x, width):
    lanes = x.shape[-1]
    lane = lax.broadcasted_iota(jnp.int32, x.shape, x.ndim - 1)
    from_right = pltpu.roll(x, lanes - width, x.ndim - 1)
    from_left = pltpu.roll(x, width, x.ndim - 1)
    return jnp.where(lane % (2 * width) < width, from_right, from_left)


def _rope(x, cos, sin_signed, quarter):
    return x * cos + _swap_pairs(x, quarter) * sin_signed


def _rms_halves(x, g):
    half = x.shape[-1] // 2
    lane = lax.broadcasted_iota(jnp.int32, x.shape, x.ndim - 1)
    lo = lane < half
    sq = x * x
    ms_lo = jnp.sum(jnp.where(lo, sq, 0.0), axis=-1, keepdims=True) / half
    ms_hi = jnp.sum(jnp.where(lo, 0.0, sq), axis=-1, keepdims=True) / half
    ms = jnp.where(lo, ms_lo, ms_hi)
    return (x * lax.rsqrt(ms + EPS)) * g


def _gqa_kernel(*refs, seq, latent, n_alias):
    qg_ref, kg_ref = refs[:2]
    q_refs = refs[2:2 + GQA_GROUP]
    k_ref, v_ref = refs[2 + GQA_GROUP:4 + GQA_GROUP]
    rest = refs[4 + GQA_GROUP:]
    if latent:
        ck_ref, cv_ref, cos_ref, sin_ref = rest[:4]
        o_ref, kt_ref, vt_ref = rest[4 + n_alias:]
    else:
        o_ref, kn_ref, vo_ref, kt_ref, vt_ref = rest[n_alias:]
    past = PAST_LEN if latent else 0
    quarter = HEAD_DIM // 4

    kn = _rms(k_ref[...], kg_ref[...])
    v = v_ref[...]
    if latent:
        kn = _rope(kn, cos_ref[...], sin_ref[...], quarter)
        kt_ref[0:past, :] = ck_ref[...].astype(BF16)
        vt_ref[0:past, 0:HEAD_DIM] = cv_ref[...].astype(BF16)
    else:
        kn_ref[...] = kn
        vo_ref[...] = v
    kt_ref[past:past + seq, :] = kn.astype(BF16)
    vt_ref[past:past + seq, 0:HEAD_DIM] = v.astype(BF16)
    vt_ref[:, HEAD_DIM:] = jnp.ones((past + seq, HEAD_DIM), BF16)

    q_scale = HEAD_DIM ** -0.5 * LOG2E

    def attend(q_rows):
        s = _dot_nt(jnp.concatenate(q_rows, axis=0), kt_ref[...])
        p = jnp.exp2(s - jnp.max(s, axis=-1, keepdims=True))
        ov = _dot(p.astype(BF16), vt_ref[...])
        o = (ov[:, :HEAD_DIM] / ov[:, HEAD_DIM:]).astype(o_ref.dtype)
        n = q_rows[0].shape[0]
        return [o[i * n:(i + 1) * n, :] for i in range(len(q_rows))]

    def query(q_ref, rows):
        qn = _rms(q_ref[rows, :], qg_ref[...])
        if latent:
            qn = _rope(qn, cos_ref[rows, :], sin_ref[rows, :], quarter)
        return (qn * q_scale).astype(BF16)

    head_lanes = [slice(g * HEAD_DIM, (g + 1) * HEAD_DIM) for g in range(GQA_GROUP)]
    if seq <= GQA_Q_BLOCK:
        rows = slice(0, seq)
        outs = attend([query(q_ref, rows) for q_ref in q_refs])
        for lanes, o in zip(head_lanes, outs):
            o_ref[rows, lanes] = o
    else:
        for lanes, q_ref in zip(head_lanes, q_refs):
            for r in range(seq // GQA_Q_BLOCK):
                rows = slice(r * GQA_Q_BLOCK, (r + 1) * GQA_Q_BLOCK)
                o_ref[rows, lanes] = attend([query(q_ref, rows)])[0]


def _gqa(proj, qnorm_g, knorm_g, layer, *, latent_args=None, kv_prev=None):
    latent = latent_args is not None
    n_batch, seq, row0 = (DEC_BATCH, DEC_SEQ, N_CTX // DEC_SEQ) if latent else (BATCH, SEQ, 0)
    past = PAST_LEN if latent else 0

    def vec():
        return pl.BlockSpec((1, HEAD_DIM), lambda b, h: (0, 0))

    def q_spec(g):
        return pl.BlockSpec((seq, HEAD_DIM), lambda b, h: (row0 + b, COL_GQ + h * GQA_GROUP + g))

    in_specs = [vec(), vec()] + [q_spec(g) for g in range(GQA_GROUP)] + [
        pl.BlockSpec((seq, HEAD_DIM), lambda b, h: (row0 + b, COL_GK + h)),
        pl.BlockSpec((seq, HEAD_DIM), lambda b, h: (row0 + b, COL_GV + h)),
    ]
    args = ([qnorm_g.reshape(1, HEAD_DIM), knorm_g.reshape(1, HEAD_DIM)]
            + [proj] * (GQA_GROUP + 2))
    o_spec = pl.BlockSpec((seq, GQA_GROUP * HEAD_DIM), lambda b, h: (row0 + b, h))
    o_shape = jax.ShapeDtypeStruct((N_ROWS, GQA_W), BF16)
    cache_spec = pl.BlockSpec((None, None, SEQ, HEAD_DIM), lambda b, h: (b, layer, 0, h))
    aliases = {}
    if latent:
        cache_k, cache_v, cos, sin_signed, o_prev = latent_args
        table_spec = pl.BlockSpec((seq, HEAD_DIM), lambda b, h: (0, 0))
        in_specs += [cache_spec, cache_spec, table_spec, table_spec, _ANY]
        args += [cache_k.reshape(DEC_BATCH, DEPTH, PAST_LEN, GQA_KV_W),
                 cache_v.reshape(DEC_BATCH, DEPTH, PAST_LEN, GQA_KV_W), cos, sin_signed, o_prev]
        aliases = {len(args) - 1: 0}
        out_shape, out_specs = o_shape, o_spec
    else:
        if kv_prev is not None:
            in_specs += [_ANY, _ANY]
            args += list(kv_prev)
            aliases = {len(args) - 2: 1, len(args) - 1: 2}
        new_shape = jax.ShapeDtypeStruct((BATCH, DEPTH, SEQ, GQA_KV_W), F32)
        out_shape = (o_shape, new_shape, new_shape)
        out_specs = (o_spec, cache_spec, cache_spec)
    return pl.pallas_call(
        functools.partial(_gqa_kernel, seq=seq, latent=latent, n_alias=len(aliases)),
        out_shape=out_shape,
        grid=(n_batch, GQA_KV_HEADS),
        in_specs=in_specs,
        out_specs=out_specs,
        scratch_shapes=[pltpu.VMEM((past + seq, HEAD_DIM), BF16),
                        pltpu.VMEM((past + seq, 2 * HEAD_DIM), BF16)],
        input_output_aliases=aliases,
        compiler_params=_params("parallel", "parallel"),
        name=f"gqa_l{layer}_{'lat' if latent else 'ctx'}",
    )(*args)


def _diff_kernel(*refs, seq, latent, lam_init, n_alias, heads):
    qg_ref, kg_ref, sg_ref, lam_ref, q_ref, k_ref, v_ref = refs[:7]
    if latent:
        ck_ref, cv_ref, cos_ref, sin_ref = refs[7:11]
        o_ref, kt_ref, vt_ref = refs[11 + n_alias:]
    else:
        o_ref, kn_ref, vo_ref, kt_ref, vt_ref = refs[7 + n_alias:]
    past = PAST_LEN if latent else 0
    quarter = DIFF_QK_DIM // 4
    head_lanes = [slice(hd * HEAD_DIM, (hd + 1) * HEAD_DIM) for hd in range(heads)]

    for hd, lanes in enumerate(head_lanes):
        kn = _rms_halves(k_ref[:, lanes], kg_ref[...])
        v = v_ref[:, lanes]
        if latent:
            kn = _rope(kn, cos_ref[...], sin_ref[...], quarter)
            kt_ref[0:past, lanes] = ck_ref[:, lanes].astype(BF16)
            vt_ref[hd, 0:past, 0:HEAD_DIM] = cv_ref[:, lanes].astype(BF16)
        else:
            kn_ref[:, lanes] = kn
            vo_ref[:, lanes] = v
        kt_ref[past:past + seq, lanes] = kn.astype(BF16)
        vt_ref[hd, past:past + seq, 0:HEAD_DIM] = v.astype(BF16)
        vt_ref[hd, :, HEAD_DIM:] = jnp.ones((past + seq, HEAD_DIM), BF16)

    lv = lam_ref[...]
    lam = (jnp.exp(jnp.sum(lv[0:1, :] * lv[1:2, :], axis=-1, keepdims=True))
           - jnp.exp(jnp.sum(lv[2:3, :] * lv[3:4, :], axis=-1, keepdims=True)) + lam_init)
    q_scale = DIFF_QK_DIM ** -0.5 * LOG2E

    def attend(q_masked, kt, vt):
        s = _dot_nt(q_masked.astype(BF16), kt)
        p = jnp.exp2(s - jnp.max(s, axis=-1, keepdims=True))
        ov = _dot(p.astype(BF16), vt)
        return ov[:, :HEAD_DIM] / ov[:, HEAD_DIM:]

    q_block = min(seq, Q_BLOCK)
    for hd, lanes in enumerate(head_lanes):
        for r in range(seq // q_block):
            rows = slice(r * q_block, (r + 1) * q_block)
            qn = _rms_halves(q_ref[rows, lanes], qg_ref[...])
            if latent:
                qn = _rope(qn, cos_ref[rows, :], sin_ref[rows, :], quarter)
            qn = qn * q_scale
            lo = lax.broadcasted_iota(jnp.int32, qn.shape, 1) < DIFF_QK_DIM
            kt, vt = kt_ref[:, lanes], vt_ref[hd]
            o = attend(jnp.where(lo, qn, 0.0), kt, vt) - lam * attend(jnp.where(lo, 0.0, qn), kt, vt)
            o = _rms(o, sg_ref[...]) * (1.0 - lam_init)
            o_ref[rows, lanes] = o.astype(o_ref.dtype)


def _diff(proj, qnorm_g, knorm_g, subln_g, lam_params, layer, *, latent_args=None, kv_prev=None):
    latent = latent_args is not None
    n_batch, seq, row0 = (DEC_BATCH, DEC_SEQ, N_CTX // DEC_SEQ) if latent else (BATCH, SEQ, 0)
    past = PAST_LEN if latent else 0
    lam_init = 0.8 - 0.6 * math.exp(-0.3 * layer)
    heads = DIFF_HP_LAT if latent else DIFF_HP_CTX
    width = heads * HEAD_DIM

    def vec():
        return pl.BlockSpec((1, HEAD_DIM), lambda b, h: (0, 0))

    def col(block):
        return pl.BlockSpec((seq, width), lambda b, h: (row0 + b, block // heads + h))

    in_specs = [
        vec(), vec(), vec(),
        pl.BlockSpec((4, DIFF_QK_DIM), lambda b, h: (0, 0)),
        col(COL_DQ), col(COL_DK), col(COL_DV),
    ]
    args = [jnp.tile(qnorm_g, 2).reshape(1, HEAD_DIM), jnp.tile(knorm_g, 2).reshape(1, HEAD_DIM),
            subln_g.reshape(1, HEAD_DIM), lam_params, proj, proj, proj]
    o_spec = pl.BlockSpec((seq, width), lambda b, h: (row0 + b, h))
    o_shape = jax.ShapeDtypeStruct((N_ROWS, DIFF_W), BF16)
    cache_spec = pl.BlockSpec((None, None, SEQ, width), lambda b, h: (b, layer, 0, h))
    aliases = {}
    if latent:
        cache_k, cache_v, cos, sin_signed, o_prev = latent_args
        table_spec = pl.BlockSpec((seq, HEAD_DIM), lambda b, h: (0, 0))
        in_specs += [cache_spec, cache_spec, table_spec, table_spec, _ANY]
        args += [cache_k.reshape(DEC_BATCH, DEPTH, PAST_LEN, DIFF_W),
                 cache_v.reshape(DEC_BATCH, DEPTH, PAST_LEN, DIFF_W), cos, sin_signed, o_prev]
        aliases = {len(args) - 1: 0}
        out_shape, out_specs = o_shape, o_spec
    else:
        if kv_prev is not None:
            in_specs += [_ANY, _ANY]
            args += list(kv_prev)
            aliases = {len(args) - 2: 1, len(args) - 1: 2}
        new_shape = jax.ShapeDtypeStruct((BATCH, DEPTH, SEQ, DIFF_W), F32)
        out_shape = (o_shape, new_shape, new_shape)
        out_specs = (o_spec, cache_spec, cache_spec)
    return pl.pallas_call(
        functools.partial(_diff_kernel, seq=seq, latent=latent, lam_init=lam_init,
                          n_alias=len(aliases), heads=heads),
        out_shape=out_shape,
        grid=(n_batch, DIFF_HEADS // heads),
        in_specs=in_specs,
        out_specs=out_specs,
        scratch_shapes=[pltpu.VMEM((past + seq, width), BF16),
                        pltpu.VMEM((heads, past + seq, 2 * HEAD_DIM), BF16)],
        input_output_aliases=aliases,
        compiler_params=_params("parallel", "parallel"),
        name=f"diff_l{layer}_{'lat' if latent else 'ctx'}",
    )(*args)


def _rope_tables(n_tokens, dim, repeat):
    quarter = dim // 4
    t = jnp.arange(n_tokens)
    pos = jnp.stack([t // GRID_W, t % GRID_W], axis=-1).astype(F32)
    inv = ROPE_BASE ** (-jnp.arange(quarter, dtype=F32) / quarter)
    ang = pos[:, :, None] * inv
    cos, sin = jnp.cos(ang), jnp.sin(ang)
    cos_l = jnp.concatenate([cos, cos], axis=-1).reshape(n_tokens, dim)
    sin_l = jnp.concatenate([-sin, sin], axis=-1).reshape(n_tokens, dim)
    return jnp.tile(cos_l, (1, repeat)), jnp.tile(sin_l, (1, repeat))


def kernel(x_prompt, x_sample, c, cache_gqa_k, cache_gqa_v, cache_diff_k, cache_diff_v, state_hgrn,
           c_ctx, w_mod, b_mod, norm_g, ffn_w_gate, ffn_w_up, ffn_w_down, w_in, w_out, hgrn_lb_raw,
           hgrn_onorm_g, gqa_qnorm_g, gqa_knorm_g, diff_qnorm_g, diff_knorm_g, diff_lambda,
           diff_subln_g):
    cond = jnp.concatenate(
        [c_ctx[None, :], c, jnp.zeros((COND_PAD - N_COND, D_MODEL), F32)], axis=0)
    mod = _modulation(cond, w_mod, b_mod)

    cos_g, sin_g = _rope_tables(DEC_SEQ, HEAD_DIM, 1)
    cos_d, sin_d = _rope_tables(DEC_SEQ, DIFF_QK_DIM, 2)
    cache_dk = cache_diff_k.reshape(DEC_BATCH, DEPTH, PAST_LEN, DIFF_HEADS, 2 * DIFF_QK_DIM)
    ctx_rows = dict(n_tiles=CTX_TILES, tile0=0)
    lat_rows = dict(n_tiles=LAT_TILES, tile0=CTX_TILES)

    x = None
    states = gqa_kv = diff_kv = None
    for l in range(DEPTH):
        ffn_w = (mod[l], norm_g[l], ffn_w_gate, ffn_w_up, ffn_w_down, l)
        if l == 0:
            x = _ffn(x_prompt.reshape(N_CTX, D_MODEL), *ffn_w, 0, **ctx_rows)
            x = _ffn(x_sample.reshape(N_LAT, D_MODEL), *ffn_w, 0, **lat_rows,
                     out_off=CTX_TILES, o_prev=x)
        else:
            x = _ffn(x, *ffn_w, 0)
        proj = _proj_in(x, mod[l], norm_g[l], w_in, l)

        o_h, states = _hgrn(proj, hgrn_lb_raw, hgrn_onorm_g[l], l, s_prev=states)
        o_h = _hgrn(proj, hgrn_lb_raw, hgrn_onorm_g[l], l, state0=state_hgrn, o_prev=o_h)

        gqa_w = (proj, gqa_qnorm_g[l], gqa_knorm_g[l], l)
        o_g, *gqa_kv = _gqa(*gqa_w, kv_prev=gqa_kv)
        o_g = _gqa(*gqa_w, latent_args=(cache_gqa_k, cache_gqa_v, cos_g, sin_g, o_g))

        diff_w = (proj, diff_qnorm_g[l], diff_knorm_g[l], diff_subln_g[l], diff_lambda[l], l)
        o_d, *diff_kv = _diff(*diff_w, kv_prev=diff_kv)
        o_d = _diff(*diff_w, latent_args=(cache_dk, cache_diff_v, cos_d, sin_d, o_d))

        x = _proj_out(x, mod[l], o_h, o_g, o_d, w_out, l)
        if l < DEPTH - 1:
            x = _ffn(x, *ffn_w, 1)
        else:
            y_prompt = _ffn(x, *ffn_w, 1, **ctx_rows, out_rows=N_CTX)
            y_sample = _ffn(x, *ffn_w, 1, **lat_rows, in_off=CTX_TILES, out_rows=N_LAT)

    return (y_prompt.reshape(BATCH, SEQ, D_MODEL), y_sample.reshape(DEC_BATCH, DEC_SEQ, D_MODEL),
            gqa_kv[0].reshape(BATCH, DEPTH, SEQ, GQA_KV_HEADS, HEAD_DIM),
            gqa_kv[1].reshape(BATCH, DEPTH, SEQ, GQA_KV_HEADS, HEAD_DIM),
            diff_kv[0].reshape(BATCH, DEPTH, SEQ, DIFF_HEADS, 2, DIFF_QK_DIM),
            diff_kv[1].reshape(BATCH, DEPTH, SEQ, DIFF_HEADS, HEAD_DIM),
            states)
```

```python
import functools
import math

import jax
import jax.numpy as jnp
from jax import lax
from jax.experimental import pallas as pl
from jax.experimental.pallas import tpu as pltpu

F32 = jnp.float32
BF16 = jnp.bfloat16

D_MODEL = 2048
BATCH = 16
SEQ = 256
DEPTH = 2
DEC_BATCH = 2
DEC_SEQ = 1024
PAST_LEN = 256
GRID_W = 64
HEAD_DIM = 128
HGRN_HEADS = 4
GQA_Q_HEADS = 6
GQA_KV_HEADS = 2
GQA_GROUP = GQA_Q_HEADS // GQA_KV_HEADS
DIFF_HEADS = 6
DIFF_QK_DIM = 64
FFN_DIM = 5632
N_MOD = 9
IN_WIDTH = 6144
ROPE_BASE = 10000.0
EPS = 1e-6
LOG2E = math.log2(math.e)

N_CTX = BATCH * SEQ
N_LAT = DEC_BATCH * DEC_SEQ
N_ROWS = N_CTX + N_LAT
N_COND = 1 + DEC_BATCH
COND_PAD = 8

COL_HQ, COL_HI, COL_HG, COL_HFF, COL_HFB = 0, 4, 8, 12, 16
COL_GQ, COL_GK, COL_GV = 20, 26, 28
COL_DQ, COL_DK, COL_DV = 30, 36, 42

HGRN_W = HGRN_HEADS * HEAD_DIM
GQA_W = GQA_Q_HEADS * HEAD_DIM
GQA_KV_W = GQA_KV_HEADS * HEAD_DIM
DIFF_W = DIFF_HEADS * HEAD_DIM

VMEM_LIMIT = 60 * 1024 * 1024

TM = 1024
ROW_CHUNK = 128
FFN_TF = 256
FFN_TILES_PER_STEP = 2
FFN_NB = 512
IN_TN = 512
IN_TILES_PER_STEP = 2
OUT_TN = 512
OUT_TILES_PER_STEP = 2
MOD_TN = 2048
HGRN_CHUNK = 64
HGRN_SUB = 8
HGRN_HP = 4
HGRN_UNROLL_MAX_CHUNKS = 4
Q_BLOCK = 256
GQA_Q_BLOCK = 256
GQA_STACK_MAX_SEQ = 256
DIFF_HP_CTX = 6
DIFF_HP_LAT = 2

CTX_TILES = N_CTX // TM
LAT_TILES = N_LAT // TM
ALL_TILES = CTX_TILES + LAT_TILES


def _cond_of_tile(i):
    tiles_per_latent = DEC_SEQ // TM
    return jnp.where(i < CTX_TILES, 0, 1 + (i - CTX_TILES) // tiles_per_latent)


def _silu(x):
    return x * jax.nn.sigmoid(x)


def _dot(a, b):
    return jnp.dot(a, b, preferred_element_type=F32)


def _dot_nt(a, b):
    return lax.dot_general(a, b, (((1,), (1,)), ((), ())), preferred_element_type=F32)


def _dot_tn(a, b):
    return lax.dot_general(a, b, (((0,), (0,)), ((), ())), preferred_element_type=F32)


def _rms(x, g):
    ms = jnp.mean(x * x, axis=-1, keepdims=True)
    return (x * lax.rsqrt(ms + EPS)) * g


def _params(*semantics):
    return pltpu.CompilerParams(dimension_semantics=semantics, vmem_limit_bytes=VMEM_LIMIT)


_ANY = pl.BlockSpec(memory_space=pl.ANY)


def _mod_kernel(cond_ref, w_ref, b_ref, o_ref):
    a = _silu(cond_ref[...]).astype(BF16)
    o_ref[...] = _dot(a, w_ref[...].astype(BF16)) + b_ref[...]


def _modulation(cond, w_mod, b_mod):
    width = N_MOD * D_MODEL
    out = pl.pallas_call(
        _mod_kernel,
        out_shape=jax.ShapeDtypeStruct((DEPTH, COND_PAD, width), F32),
        grid=(DEPTH, width // MOD_TN),
        in_specs=[
            pl.BlockSpec((COND_PAD, D_MODEL), lambda l, j: (0, 0)),
            pl.BlockSpec((None, D_MODEL, MOD_TN), lambda l, j: (l, 0, j)),
            pl.BlockSpec((None, 1, MOD_TN), lambda l, j: (l, 0, j)),
        ],
        out_specs=pl.BlockSpec((None, COND_PAD, MOD_TN), lambda l, j: (l, 0, j)),
        compiler_params=_params("parallel", "parallel"),
        name="modulation",
    )(cond, w_mod, b_mod.reshape(DEPTH, 1, width))
    return out[:, :N_COND].reshape(DEPTH, N_COND, N_MOD, D_MODEL)


def _mod_norm_into(x_ref, mod_ref, g_ref, h_ref, sub, row0=0):
    shift = mod_ref[3 * sub:3 * sub + 1, :]
    gain = g_ref[sub:sub + 1, :] * (1.0 + mod_ref[3 * sub + 1:3 * sub + 2, :])

    def body(r, carry):
        rows = pl.ds(pl.multiple_of(row0 + r * ROW_CHUNK, ROW_CHUNK), ROW_CHUNK)
        h_ref[rows, :] = (_rms(x_ref[rows, :], gain) + shift).astype(BF16)
        return carry

    lax.fori_loop(0, TM // ROW_CHUNK, body, 0)


def _ffn_kernel(x_ref, mod_ref, g_ref, wg_ref, wu_ref, wd_ref, *rest, sub, tile0):
    o_ref, h_ref = rest[-2:]
    j = pl.program_id(1)
    conds = [_cond_of_tile(tile0 + pl.program_id(0) * FFN_TILES_PER_STEP + part)
             for part in range(FFN_TILES_PER_STEP)]

    @pl.when(j == 0)
    def _():
        for part, cond in enumerate(conds):
            _mod_norm_into(x_ref, mod_ref.at[cond], g_ref, h_ref, sub, row0=part * TM)
        o_ref[...] = jnp.zeros_like(o_ref)

    for part in range(FFN_TILES_PER_STEP):
        rows = slice(part * TM, (part + 1) * TM)
        h = h_ref[rows, :]
        gate_act = _dot(h, wg_ref[...].astype(BF16))
        up = _dot(h, wu_ref[...].astype(BF16))
        a = (_silu(gate_act) * up).astype(BF16)
        for n in range(0, D_MODEL, FFN_NB):
            o_ref[rows, n:n + FFN_NB] += _dot(a, wd_ref[:, n:n + FFN_NB].astype(BF16))

    @pl.when(j == pl.num_programs(1) - 1)
    def _():
        for part, cond in enumerate(conds):
            gate = mod_ref[cond, 3 * sub + 2:3 * sub + 3, :]

            def body(r, carry):
                rows = pl.ds(pl.multiple_of(part * TM + r * ROW_CHUNK, ROW_CHUNK), ROW_CHUNK)
                o_ref[rows, :] = x_ref[rows, :] + gate * (0.5 * o_ref[rows, :])
                return carry

            lax.fori_loop(0, TM // ROW_CHUNK, body, 0)


def _ffn(x, mod_l, norm_g_l, w_gate, w_up, w_down, layer, which, *, n_tiles=ALL_TILES, in_off=0,
         out_off=0, tile0=0, out_rows=N_ROWS, o_prev=None):
    sub = 2 * which
    per = FFN_TILES_PER_STEP
    rows = per * TM
    assert n_tiles % per == 0 and in_off % per == 0 and out_off % per == 0
    in_specs = [
        pl.BlockSpec((rows, D_MODEL), lambda i, j: (in_off // per + i, 0), pipeline_mode=pl.Buffered(1)),
        pl.BlockSpec((N_COND, N_MOD, D_MODEL), lambda i, j: (0, 0, 0)),
        pl.BlockSpec((3, D_MODEL), lambda i, j: (0, 0)),
        pl.BlockSpec((None, None, D_MODEL, FFN_TF), lambda i, j: (layer, which, 0, j)),
        pl.BlockSpec((None, None, D_MODEL, FFN_TF), lambda i, j: (layer, which, 0, j)),
        pl.BlockSpec((None, None, FFN_TF, D_MODEL), lambda i, j: (layer, which, j, 0)),
    ]
    args = [x, mod_l, norm_g_l, w_gate, w_up, w_down]
    aliases = {}
    if o_prev is not None:
        in_specs.append(_ANY)
        args.append(o_prev)
        aliases = {len(args) - 1: 0}
    return pl.pallas_call(
        functools.partial(_ffn_kernel, sub=sub, tile0=tile0),
        out_shape=jax.ShapeDtypeStruct((out_rows, D_MODEL), F32),
        grid=(n_tiles // per, FFN_DIM // FFN_TF),
        in_specs=in_specs,
        out_specs=pl.BlockSpec((rows, D_MODEL), lambda i, j: (out_off // per + i, 0),
                               pipeline_mode=pl.Buffered(1)),
        scratch_shapes=[pltpu.VMEM((rows, D_MODEL), BF16)],
        input_output_aliases=aliases,
        compiler_params=_params("parallel", "arbitrary"),
        name=f"ffn_l{layer}_h{which}_t{tile0}n{n_tiles}",
    )(*args)


def _proj_in_kernel(x_ref, mod_ref, g_ref, w_ref, o_ref, h_ref):
    @pl.when(pl.program_id(1) == 0)
    def _():
        for part in range(IN_TILES_PER_STEP):
            cond = _cond_of_tile(pl.program_id(0) * IN_TILES_PER_STEP + part)
            _mod_norm_into(x_ref, mod_ref.at[cond], g_ref, h_ref, 1, row0=part * TM)

    w = w_ref[...].astype(BF16)
    for part in range(IN_TILES_PER_STEP):
        rows = slice(part * TM, (part + 1) * TM)
        o_ref[rows, :] = _dot(h_ref[rows, :], w)


def _proj_in(x, mod_l, norm_g_l, w_in, layer):
    rows = IN_TILES_PER_STEP * TM
    return pl.pallas_call(
        _proj_in_kernel,
        out_shape=jax.ShapeDtypeStruct((N_ROWS, IN_WIDTH), F32),
        grid=(N_ROWS // rows, IN_WIDTH // IN_TN),
        in_specs=[
            pl.BlockSpec((rows, D_MODEL), lambda i, j: (i, 0)),
            pl.BlockSpec((N_COND, N_MOD, D_MODEL), lambda i, j: (0, 0, 0)),
            pl.BlockSpec((3, D_MODEL), lambda i, j: (0, 0)),
            pl.BlockSpec((None, D_MODEL, IN_TN), lambda i, j: (layer, 0, j)),
        ],
        out_specs=pl.BlockSpec((rows, IN_TN), lambda i, j: (i, j)),
        scratch_shapes=[pltpu.VMEM((rows, D_MODEL), BF16)],
        compiler_params=_params("parallel", "arbitrary"),
        name=f"proj_in_l{layer}",
    )(x, mod_l, norm_g_l, w_in)


def _proj_out_kernel(x_ref, mod_ref, oh_ref, og_ref, od_ref, w_ref, o_ref):
    w_h = w_ref[0:HGRN_W, :].astype(BF16)
    w_g = w_ref[HGRN_W:HGRN_W + GQA_W, :].astype(BF16)
    w_d = w_ref[HGRN_W + GQA_W:, :].astype(BF16)
    for part in range(OUT_TILES_PER_STEP):
        rows = slice(part * TM, (part + 1) * TM)
        cond = _cond_of_tile(pl.program_id(0) * OUT_TILES_PER_STEP + part)
        m = _dot(oh_ref[rows, :], w_h) + _dot(og_ref[rows, :], w_g) + _dot(od_ref[rows, :], w_d)
        o_ref[rows, :] = x_ref[rows, :] + mod_ref[cond, 5:6, :] * m


def _proj_out(x, mod_l, o_h, o_g, o_d, w_out, layer):
    rows = OUT_TILES_PER_STEP * TM
    return pl.pallas_call(
        _proj_out_kernel,
        out_shape=jax.ShapeDtypeStruct((N_ROWS, D_MODEL), F32),
        grid=(N_ROWS // rows, D_MODEL // OUT_TN),
        in_specs=[
            pl.BlockSpec((rows, OUT_TN), lambda i, j: (i, j)),
            pl.BlockSpec((N_COND, N_MOD, OUT_TN), lambda i, j: (0, 0, j)),
            pl.BlockSpec((rows, HGRN_W), lambda i, j: (i, 0)),
            pl.BlockSpec((rows, GQA_W), lambda i, j: (i, 0)),
            pl.BlockSpec((rows, DIFF_W), lambda i, j: (i, 0)),
            pl.BlockSpec((None, D_MODEL, OUT_TN), lambda i, j: (layer, 0, j)),
        ],
        out_specs=pl.BlockSpec((rows, OUT_TN), lambda i, j: (i, j)),
        compiler_params=_params("parallel", "parallel"),
        name=f"proj_out_l{layer}",
    )(x, mod_l, o_h, o_g, o_d, w_out)


def _log2_forget_and_key(z, lb):
    z2 = z * LOG2E
    soft = jnp.log2(1.0 + jnp.exp2(-jnp.abs(z2)))
    log_1mlb = jnp.log1p(-lb) * LOG2E
    a = jnp.log2(lb)
    c = log_1mlb + (jnp.minimum(z2, 0.0) - soft)
    log_f = jnp.maximum(a, c) + jnp.log2(1.0 + jnp.exp2(-jnp.abs(a - c)))
    log_k = log_1mlb + (jnp.minimum(-z2, 0.0) - soft)
    return log_f, log_k


def _cumsum_rows(x, reverse):
    tile = HGRN_SUB
    row = lax.broadcasted_iota(jnp.int32, (tile, 1), 0)
    tiles = []
    for j in range(x.shape[0] // tile):
        y = x[j * tile:(j + 1) * tile, :]
        for sh in (1, 2, 4):
            if reverse:
                y = y + jnp.where(row < tile - sh, pltpu.roll(y, tile - sh, 0), 0.0)
            else:
                y = y + jnp.where(row >= sh, pltpu.roll(y, sh, 0), 0.0)
        tiles.append(y)
    order = range(len(tiles) - 1, -1, -1) if reverse else range(len(tiles))
    carry = None
    for j in order:
        if carry is not None:
            tiles[j] = tiles[j] + carry
        carry = tiles[j][0:1, :] if reverse else tiles[j][tile - 1:tile, :]
    return jnp.concatenate(tiles, axis=0)


def _hgrn_prepare(q, z, lb, reverse):
    lf2, lk2 = _log2_forget_and_key(z, lb)
    b = _cumsum_rows(lf2, reverse)
    return _silu(q), b, b - lk2


def _hgrn_chunk(qs, b, c, v, st, reverse):
    C, SUB = HGRN_CHUNK, HGRN_SUB
    n_sub = C // SUB
    b_tot = b[0:1, :] if reverse else b[C - 1:C, :]

    o = _dot_nt((qs * jnp.exp2(b)).astype(BF16), st.astype(BF16))

    terms = []
    for i in range(n_sub):
        lo = i * SUB
        qi, bi, ci = qs[lo:lo + SUB, :], b[lo:lo + SUB, :], c[lo:lo + SUB, :]
        for s in range(SUB):
            terms.append(qi * jnp.exp2(bi - ci[s:s + 1, :]))
    k_sums = _dot(jnp.concatenate(terms, axis=0).astype(BF16), jnp.ones((HEAD_DIM, C), BF16))

    row = lax.broadcasted_iota(jnp.int32, (SUB, C), 0)
    lane = lax.broadcasted_iota(jnp.int32, (SUB, C), 1)
    lane_s = lane % SUB
    causal = (lane_s >= row) if reverse else (lane_s <= row)
    blocks = []
    for i in range(n_sub):
        lo, hi = i * SUB, (i + 1) * SUB
        diag = k_sums[lo * SUB:(lo + 1) * SUB, :]
        for s in range(1, SUB):
            diag = jnp.where(lane_s == s, k_sums[(lo + s) * SUB:(lo + s + 1) * SUB, :], diag)
        has_off = (i < n_sub - 1) if reverse else (i > 0)
        if has_off:
            ref = b[hi:hi + 1, :] if reverse else b[lo - 1:lo, :]
            qt = (qs[lo:hi, :] * jnp.exp2(b[lo:hi, :] - ref)).astype(BF16)
            if reverse:
                kt = jnp.concatenate([jnp.zeros((hi, HEAD_DIM), F32), jnp.exp2(ref - c[hi:, :])], axis=0)
            else:
                kt = jnp.concatenate([jnp.exp2(ref - c[:lo, :]), jnp.zeros((C - lo, HEAD_DIM), F32)], axis=0)
            off = _dot_nt(qt, kt.astype(BF16))
        else:
            off = jnp.zeros((SUB, C), F32)
        blocks.append(jnp.where((lane // SUB == i) & causal, diag, off))
    a = jnp.concatenate(blocks, axis=0)
    o = o + _dot(a.astype(BF16), v.astype(BF16))

    st_new = jnp.exp2(b_tot) * st + _dot_tn(v.astype(BF16), jnp.exp2(b_tot - c).astype(BF16))
    return o, st_new


def _hgrn_kernel(*refs, layer, seq, latent, n_alias):
    raw_ref, q_ref, v_ref, g_ref, ff_ref, fb_ref, ong_ref = refs[:7]
    s0_ref = refs[7] if latent else None
    outs = refs[7 + (1 if latent else 0) + n_alias:]
    if latent:
        o_ref, of_ref, ob_ref, st_ref, pre_ref = outs
        s_ref = None
    else:
        o_ref, s_ref, of_ref, ob_ref, st_ref, pre_ref = outs
    C = HGRN_CHUNK
    n_chunks = seq // C

    def lower_bound(d, lanes):
        rows = [raw_ref[2 * l + d:2 * l + d + 1, lanes] for l in range(DEPTH)]
        m = functools.reduce(jnp.maximum, rows)
        e = [jnp.exp(r - m) for r in rows]
        tot = functools.reduce(lambda x, y: x + y, e)
        lb = jnp.zeros_like(m)
        for l in range(1, layer + 1):
            lb = lb + e[l] / tot
        return lb

    head_lanes = [slice(hp * HEAD_DIM, (hp + 1) * HEAD_DIM) for hp in range(HGRN_HP)]
    lbs = [[lower_bound(d, lanes) for d in range(2)] for lanes in head_lanes]

    for hp in range(HGRN_HP):
        for d in range(2):
            st_ref[hp, d] = s0_ref[d, hp].T if latent else jnp.zeros((HEAD_DIM, HEAD_DIM), F32)

    def chunk_rows(ci):
        if isinstance(ci, int):
            return pl.ds(ci * C, C), pl.ds((n_chunks - 1 - ci) * C, C)
        return (pl.ds(pl.multiple_of(ci * C, C), C),
                pl.ds(pl.multiple_of((n_chunks - 1 - ci) * C, C), C))

    def prepare(ci, slot):
        rows = chunk_rows(ci)
        for hp, lanes in enumerate(head_lanes):
            for d, f_ref in enumerate((ff_ref, fb_ref)):
                pre = _hgrn_prepare(q_ref[rows[d], lanes], f_ref[rows[d], lanes], lbs[hp][d], d == 1)
                for n, val in enumerate(pre):
                    pre_ref[slot, 2 * hp + d, n] = val

    prepare(0, 0)

    def consume(ci, slot):
        rows = chunk_rows(ci)
        for hp, lanes in enumerate(head_lanes):
            for d, acc_ref in enumerate((of_ref, ob_ref)):
                qs, b, c = (pre_ref[slot, 2 * hp + d, n] for n in range(3))
                o, st = _hgrn_chunk(qs, b, c, v_ref[rows[d], lanes], st_ref[hp, d], d == 1)
                acc_ref[rows[d], lanes] = o
                st_ref[hp, d] = st

    def body(ci, carry):
        slot = ci % 2
        consume(ci, slot)
        prepare(ci + 1, 1 - slot)
        return carry

    if n_chunks <= HGRN_UNROLL_MAX_CHUNKS:
        for ci in range(n_chunks - 1):
            body(ci, 0)
    else:
        lax.fori_loop(0, n_chunks - 1, body, 0)
    consume(n_chunks - 1, (n_chunks - 1) % 2)

    for hp, lanes in enumerate(head_lanes):
        if s_ref is not None:
            for d in range(2):
                s_ref[d, hp] = st_ref[hp, d].T
        o = of_ref[:, lanes] + ob_ref[:, lanes]
        o_ref[:, lanes] = (_rms(o, ong_ref[...]) * _silu(g_ref[:, lanes])).astype(o_ref.dtype)


def _hgrn(proj, lb_raw, onorm_g_l, layer, *, state0=None, o_prev=None, s_prev=None):
    latent = state0 is not None
    n_batch, seq, row0 = (DEC_BATCH, DEC_SEQ, N_CTX // DEC_SEQ) if latent else (BATCH, SEQ, 0)
    width = HGRN_HP * HEAD_DIM

    def col(block):
        return pl.BlockSpec((seq, width), lambda b, h: (row0 + b, block // HGRN_HP + h))

    in_specs = [
        pl.BlockSpec((2 * DEPTH, width), lambda b, h: (0, h)),
        col(COL_HQ), col(COL_HI), col(COL_HG), col(COL_HFF), col(COL_HFB),
        pl.BlockSpec((1, HEAD_DIM), lambda b, h: (0, 0)),
    ]
    args = [lb_raw.reshape(2 * DEPTH, HGRN_W), proj, proj, proj, proj, proj,
            onorm_g_l.reshape(1, HEAD_DIM)]
    o_shape = jax.ShapeDtypeStruct((N_ROWS, HGRN_W), BF16)
    o_spec = pl.BlockSpec((seq, width), lambda b, h: (row0 + b, h))
    aliases = {}
    if latent:
        in_specs += [pl.BlockSpec((None, None, 2, HGRN_HP, HEAD_DIM, HEAD_DIM),
                                  lambda b, h: (b, layer, 0, h, 0, 0)), _ANY]
        args += [state0, o_prev]
        aliases = {len(args) - 1: 0}
        out_shape, out_specs = o_shape, o_spec
    else:
        if s_prev is not None:
            in_specs.append(_ANY)
            args.append(s_prev)
            aliases = {len(args) - 1: 1}
        out_shape = (o_shape, jax.ShapeDtypeStruct(
            (BATCH, DEPTH, 2, HGRN_HEADS, HEAD_DIM, HEAD_DIM), F32))
        out_specs = (o_spec, pl.BlockSpec((None, None, 2, HGRN_HP, HEAD_DIM, HEAD_DIM),
                                          lambda b, h: (b, layer, 0, h, 0, 0)))
    return pl.pallas_call(
        functools.partial(_hgrn_kernel, layer=layer, seq=seq, latent=latent, n_alias=len(aliases)),
        out_shape=out_shape,
        grid=(n_batch, HGRN_HEADS // HGRN_HP),
        in_specs=in_specs,
        out_specs=out_specs,
        scratch_shapes=[pltpu.VMEM((seq, width), F32), pltpu.VMEM((seq, width), F32),
                        pltpu.VMEM((HGRN_HP, 2, HEAD_DIM, HEAD_DIM), F32),
                        pltpu.VMEM((2, 2 * HGRN_HP, 3, HGRN_CHUNK, HEAD_DIM), F32)],
        input_output_aliases=aliases,
        compiler_params=_params("parallel", "parallel"),
        name=f"hgrn_l{layer}_{'lat' if latent else 'ctx'}",
    )(*args)


def _swap_pairs(x, width):
    lanes = x.shape[-1]
    lane = lax.broadcasted_iota(jnp.int32, x.shape, x.ndim - 1)
    from_right = pltpu.roll(x, lanes - width, x.ndim - 1)
    from_left = pltpu.roll(x, width, x.ndim - 1)
    return jnp.where(lane % (2 * width) < width, from_right, from_left)


def _rope(x, cos, sin_signed, quarter):
    return x * cos + _swap_pairs(x, quarter) * sin_signed


def _rms_halves(x, g):
    half = x.shape[-1] // 2
    lane = lax.broadcasted_iota(jnp.int32, x.shape, x.ndim - 1)
    lo = lane < half
    sq = x * x
    ms_lo = jnp.sum(jnp.where(lo, sq, 0.0), axis=-1, keepdims=True) / half
    ms_hi = jnp.sum(jnp.where(lo, 0.0, sq), axis=-1, keepdims=True) / half
    ms = jnp.where(lo, ms_lo, ms_hi)
    return (x * lax.rsqrt(ms + EPS)) * g


def _gqa_kernel(*refs, seq, latent, n_alias):
    qg_ref, kg_ref = refs[:2]
    q_refs = refs[2:2 + GQA_GROUP]
    k_ref, v_ref = refs[2 + GQA_GROUP:4 + GQA_GROUP]
    rest = refs[4 + GQA_GROUP:]
    if latent:
        ck_ref, cv_ref, cos_ref, sin_ref = rest[:4]
        o_ref, kt_ref, vt_ref = rest[4 + n_alias:]
    else:
        o_ref, kn_ref, vo_ref, kt_ref, vt_ref = rest[n_alias:]
    past = PAST_LEN if latent else 0
    quarter = HEAD_DIM // 4

    kn = _rms(k_ref[...], kg_ref[...])
    v = v_ref[...]
    if latent:
        kn = _rope(kn, cos_ref[...], sin_ref[...], quarter)
        kt_ref[0:past, :] = ck_ref[...].astype(BF16)
        vt_ref[0:past, 0:HEAD_DIM] = cv_ref[...].astype(BF16)
    else:
        kn_ref[...] = kn
        vo_ref[...] = v
    kt_ref[past:past + seq, :] = kn.astype(BF16)
    vt_ref[past:past + seq, 0:HEAD_DIM] = v.astype(BF16)
    vt_ref[:, HEAD_DIM:] = jnp.ones((past + seq, HEAD_DIM), BF16)

    q_scale = HEAD_DIM ** -0.5 * LOG2E

    def attend(q_rows):
        s = _dot_nt(jnp.concatenate(q_rows, axis=0), kt_ref[...])
        p = jnp.exp2(s - jnp.max(s, axis=-1, keepdims=True))
        ov = _dot(p.astype(BF16), vt_ref[...])
        o = (ov[:, :HEAD_DIM] / ov[:, HEAD_DIM:]).astype(o_ref.dtype)
        n = q_rows[0].shape[0]
        return [o[i * n:(i + 1) * n, :] for i in range(len(q_rows))]

    def query(q_ref, rows):
        qn = _rms(q_ref[rows, :], qg_ref[...])
        if latent:
            qn = _rope(qn, cos_ref[rows, :], sin_ref[rows, :], quarter)
        return (qn * q_scale).astype(BF16)

    head_lanes = [slice(g * HEAD_DIM, (g + 1) * HEAD_DIM) for g in range(GQA_GROUP)]
    if seq <= GQA_STACK_MAX_SEQ:
        rows = slice(0, seq)
        outs = attend([query(q_ref, rows) for q_ref in q_refs])
        for lanes, o in zip(head_lanes, outs):
            o_ref[rows, lanes] = o
    else:
        for lanes, q_ref in zip(head_lanes, q_refs):
            for r in range(seq // GQA_Q_BLOCK):
                rows = slice(r * GQA_Q_BLOCK, (r + 1) * GQA_Q_BLOCK)
                o_ref[rows, lanes] = attend([query(q_ref, rows)])[0]


def _gqa(proj, qnorm_g, knorm_g, layer, *, latent_args=None, kv_prev=None):
    latent = latent_args is not None
    n_batch, seq, row0 = (DEC_BATCH, DEC_SEQ, N_CTX // DEC_SEQ) if latent else (BATCH, SEQ, 0)
    past = PAST_LEN if latent else 0

    def vec():
        return pl.BlockSpec((1, HEAD_DIM), lambda b, h: (0, 0))

    def q_spec(g):
        return pl.BlockSpec((seq, HEAD_DIM), lambda b, h: (row0 + b, COL_GQ + h * GQA_GROUP + g))

    in_specs = [vec(), vec()] + [q_spec(g) for g in range(GQA_GROUP)] + [
        pl.BlockSpec((seq, HEAD_DIM), lambda b, h: (row0 + b, COL_GK + h)),
        pl.BlockSpec((seq, HEAD_DIM), lambda b, h: (row0 + b, COL_GV + h)),
    ]
    args = ([qnorm_g.reshape(1, HEAD_DIM), knorm_g.reshape(1, HEAD_DIM)]
            + [proj] * (GQA_GROUP + 2))
    o_spec = pl.BlockSpec((seq, GQA_GROUP * HEAD_DIM), lambda b, h: (row0 + b, h))
    o_shape = jax.ShapeDtypeStruct((N_ROWS, GQA_W), BF16)
    cache_spec = pl.BlockSpec((None, None, SEQ, HEAD_DIM), lambda b, h: (b, layer, 0, h))
    aliases = {}
    if latent:
        cache_k, cache_v, cos, sin_signed, o_prev = latent_args
        table_spec = pl.BlockSpec((seq, HEAD_DIM), lambda b, h: (0, 0))
        in_specs += [cache_spec, cache_spec, table_spec, table_spec, _ANY]
        args += [cache_k.reshape(DEC_BATCH, DEPTH, PAST_LEN, GQA_KV_W),
                 cache_v.reshape(DEC_BATCH, DEPTH, PAST_LEN, GQA_KV_W), cos, sin_signed, o_prev]
        aliases = {len(args) - 1: 0}
        out_shape, out_specs = o_shape, o_spec
    else:
        if kv_prev is not None:
            in_specs += [_ANY, _ANY]
            args += list(kv_prev)
            aliases = {len(args) - 2: 1, len(args) - 1: 2}
        new_shape = jax.ShapeDtypeStruct((BATCH, DEPTH, SEQ, GQA_KV_W), F32)
        out_shape = (o_shape, new_shape, new_shape)
        out_specs = (o_spec, cache_spec, cache_spec)
    return pl.pallas_call(
        functools.partial(_gqa_kernel, seq=seq, latent=latent, n_alias=len(aliases)),
        out_shape=out_shape,
        grid=(n_batch, GQA_KV_HEADS),
        in_specs=in_specs,
        out_specs=out_specs,
        scratch_shapes=[pltpu.VMEM((past + seq, HEAD_DIM), BF16),
                        pltpu.VMEM((past + seq, 2 * HEAD_DIM), BF16)],
        input_output_aliases=aliases,
        compiler_params=_params("parallel", "parallel"),
        name=f"gqa_l{layer}_{'lat' if latent else 'ctx'}",
    )(*args)


def _diff_kernel(*refs, seq, latent, lam_init, n_alias, heads):
    qg_ref, kg_ref, sg_ref, lam_ref, q_ref, k_ref, v_ref = refs[:7]
    if latent:
        ck_ref, cv_ref, cos_ref, sin_ref = refs[7:11]
        o_ref, kt_ref, vt_ref = refs[11 + n_alias:]
    else:
        o_ref, kn_ref, vo_ref, kt_ref, vt_ref = refs[7 + n_alias:]
    past = PAST_LEN if latent else 0
    quarter = DIFF_QK_DIM // 4
    head_lanes = [slice(hd * HEAD_DIM, (hd + 1) * HEAD_DIM) for hd in range(heads)]

    for hd, lanes in enumerate(head_lanes):
        kn = _rms_halves(k_ref[:, lanes], kg_ref[...])
        v = v_ref[:, lanes]
        if latent:
            kn = _rope(kn, cos_ref[...], sin_ref[...], quarter)
            kt_ref[0:past, lanes] = ck_ref[:, lanes].astype(BF16)
            vt_ref[hd, 0:past, 0:HEAD_DIM] = cv_ref[:, lanes].astype(BF16)
        else:
            kn_ref[:, lanes] = kn
            vo_ref[:, lanes] = v
        kt_ref[past:past + seq, lanes] = kn.astype(BF16)
        vt_ref[hd, past:past + seq, 0:HEAD_DIM] = v.astype(BF16)
        vt_ref[hd, :, HEAD_DIM:] = jnp.ones((past + seq, HEAD_DIM), BF16)

    lv = lam_ref[...]
    lam = (jnp.exp(jnp.sum(lv[0:1, :] * lv[1:2, :], axis=-1, keepdims=True))
           - jnp.exp(jnp.sum(lv[2:3, :] * lv[3:4, :], axis=-1, keepdims=True)) + lam_init)
    q_scale = DIFF_QK_DIM ** -0.5 * LOG2E

    def attend(q_masked, kt, vt):
        s = _dot_nt(q_masked.astype(BF16), kt)
        p = jnp.exp2(s - jnp.max(s, axis=-1, keepdims=True))
        ov = _dot(p.astype(BF16), vt)
        return ov[:, :HEAD_DIM] / ov[:, HEAD_DIM:]

    q_block = min(seq, Q_BLOCK)
    for hd, lanes in enumerate(head_lanes):
        for r in range(seq // q_block):
            rows = slice(r * q_block, (r + 1) * q_block)
            qn = _rms_halves(q_ref[rows, lanes], qg_ref[...])
            if latent:
                qn = _rope(qn, cos_ref[rows, :], sin_ref[rows, :], quarter)
            qn = qn * q_scale
            lo = lax.broadcasted_iota(jnp.int32, qn.shape, 1) < DIFF_QK_DIM
            kt, vt = kt_ref[:, lanes], vt_ref[hd]
            o = attend(jnp.where(lo, qn, 0.0), kt, vt) - lam * attend(jnp.where(lo, 0.0, qn), kt, vt)
            o = _rms(o, sg_ref[...]) * (1.0 - lam_init)
            o_ref[rows, lanes] = o.astype(o_ref.dtype)


def _diff(proj, qnorm_g, knorm_g, subln_g, lam_params, layer, *, latent_args=None, kv_prev=None):
    latent = latent_args is not None
    n_batch, seq, row0 = (DEC_BATCH, DEC_SEQ, N_CTX // DEC_SEQ) if latent else (BATCH, SEQ, 0)
    past = PAST_LEN if latent else 0
    lam_init = 0.8 - 0.6 * math.exp(-0.3 * layer)
    heads = DIFF_HP_LAT if latent else DIFF_HP_CTX
    width = heads * HEAD_DIM

    def vec():
        return pl.BlockSpec((1, HEAD_DIM), lambda b, h: (0, 0))

    def col(block):
        return pl.BlockSpec((seq, width), lambda b, h: (row0 + b, block // heads + h))

    in_specs = [
        vec(), vec(), vec(),
        pl.BlockSpec((4, DIFF_QK_DIM), lambda b, h: (0, 0)),
        col(COL_DQ), col(COL_DK), col(COL_DV),
    ]
    args = [jnp.tile(qnorm_g, 2).reshape(1, HEAD_DIM), jnp.tile(knorm_g, 2).reshape(1, HEAD_DIM),
            subln_g.reshape(1, HEAD_DIM), lam_params, proj, proj, proj]
    o_spec = pl.BlockSpec((seq, width), lambda b, h: (row0 + b, h))
    o_shape = jax.ShapeDtypeStruct((N_ROWS, DIFF_W), BF16)
    cache_spec = pl.BlockSpec((None, None, SEQ, width), lambda b, h: (b, layer, 0, h))
    aliases = {}
    if latent:
        cache_k, cache_v, cos, sin_signed, o_prev = latent_args
        table_spec = pl.BlockSpec((seq, HEAD_DIM), lambda b, h: (0, 0))
        in_specs += [cache_spec, cache_spec, table_spec, table_spec, _ANY]
        args += [cache_k.reshape(DEC_BATCH, DEPTH, PAST_LEN, DIFF_W),
                 cache_v.reshape(DEC_BATCH, DEPTH, PAST_LEN, DIFF_W), cos, sin_signed, o_prev]
        aliases = {len(args) - 1: 0}
        out_shape, out_specs = o_shape, o_spec
    else:
        if kv_prev is not None:
            in_specs += [_ANY, _ANY]
            args += list(kv_prev)
            aliases = {len(args) - 2: 1, len(args) - 1: 2}
        new_shape = jax.ShapeDtypeStruct((BATCH, DEPTH, SEQ, DIFF_W), F32)
        out_shape = (o_shape, new_shape, new_shape)
        out_specs = (o_spec, cache_spec, cache_spec)
    return pl.pallas_call(
        functools.partial(_diff_kernel, seq=seq, latent=latent, lam_init=lam_init,
                          n_alias=len(aliases), heads=heads),
        out_shape=out_shape,
        grid=(n_batch, DIFF_HEADS // heads),
        in_specs=in_specs,
        out_specs=out_specs,
        scratch_shapes=[pltpu.VMEM((past + seq, width), BF16),
                        pltpu.VMEM((heads, past + seq, 2 * HEAD_DIM), BF16)],
        input_output_aliases=aliases,
        compiler_params=_params("parallel", "parallel"),
        name=f"diff_l{layer}_{'lat' if latent else 'ctx'}",
    )(*args)


def _rope_tables(n_tokens, dim, repeat):
    quarter = dim // 4
    t = jnp.arange(n_tokens)
    pos = jnp.stack([t // GRID_W, t % GRID_W], axis=-1).astype(F32)
    inv = ROPE_BASE ** (-jnp.arange(quarter, dtype=F32) / quarter)
    ang = pos[:, :, None] * inv
    cos, sin = jnp.cos(ang), jnp.sin(ang)
    cos_l = jnp.concatenate([cos, cos], axis=-1).reshape(n_tokens, dim)
    sin_l = jnp.concatenate([-sin, sin], axis=-1).reshape(n_tokens, dim)
    return jnp.tile(cos_l, (1, repeat)), jnp.tile(sin_l, (1, repeat))


def kernel(x_prompt, x_sample, c, cache_gqa_k, cache_gqa_v, cache_diff_k, cache_diff_v, state_hgrn,
           c_ctx, w_mod, b_mod, norm_g, ffn_w_gate, ffn_w_up, ffn_w_down, w_in, w_out, hgrn_lb_raw,
           hgrn_onorm_g, gqa_qnorm_g, gqa_knorm_g, diff_qnorm_g, diff_knorm_g, diff_lambda,
           diff_subln_g):
    cond = jnp.concatenate(
        [c_ctx[None, :], c, jnp.zeros((COND_PAD - N_COND, D_MODEL), F32)], axis=0)
    mod = _modulation(cond, w_mod, b_mod)

    cos_g, sin_g = _rope_tables(DEC_SEQ, HEAD_DIM, 1)
    cos_d, sin_d = _rope_tables(DEC_SEQ, DIFF_QK_DIM, 2)
    cache_dk = cache_diff_k.reshape(DEC_BATCH, DEPTH, PAST_LEN, DIFF_HEADS, 2 * DIFF_QK_DIM)
    ctx_rows = dict(n_tiles=CTX_TILES, tile0=0)
    lat_rows = dict(n_tiles=LAT_TILES, tile0=CTX_TILES)

    x = None
    states = gqa_kv = diff_kv = None
    for l in range(DEPTH):
        ffn_w = (mod[l], norm_g[l], ffn_w_gate, ffn_w_up, ffn_w_down, l)
        if l == 0:
            x = _ffn(x_prompt.reshape(N_CTX, D_MODEL), *ffn_w, 0, **ctx_rows)
            x = _ffn(x_sample.reshape(N_LAT, D_MODEL), *ffn_w, 0, **lat_rows,
                     out_off=CTX_TILES, o_prev=x)
        else:
            x = _ffn(x, *ffn_w, 0)
        proj = _proj_in(x, mod[l], norm_g[l], w_in, l)

        o_h, states = _hgrn(proj, hgrn_lb_raw, hgrn_onorm_g[l], l, s_prev=states)
        o_h = _hgrn(proj, hgrn_lb_raw, hgrn_onorm_g[l], l, state0=state_hgrn, o_prev=o_h)

        gqa_w = (proj, gqa_qnorm_g[l], gqa_knorm_g[l], l)
        o_g, *gqa_kv = _gqa(*gqa_w, kv_prev=gqa_kv)
        o_g = _gqa(*gqa_w, latent_args=(cache_gqa_k, cache_gqa_v, cos_g, sin_g, o_g))

        diff_w = (proj, diff_qnorm_g[l], diff_knorm_g[l], diff_subln_g[l], diff_lambda[l], l)
        o_d, *diff_kv = _diff(*diff_w, kv_prev=diff_kv)
        o_d = _diff(*diff_w, latent_args=(cache_dk, cache_diff_v, cos_d, sin_d, o_d))

        x = _proj_out(x, mod[l], o_h, o_g, o_d, w_out, l)
        if l < DEPTH - 1:
            x = _ffn(x, *ffn_w, 1)
        else:
            y_prompt = _ffn(x, *ffn_w, 1, **ctx_rows, out_rows=N_CTX)
            y_sample = _ffn(x, *ffn_w, 1, **lat_rows, in_off=CTX_TILES, out_rows=N_LAT)

    return (y_prompt.reshape(BATCH, SEQ, D_MODEL), y_sample.reshape(DEC_BATCH, DEC_SEQ, D_MODEL),
            gqa_kv[0].reshape(BATCH, DEPTH, SEQ, GQA_KV_HEADS, HEAD_DIM),
            gqa_kv[1].reshape(BATCH, DEPTH, SEQ, GQA_KV_HEADS, HEAD_DIM),
            diff_kv[0].reshape(BATCH, DEPTH, SEQ, DIFF_HEADS, 2, DIFF_QK_DIM),
            diff_kv[1].reshape(BATCH, DEPTH, SEQ, DIFF_HEADS, HEAD_DIM),
            states)
```

```python
import functools
import math

import jax
import jax.numpy as jnp
from jax import lax
from jax.experimental import pallas as pl
from jax.experimental.pallas import tpu as pltpu

F32 = jnp.float32
BF16 = jnp.bfloat16

D_MODEL = 2048
BATCH = 16
SEQ = 256
DEPTH = 2
DEC_BATCH = 2
DEC_SEQ = 1024
PAST_LEN = 256
GRID_W = 64
HEAD_DIM = 128
HGRN_HEADS = 4
GQA_Q_HEADS = 6
GQA_KV_HEADS = 2
GQA_GROUP = GQA_Q_HEADS // GQA_KV_HEADS
DIFF_HEADS = 6
DIFF_QK_DIM = 64
FFN_DIM = 5632
N_MOD = 9
IN_WIDTH = 6144
ROPE_BASE = 10000.0
EPS = 1e-6
LOG2E = math.log2(math.e)

N_CTX = BATCH * SEQ
N_LAT = DEC_BATCH * DEC_SEQ
N_ROWS = N_CTX + N_LAT
N_COND = 1 + DEC_BATCH
COND_PAD = 8

COL_HQ, COL_HI, COL_HG, COL_HFF, COL_HFB = 0, 4, 8, 12, 16
COL_GQ, COL_GK, COL_GV = 20, 26, 28
COL_DQ, COL_DK, COL_DV = 30, 36, 42

HGRN_W = HGRN_HEADS * HEAD_DIM
GQA_W = GQA_Q_HEADS * HEAD_DIM
GQA_KV_W = GQA_KV_HEADS * HEAD_DIM
DIFF_W = DIFF_HEADS * HEAD_DIM

VMEM_LIMIT = 60 * 1024 * 1024

TM = 1024
ROW_CHUNK = 128
FFN_TF = 256
FFN_TILES_PER_STEP = 2
FFN_NB = 512
IN_TN = 512
IN_TILES_PER_STEP = 2
OUT_TN = 512
OUT_TILES_PER_STEP = 2
MOD_TN = 2048
HGRN_CHUNK = 64
HGRN_SUB = 8
HGRN_HP = 4
HGRN_UNROLL_MAX_CHUNKS = 4
Q_BLOCK = 256
GQA_Q_BLOCK = 256
GQA_STACK_MAX_SEQ = 256
DIFF_HP_CTX = 6
DIFF_HP_LAT = 2

CTX_TILES = N_CTX // TM
LAT_TILES = N_LAT // TM
ALL_TILES = CTX_TILES + LAT_TILES


def _cond_of_tile(i):
    tiles_per_latent = DEC_SEQ // TM
    return jnp.where(i < CTX_TILES, 0, 1 + (i - CTX_TILES) // tiles_per_latent)


def _silu(x):
    return x * jax.nn.sigmoid(x)


def _dot(a, b):
    return jnp.dot(a, b, preferred_element_type=F32)


def _dot_nt(a, b):
    return lax.dot_general(a, b, (((1,), (1,)), ((), ())), preferred_element_type=F32)


def _dot_tn(a, b):
    return lax.dot_general(a, b, (((0,), (0,)), ((), ())), preferred_element_type=F32)


def _rms(x, g):
    ms = jnp.mean(x * x, axis=-1, keepdims=True)
    return (x * lax.rsqrt(ms + EPS)) * g


def _params(*semantics):
    return pltpu.CompilerParams(dimension_semantics=semantics, vmem_limit_bytes=VMEM_LIMIT)


_ANY = pl.BlockSpec(memory_space=pl.ANY)


def _mod_kernel(cond_ref, w_ref, b_ref, o_ref):
    a = _silu(cond_ref[...]).astype(BF16)
    o_ref[...] = _dot(a, w_ref[...].astype(BF16)) + b_ref[...]


def _modulation(cond, w_mod, b_mod):
    width = N_MOD * D_MODEL
    out = pl.pallas_call(
        _mod_kernel,
        out_shape=jax.ShapeDtypeStruct((DEPTH, COND_PAD, width), F32),
        grid=(DEPTH, width // MOD_TN),
        in_specs=[
            pl.BlockSpec((COND_PAD, D_MODEL), lambda l, j: (0, 0)),
            pl.BlockSpec((None, D_MODEL, MOD_TN), lambda l, j: (l, 0, j)),
            pl.BlockSpec((None, 1, MOD_TN), lambda l, j: (l, 0, j)),
        ],
        out_specs=pl.BlockSpec((None, COND_PAD, MOD_TN), lambda l, j: (l, 0, j)),
        compiler_params=_params("parallel", "parallel"),
        name="modulation",
    )(cond, w_mod, b_mod.reshape(DEPTH, 1, width))
    return out[:, :N_COND].reshape(DEPTH, N_COND, N_MOD, D_MODEL)


def _mod_norm_into(x_ref, mod_ref, g_ref, h_ref, sub, row0=0):
    shift = mod_ref[3 * sub:3 * sub + 1, :]
    gain = g_ref[sub:sub + 1, :] * (1.0 + mod_ref[3 * sub + 1:3 * sub + 2, :])

    def body(r, carry):
        rows = pl.ds(pl.multiple_of(row0 + r * ROW_CHUNK, ROW_CHUNK), ROW_CHUNK)
        h_ref[rows, :] = (_rms(x_ref[rows, :], gain) + shift).astype(BF16)
        return carry

    lax.fori_loop(0, TM // ROW_CHUNK, body, 0)


def _ffn_kernel(x_ref, mod_ref, g_ref, wg_ref, wu_ref, wd_ref, *rest, sub, tile0):
    o_ref, h_ref = rest[-2:]
    j = pl.program_id(1)
    conds = [_cond_of_tile(tile0 + pl.program_id(0) * FFN_TILES_PER_STEP + part)
             for part in range(FFN_TILES_PER_STEP)]

    @pl.when(j == 0)
    def _():
        for part, cond in enumerate(conds):
            _mod_norm_into(x_ref, mod_ref.at[cond], g_ref, h_ref, sub, row0=part * TM)
        o_ref[...] = jnp.zeros_like(o_ref)

    for part in range(FFN_TILES_PER_STEP):
        rows = slice(part * TM, (part + 1) * TM)
        h = h_ref[rows, :]
        gate_act = _dot(h, wg_ref[...].astype(BF16))
        up = _dot(h, wu_ref[...].astype(BF16))
        a = (_silu(gate_act) * up).astype(BF16)
        for n in range(0, D_MODEL, FFN_NB):
            o_ref[rows, n:n + FFN_NB] += _dot(a, wd_ref[:, n:n + FFN_NB].astype(BF16))

    @pl.when(j == pl.num_programs(1) - 1)
    def _():
        for part, cond in enumerate(conds):
            gate = mod_ref[cond, 3 * sub + 2:3 * sub + 3, :]

            def body(r, carry):
                rows = pl.ds(pl.multiple_of(part * TM + r * ROW_CHUNK, ROW_CHUNK), ROW_CHUNK)
                o_ref[rows, :] = x_ref[rows, :] + gate * (0.5 * o_ref[rows, :])
                return carry

            lax.fori_loop(0, TM // ROW_CHUNK, body, 0)


def _ffn(x, mod_l, norm_g_l, w_gate, w_up, w_down, layer, which, *, n_tiles=ALL_TILES, in_off=0,
         out_off=0, tile0=0, out_rows=N_ROWS, o_prev=None):
    sub = 2 * which
    per = FFN_TILES_PER_STEP
    rows = per * TM
    assert n_tiles % per == 0 and in_off % per == 0 and out_off % per == 0
    in_specs = [
        pl.BlockSpec((rows, D_MODEL), lambda i, j: (in_off // per + i, 0), pipeline_mode=pl.Buffered(1)),
        pl.BlockSpec((N_COND, N_MOD, D_MODEL), lambda i, j: (0, 0, 0)),
        pl.BlockSpec((3, D_MODEL), lambda i, j: (0, 0)),
        pl.BlockSpec((None, None, D_MODEL, FFN_TF), lambda i, j: (layer, which, 0, j)),
        pl.BlockSpec((None, None, D_MODEL, FFN_TF), lambda i, j: (layer, which, 0, j)),
        pl.BlockSpec((None, None, FFN_TF, D_MODEL), lambda i, j: (layer, which, j, 0)),
    ]
    args = [x, mod_l, norm_g_l, w_gate, w_up, w_down]
    aliases = {}
    if o_prev is not None:
        in_specs.append(_ANY)
        args.append(o_prev)
        aliases = {len(args) - 1: 0}
    return pl.pallas_call(
        functools.partial(_ffn_kernel, sub=sub, tile0=tile0),
        out_shape=jax.ShapeDtypeStruct((out_rows, D_MODEL), F32),
        grid=(n_tiles // per, FFN_DIM // FFN_TF),
        in_specs=in_specs,
        out_specs=pl.BlockSpec((rows, D_MODEL), lambda i, j: (out_off // per + i, 0),
                               pipeline_mode=pl.Buffered(1)),
        scratch_shapes=[pltpu.VMEM((rows, D_MODEL), BF16)],
        input_output_aliases=aliases,
        compiler_params=_params("parallel", "arbitrary"),
        name=f"ffn_l{layer}_h{which}_t{tile0}n{n_tiles}",
    )(*args)


def _proj_in_kernel(x_ref, mod_ref, g_ref, w_ref, o_ref, h_ref):
    @pl.when(pl.program_id(1) == 0)
    def _():
        for part in range(IN_TILES_PER_STEP):
            cond = _cond_of_tile(pl.program_id(0) * IN_TILES_PER_STEP + part)
            _mod_norm_into(x_ref, mod_ref.at[cond], g_ref, h_ref, 1, row0=part * TM)

    w = w_ref[...].astype(BF16)
    for part in range(IN_TILES_PER_STEP):
        rows = slice(part * TM, (part + 1) * TM)
        o_ref[rows, :] = _dot(h_ref[rows, :], w)


def _proj_in(x, mod_l, norm_g_l, w_in, layer):
    rows = IN_TILES_PER_STEP * TM
    return pl.pallas_call(
        _proj_in_kernel,
        out_shape=jax.ShapeDtypeStruct((N_ROWS, IN_WIDTH), F32),
        grid=(N_ROWS // rows, IN_WIDTH // IN_TN),
        in_specs=[
            pl.BlockSpec((rows, D_MODEL), lambda i, j: (i, 0)),
            pl.BlockSpec((N_COND, N_MOD, D_MODEL), lambda i, j: (0, 0, 0)),
            pl.BlockSpec((3, D_MODEL), lambda i, j: (0, 0)),
            pl.BlockSpec((None, D_MODEL, IN_TN), lambda i, j: (layer, 0, j)),
        ],
        out_specs=pl.BlockSpec((rows, IN_TN), lambda i, j: (i, j)),
        scratch_shapes=[pltpu.VMEM((rows, D_MODEL), BF16)],
        compiler_params=_params("parallel", "arbitrary"),
        name=f"proj_in_l{layer}",
    )(x, mod_l, norm_g_l, w_in)


def _proj_out_kernel(x_ref, mod_ref, oh_ref, og_ref, od_ref, w_ref, o_ref):
    w_h = w_ref[0:HGRN_W, :].astype(BF16)
    w_g = w_ref[HGRN_W:HGRN_W + GQA_W, :].astype(BF16)
    w_d = w_ref[HGRN_W + GQA_W:, :].astype(BF16)
    for part in range(OUT_TILES_PER_STEP):
        rows = slice(part * TM, (part + 1) * TM)
        cond = _cond_of_tile(pl.program_id(0) * OUT_TILES_PER_STEP + part)
        m = _dot(oh_ref[rows, :], w_h) + _dot(og_ref[rows, :], w_g) + _dot(od_ref[rows, :], w_d)
        o_ref[rows, :] = x_ref[rows, :] + mod_ref[cond, 5:6, :] * m


def _proj_out(x, mod_l, o_h, o_g, o_d, w_out, layer):
    rows = OUT_TILES_PER_STEP * TM
    return pl.pallas_call(
        _proj_out_kernel,
        out_shape=jax.ShapeDtypeStruct((N_ROWS, D_MODEL), F32),
        grid=(N_ROWS // rows, D_MODEL // OUT_TN),
        in_specs=[
            pl.BlockSpec((rows, OUT_TN), lambda i, j: (i, j)),
            pl.BlockSpec((N_COND, N_MOD, OUT_TN), lambda i, j: (0, 0, j)),
            pl.BlockSpec((rows, HGRN_W), lambda i, j: (i, 0)),
            pl.BlockSpec((rows, GQA_W), lambda i, j: (i, 0)),
            pl.BlockSpec((rows, DIFF_W), lambda i, j: (i, 0)),
            pl.BlockSpec((None, D_MODEL, OUT_TN), lambda i, j: (layer, 0, j)),
        ],
        out_specs=pl.BlockSpec((rows, OUT_TN), lambda i, j: (i, j)),
        compiler_params=_params("parallel", "parallel"),
        name=f"proj_out_l{layer}",
    )(x, mod_l, o_h, o_g, o_d, w_out)


def _log2_forget_and_key(z, lb):
    z2 = z * LOG2E
    soft = jnp.log2(1.0 + jnp.exp2(-jnp.abs(z2)))
    log_1mlb = jnp.log1p(-lb) * LOG2E
    a = jnp.log2(lb)
    c = log_1mlb + (jnp.minimum(z2, 0.0) - soft)
    log_f = jnp.maximum(a, c) + jnp.log2(1.0 + jnp.exp2(-jnp.abs(a - c)))
    log_k = log_1mlb + (jnp.minimum(-z2, 0.0) - soft)
    return log_f, log_k


def _cumsum_rows(x, reverse):
    tile = HGRN_SUB
    row = lax.broadcasted_iota(jnp.int32, (tile, 1), 0)
    tiles = []
    for j in range(x.shape[0] // tile):
        y = x[j * tile:(j + 1) * tile, :]
        for sh in (1, 2, 4):
            if reverse:
                y = y + jnp.where(row < tile - sh, pltpu.roll(y, tile - sh, 0), 0.0)
            else:
                y = y + jnp.where(row >= sh, pltpu.roll(y, sh, 0), 0.0)
        tiles.append(y)
    order = range(len(tiles) - 1, -1, -1) if reverse else range(len(tiles))
    carry = None
    for j in order:
        if carry is not None:
            tiles[j] = tiles[j] + carry
        carry = tiles[j][0:1, :] if reverse else tiles[j][tile - 1:tile, :]
    return jnp.concatenate(tiles, axis=0)


def _hgrn_prepare(q, z, lb, reverse):
    lf2, lk2 = _log2_forget_and_key(z, lb)
    b = _cumsum_rows(lf2, reverse)
    return _silu(q), b, b - lk2


def _hgrn_chunk(qs, b, c, v, st, reverse):
    C, SUB = HGRN_CHUNK, HGRN_SUB
    n_sub = C // SUB
    b_tot = b[0:1, :] if reverse else b[C - 1:C, :]

    o = _dot_nt((qs * jnp.exp2(b)).astype(BF16), st.astype(BF16))

    terms = []
    for i in range(n_sub):
        lo = i * SUB
        qi, bi, ci = qs[lo:lo + SUB, :], b[lo:lo + SUB, :], c[lo:lo + SUB, :]
        for s in range(SUB):
            terms.append(qi * jnp.exp2(bi - ci[s:s + 1, :]))
    k_sums = _dot(jnp.concatenate(terms, axis=0).astype(BF16), jnp.ones((HEAD_DIM, C), BF16))

    row = lax.broadcasted_iota(jnp.int32, (SUB, C), 0)
    lane = lax.broadcasted_iota(jnp.int32, (SUB, C), 1)
    lane_s = lane % SUB
    causal = (lane_s >= row) if reverse else (lane_s <= row)
    blocks = []
    for i in range(n_sub):
        lo, hi = i * SUB, (i + 1) * SUB
        diag = k_sums[lo * SUB:(lo + 1) * SUB, :]
        for s in range(1, SUB):
            diag = jnp.where(lane_s == s, k_sums[(lo + s) * SUB:(lo + s + 1) * SUB, :], diag)
        has_off = (i < n_sub - 1) if reverse else (i > 0)
        if has_off:
            ref = b[hi:hi + 1, :] if reverse else b[lo - 1:lo, :]
            qt = (qs[lo:hi, :] * jnp.exp2(b[lo:hi, :] - ref)).astype(BF16)
            if reverse:
                kt = jnp.concatenate([jnp.zeros((hi, HEAD_DIM), F32), jnp.exp2(ref - c[hi:, :])], axis=0)
            else:
                kt = jnp.concatenate([jnp.exp2(ref - c[:lo, :]), jnp.zeros((C - lo, HEAD_DIM), F32)], axis=0)
            off = _dot_nt(qt, kt.astype(BF16))
        else:
            off = jnp.zeros((SUB, C), F32)
        blocks.append(jnp.where((lane // SUB == i) & causal, diag, off))
    a = jnp.concatenate(blocks, axis=0)
    o = o + _dot(a.astype(BF16), v.astype(BF16))

    st_new = jnp.exp2(b_tot) * st + _dot_tn(v.astype(BF16), jnp.exp2(b_tot - c).astype(BF16))
    return o, st_new


def _hgrn_kernel(*refs, layer, seq, latent, n_alias):
    raw_ref, q_ref, v_ref, g_ref, ff_ref, fb_ref, ong_ref = refs[:7]
    s0_ref = refs[7] if latent else None
    outs = refs[7 + (1 if latent else 0) + n_alias:]
    if latent:
        o_ref, of_ref, ob_ref, st_ref, pre_ref = outs
        s_ref = None
    else:
        o_ref, s_ref, of_ref, ob_ref, st_ref, pre_ref = outs
    C = HGRN_CHUNK
    n_chunks = seq // C

    def lower_bound(d, lanes):
        rows = [raw_ref[2 * l + d:2 * l + d + 1, lanes] for l in range(DEPTH)]
        m = functools.reduce(jnp.maximum, rows)
        e = [jnp.exp(r - m) for r in rows]
        tot = functools.reduce(lambda x, y: x + y, e)
        lb = jnp.zeros_like(m)
        for l in range(1, layer + 1):
            lb = lb + e[l] / tot
        return lb

    head_lanes = [slice(hp * HEAD_DIM, (hp + 1) * HEAD_DIM) for hp in range(HGRN_HP)]
    lbs = [[lower_bound(d, lanes) for d in range(2)] for lanes in head_lanes]

    for hp in range(HGRN_HP):
        for d in range(2):
            st_ref[hp, d] = s0_ref[d, hp].T if latent else jnp.zeros((HEAD_DIM, HEAD_DIM), F32)

    def chunk_rows(ci):
        if isinstance(ci, int):
            return pl.ds(ci * C, C), pl.ds((n_chunks - 1 - ci) * C, C)
        return (pl.ds(pl.multiple_of(ci * C, C), C),
                pl.ds(pl.multiple_of((n_chunks - 1 - ci) * C, C), C))

    def prepare(ci, slot):
        rows = chunk_rows(ci)
        for hp, lanes in enumerate(head_lanes):
            for d, f_ref in enumerate((ff_ref, fb_ref)):
                pre = _hgrn_prepare(q_ref[rows[d], lanes], f_ref[rows[d], lanes], lbs[hp][d], d == 1)
                for n, val in enumerate(pre):
                    pre_ref[slot, 2 * hp + d, n] = val

    prepare(0, 0)

    def consume(ci, slot):
        rows = chunk_rows(ci)
        for hp, lanes in enumerate(head_lanes):
            for d, acc_ref in enumerate((of_ref, ob_ref)):
                qs, b, c = (pre_ref[slot, 2 * hp + d, n] for n in range(3))
                o, st = _hgrn_chunk(qs, b, c, v_ref[rows[d], lanes], st_ref[hp, d], d == 1)
                acc_ref[rows[d], lanes] = o
                st_ref[hp, d] = st

    def step(ci, slot):
        consume(ci, slot)
        prepare(ci + 1, 1 - slot)

    n_steps = n_chunks - 1
    if n_chunks <= HGRN_UNROLL_MAX_CHUNKS:
        for ci in range(n_steps):
            step(ci, ci % 2)
    else:
        peeled = n_steps % 2
        for ci in range(peeled):
            step(ci, ci % 2)

        def pair(p, carry):
            ci = peeled + 2 * p
            step(ci, peeled % 2)
            step(ci + 1, 1 - peeled % 2)
            return carry

        lax.fori_loop(0, (n_steps - peeled) // 2, pair, 0)
    consume(n_chunks - 1, (n_chunks - 1) % 2)

    for hp, lanes in enumerate(head_lanes):
        if s_ref is not None:
            for d in range(2):
                s_ref[d, hp] = st_ref[hp, d].T
        o = of_ref[:, lanes] + ob_ref[:, lanes]
        o_ref[:, lanes] = (_rms(o, ong_ref[...]) * _silu(g_ref[:, lanes])).astype(o_ref.dtype)


def _hgrn(proj, lb_raw, onorm_g_l, layer, *, state0=None, o_prev=None, s_prev=None):
    latent = state0 is not None
    n_batch, seq, row0 = (DEC_BATCH, DEC_SEQ, N_CTX // DEC_SEQ) if latent else (BATCH, SEQ, 0)
    width = HGRN_HP * HEAD_DIM

    def col(block):
        return pl.BlockSpec((seq, width), lambda b, h: (row0 + b, block // HGRN_HP + h))

    in_specs = [
        pl.BlockSpec((2 * DEPTH, width), lambda b, h: (0, h)),
        col(COL_HQ), col(COL_HI), col(COL_HG), col(COL_HFF), col(COL_HFB),
        pl.BlockSpec((1, HEAD_DIM), lambda b, h: (0, 0)),
    ]
    args = [lb_raw.reshape(2 * DEPTH, HGRN_W), proj, proj, proj, proj, proj,
            onorm_g_l.reshape(1, HEAD_DIM)]
    o_shape = jax.ShapeDtypeStruct((N_ROWS, HGRN_W), BF16)
    o_spec = pl.BlockSpec((seq, width), lambda b, h: (row0 + b, h))
    aliases = {}
    if latent:
        in_specs += [pl.BlockSpec((None, None, 2, HGRN_HP, HEAD_DIM, HEAD_DIM),
                                  lambda b, h: (b, layer, 0, h, 0, 0)), _ANY]
        args += [state0, o_prev]
        aliases = {len(args) - 1: 0}
        out_shape, out_specs = o_shape, o_spec
    else:
        if s_prev is not None:
            in_specs.append(_ANY)
            args.append(s_prev)
            aliases = {len(args) - 1: 1}
        out_shape = (o_shape, jax.ShapeDtypeStruct(
            (BATCH, DEPTH, 2, HGRN_HEADS, HEAD_DIM, HEAD_DIM), F32))
        out_specs = (o_spec, pl.BlockSpec((None, None, 2, HGRN_HP, HEAD_DIM, HEAD_DIM),
                                          lambda b, h: (b, layer, 0, h, 0, 0)))
    return pl.pallas_call(
        functools.partial(_hgrn_kernel, layer=layer, seq=seq, latent=latent, n_alias=len(aliases)),
        out_shape=out_shape,
        grid=(n_batch, HGRN_HEADS // HGRN_HP),
        in_specs=in_specs,
        out_specs=out_specs,
        scratch_shapes=[pltpu.VMEM((seq, width), F32), pltpu.VMEM((seq, width), F32),
                        pltpu.VMEM((HGRN_HP, 2, HEAD_DIM, HEAD_DIM), F32),
                        pltpu.VMEM((2, 2 * HGRN_HP, 3, HGRN_CHUNK, HEAD_DIM), F32)],
        input_output_aliases=aliases,
        compiler_params=_params("parallel", "parallel"),
        name=f"hgrn_l{layer}_{'lat' if latent else 'ctx'}",
    )(*args)


def _swap_pairs(x, width):
    lanes = x.shape[-1]
    lane = lax.broadcasted_iota(jnp.int32, x.shape, x.ndim - 1)
    from_right = pltpu.roll(x, lanes - width, x.ndim - 1)
    from_left = pltpu.roll(x, width, x.ndim - 1)
    return jnp.where(lane % (2 * width) < width, from_right, from_left)


def _rope(x, cos, sin_signed, quarter):
    return x * cos + _swap_pairs(x, quarter) * sin_signed


def _rms_halves(x, g):
    half = x.shape[-1] // 2
    lane = lax.broadcasted_iota(jnp.int32, x.shape, x.ndim - 1)
    lo = lane < half
    sq = x * x
    ms_lo = jnp.sum(jnp.where(lo, sq, 0.0), axis=-1, keepdims=True) / half
    ms_hi = jnp.sum(jnp.where(lo, 0.0, sq), axis=-1, keepdims=True) / half
    ms = jnp.where(lo, ms_lo, ms_hi)
    return (x * lax.rsqrt(ms + EPS)) * g


def _gqa_kernel(*refs, seq, latent, n_alias):
    qg_ref, kg_ref = refs[:2]
    q_refs = refs[2:2 + GQA_GROUP]
    k_ref, v_ref = refs[2 + GQA_GROUP:4 + GQA_GROUP]
    rest = refs[4 + GQA_GROUP:]
    if latent:
        ck_ref, cv_ref, cos_ref, sin_ref = rest[:4]
        o_ref, kt_ref, vt_ref = rest[4 + n_alias:]
    else:
        o_ref, kn_ref, vo_ref, kt_ref, vt_ref = rest[n_alias:]
    past = PAST_LEN if latent else 0
    quarter = HEAD_DIM // 4

    kn = _rms(k_ref[...], kg_ref[...])
    v = v_ref[...]
    if latent:
        kn = _rope(kn, cos_ref[...], sin_ref[...], quarter)
        kt_ref[0:past, :] = ck_ref[...].astype(BF16)
        vt_ref[0:past, 0:HEAD_DIM] = cv_ref[...].astype(BF16)
    else:
        kn_ref[...] = kn
        vo_ref[...] = v
    kt_ref[past:past + seq, :] = kn.astype(BF16)
    vt_ref[past:past + seq, 0:HEAD_DIM] = v.astype(BF16)
    vt_ref[:, HEAD_DIM:] = jnp.ones((past + seq, HEAD_DIM), BF16)

    q_scale = HEAD_DIM ** -0.5 * LOG2E

    def attend(q_rows):
        s = _dot_nt(jnp.concatenate(q_rows, axis=0), kt_ref[...])
        p = jnp.exp2(s - jnp.max(s, axis=-1, keepdims=True))
        ov = _dot(p.astype(BF16), vt_ref[...])
        o = (ov[:, :HEAD_DIM] / ov[:, HEAD_DIM:]).astype(o_ref.dtype)
        n = q_rows[0].shape[0]
        return [o[i * n:(i + 1) * n, :] for i in range(len(q_rows))]

    def query(q_ref, rows):
        qn = _rms(q_ref[rows, :], qg_ref[...])
        if latent:
            qn = _rope(qn, cos_ref[rows, :], sin_ref[rows, :], quarter)
        return (qn * q_scale).astype(BF16)

    head_lanes = [slice(g * HEAD_DIM, (g + 1) * HEAD_DIM) for g in range(GQA_GROUP)]
    if seq <= GQA_STACK_MAX_SEQ:
        rows = slice(0, seq)
        outs = attend([query(q_ref, rows) for q_ref in q_refs])
        for lanes, o in zip(head_lanes, outs):
            o_ref[rows, lanes] = o
    else:
        for lanes, q_ref in zip(head_lanes, q_refs):
            for r in range(seq // GQA_Q_BLOCK):
                rows = slice(r * GQA_Q_BLOCK, (r + 1) * GQA_Q_BLOCK)
                o_ref[rows, lanes] = attend([query(q_ref, rows)])[0]


def _gqa(proj, qnorm_g, knorm_g, layer, *, latent_args=None, kv_prev=None):
    latent = latent_args is not None
    n_batch, seq, row0 = (DEC_BATCH, DEC_SEQ, N_CTX // DEC_SEQ) if latent else (BATCH, SEQ, 0)
    past = PAST_LEN if latent else 0

    def vec():
        return pl.BlockSpec((1, HEAD_DIM), lambda b, h: (0, 0))

    def q_spec(g):
        return pl.BlockSpec((seq, HEAD_DIM), lambda b, h: (row0 + b, COL_GQ + h * GQA_GROUP + g))

    in_specs = [vec(), vec()] + [q_spec(g) for g in range(GQA_GROUP)] + [
        pl.BlockSpec((seq, HEAD_DIM), lambda b, h: (row0 + b, COL_GK + h)),
        pl.BlockSpec((seq, HEAD_DIM), lambda b, h: (row0 + b, COL_GV + h)),
    ]
    args = ([qnorm_g.reshape(1, HEAD_DIM), knorm_g.reshape(1, HEAD_DIM)]
            + [proj] * (GQA_GROUP + 2))
    o_spec = pl.BlockSpec((seq, GQA_GROUP * HEAD_DIM), lambda b, h: (row0 + b, h))
    o_shape = jax.ShapeDtypeStruct((N_ROWS, GQA_W), BF16)
    cache_spec = pl.BlockSpec((None, None, SEQ, HEAD_DIM), lambda b, h: (b, layer, 0, h))
    aliases = {}
    if latent:
        cache_k, cache_v, cos, sin_signed, o_prev = latent_args
        table_spec = pl.BlockSpec((seq, HEAD_DIM), lambda b, h: (0, 0))
        in_specs += [cache_spec, cache_spec, table_spec, table_spec, _ANY]
        args += [cache_k.reshape(DEC_BATCH, DEPTH, PAST_LEN, GQA_KV_W),
                 cache_v.reshape(DEC_BATCH, DEPTH, PAST_LEN, GQA_KV_W), cos, sin_signed, o_prev]
        aliases = {len(args) - 1: 0}
        out_shape, out_specs = o_shape, o_spec
    else:
        if kv_prev is not None:
            in_specs += [_ANY, _ANY]
            args += list(kv_prev)
            aliases = {len(args) - 2: 1, len(args) - 1: 2}
        new_shape = jax.ShapeDtypeStruct((BATCH, DEPTH, SEQ, GQA_KV_W), F32)
        out_shape = (o_shape, new_shape, new_shape)
        out_specs = (o_spec, cache_spec, cache_spec)
    return pl.pallas_call(
        functools.partial(_gqa_kernel, seq=seq, latent=latent, n_alias=len(aliases)),
        out_shape=out_shape,
        grid=(n_batch, GQA_KV_HEADS),
        in_specs=in_specs,
        out_specs=out_specs,
        scratch_shapes=[pltpu.VMEM((past + seq, HEAD_DIM), BF16),
                        pltpu.VMEM((past + seq, 2 * HEAD_DIM), BF16)],
        input_output_aliases=aliases,
        compiler_params=_params("parallel", "parallel"),
        name=f"gqa_l{layer}_{'lat' if latent else 'ctx'}",
    )(*args)


def _diff_kernel(*refs, seq, latent, lam_init, n_alias, heads):
    qg_ref, kg_ref, sg_ref, lam_ref, q_ref, k_ref, v_ref = refs[:7]
    if latent:
        ck_ref, cv_ref, cos_ref, sin_ref = refs[7:11]
        o_ref, kt_ref, vt_ref = refs[11 + n_alias:]
    else:
        o_ref, kn_ref, vo_ref, kt_ref, vt_ref = refs[7 + n_alias:]
    past = PAST_LEN if latent else 0
    quarter = DIFF_QK_DIM // 4
    head_lanes = [slice(hd * HEAD_DIM, (hd + 1) * HEAD_DIM) for hd in range(heads)]

    for hd, lanes in enumerate(head_lanes):
        kn = _rms_halves(k_ref[:, lanes], kg_ref[...])
        v = v_ref[:, lanes]
        if latent:
            kn = _rope(kn, cos_ref[...], sin_ref[...], quarter)
            kt_ref[0:past, lanes] = ck_ref[:, lanes].astype(BF16)
            vt_ref[hd, 0:past, 0:HEAD_DIM] = cv_ref[:, lanes].astype(BF16)
        else:
            kn_ref[:, lanes] = kn
            vo_ref[:, lanes] = v
        kt_ref[past:past + seq, lanes] = kn.astype(BF16)
        vt_ref[hd, past:past + seq, 0:HEAD_DIM] = v.astype(BF16)
        vt_ref[hd, :, HEAD_DIM:] = jnp.ones((past + seq, HEAD_DIM), BF16)

    lv = lam_ref[...]
    lam = (jnp.exp(jnp.sum(lv[0:1, :] * lv[1:2, :], axis=-1, keepdims=True))
           - jnp.exp(jnp.sum(lv[2:3, :] * lv[3:4, :], axis=-1, keepdims=True)) + lam_init)
    q_scale = DIFF_QK_DIM ** -0.5 * LOG2E

    def attend(q_masked, kt, vt):
        s = _dot_nt(q_masked.astype(BF16), kt)
        p = jnp.exp2(s - jnp.max(s, axis=-1, keepdims=True))
        ov = _dot(p.astype(BF16), vt)
        return ov[:, :HEAD_DIM] / ov[:, HEAD_DIM:]

    q_block = min(seq, Q_BLOCK)
    for hd, lanes in enumerate(head_lanes):
        for r in range(seq // q_block):
            rows = slice(r * q_block, (r + 1) * q_block)
            qn = _rms_halves(q_ref[rows, lanes], qg_ref[...])
            if latent:
                qn = _rope(qn, cos_ref[rows, :], sin_ref[rows, :], quarter)
            qn = qn * q_scale
            lo = lax.broadcasted_iota(jnp.int32, qn.shape, 1) < DIFF_QK_DIM
            kt, vt = kt_ref[:, lanes], vt_ref[hd]
            o = attend(jnp.where(lo, qn, 0.0), kt, vt) - lam * attend(jnp.where(lo, 0.0, qn), kt, vt)
            o = _rms(o, sg_ref[...]) * (1.0 - lam_init)
            o_ref[rows, lanes] = o.astype(o_ref.dtype)


def _diff(proj, qnorm_g, knorm_g, subln_g, lam_params, layer, *, latent_args=None, kv_prev=None):
    latent = latent_args is not None
    n_batch, seq, row0 = (DEC_BATCH, DEC_SEQ, N_CTX // DEC_SEQ) if latent else (BATCH, SEQ, 0)
    past = PAST_LEN if latent else 0
    lam_init = 0.8 - 0.6 * math.exp(-0.3 * layer)
    heads = DIFF_HP_LAT if latent else DIFF_HP_CTX
    width = heads * HEAD_DIM

    def vec():
        return pl.BlockSpec((1, HEAD_DIM), lambda b, h: (0, 0))

    def col(block):
        return pl.BlockSpec((seq, width), lambda b, h: (row0 + b, block // heads + h))

    in_specs = [
        vec(), vec(), vec(),
        pl.BlockSpec((4, DIFF_QK_DIM), lambda b, h: (0, 0)),
        col(COL_DQ), col(COL_DK), col(COL_DV),
    ]
    args = [jnp.tile(qnorm_g, 2).reshape(1, HEAD_DIM), jnp.tile(knorm_g, 2).reshape(1, HEAD_DIM),
            subln_g.reshape(1, HEAD_DIM), lam_params, proj, proj, proj]
    o_spec = pl.BlockSpec((seq, width), lambda b, h: (row0 + b, h))
    o_shape = jax.ShapeDtypeStruct((N_ROWS, DIFF_W), BF16)
    cache_spec = pl.BlockSpec((None, None, SEQ, width), lambda b, h: (b, layer, 0, h))
    aliases = {}
    if latent:
        cache_k, cache_v, cos, sin_signed, o_prev = latent_args
        table_spec = pl.BlockSpec((seq, HEAD_DIM), lambda b, h: (0, 0))
        in_specs += [cache_spec, cache_spec, table_spec, table_spec, _ANY]
        args += [cache_k.reshape(DEC_BATCH, DEPTH, PAST_LEN, DIFF_W),
                 cache_v.reshape(DEC_BATCH, DEPTH, PAST_LEN, DIFF_W), cos, sin_signed, o_prev]
        aliases = {len(args) - 1: 0}
        out_shape, out_specs = o_shape, o_spec
    else:
        if kv_prev is not None:
            in_specs += [_ANY, _ANY]
            args += list(kv_prev)
            aliases = {len(args) - 2: 1, len(args) - 1: 2}
        new_shape = jax.ShapeDtypeStruct((BATCH, DEPTH, SEQ, DIFF_W), F32)
        out_shape = (o_shape, new_shape, new_shape)
        out_specs = (o_spec, cache_spec, cache_spec)
    return pl.pallas_call(
        functools.partial(_diff_kernel, seq=seq, latent=latent, lam_init=lam_init,
                          n_alias=len(aliases), heads=heads),
        out_shape=out_shape,
        grid=(n_batch, DIFF_HEADS // heads),
        in_specs=in_specs,
        out_specs=out_specs,
        scratch_shapes=[pltpu.VMEM((past + seq, width), BF16),
                        pltpu.VMEM((heads, past + seq, 2 * HEAD_DIM), BF16)],
        input_output_aliases=aliases,
        compiler_params=_params("parallel", "parallel"),
        name=f"diff_l{layer}_{'lat' if latent else 'ctx'}",
    )(*args)


def _rope_tables(n_tokens, dim, repeat):
    quarter = dim // 4
    t = jnp.arange(n_tokens)
    pos = jnp.stack([t // GRID_W, t % GRID_W], axis=-1).astype(F32)
    inv = ROPE_BASE ** (-jnp.arange(quarter, dtype=F32) / quarter)
    ang = pos[:, :, None] * inv
    cos, sin = jnp.cos(ang), jnp.sin(ang)
    cos_l = jnp.concatenate([cos, cos], axis=-1).reshape(n_tokens, dim)
    sin_l = jnp.concatenate([-sin, sin], axis=-1).reshape(n_tokens, dim)
    return jnp.tile(cos_l, (1, repeat)), jnp.tile(sin_l, (1, repeat))


def kernel(x_prompt, x_sample, c, cache_gqa_k, cache_gqa_v, cache_diff_k, cache_diff_v, state_hgrn,
           c_ctx, w_mod, b_mod, norm_g, ffn_w_gate, ffn_w_up, ffn_w_down, w_in, w_out, hgrn_lb_raw,
           hgrn_onorm_g, gqa_qnorm_g, gqa_knorm_g, diff_qnorm_g, diff_knorm_g, diff_lambda,
           diff_subln_g):
    cond = jnp.concatenate(
        [c_ctx[None, :], c, jnp.zeros((COND_PAD - N_COND, D_MODEL), F32)], axis=0)
    mod = _modulation(cond, w_mod, b_mod)

    cos_g, sin_g = _rope_tables(DEC_SEQ, HEAD_DIM, 1)
    cos_d, sin_d = _rope_tables(DEC_SEQ, DIFF_QK_DIM, 2)
    cache_dk = cache_diff_k.reshape(DEC_BATCH, DEPTH, PAST_LEN, DIFF_HEADS, 2 * DIFF_QK_DIM)
    ctx_rows = dict(n_tiles=CTX_TILES, tile0=0)
    lat_rows = dict(n_tiles=LAT_TILES, tile0=CTX_TILES)

    x = None
    states = gqa_kv = diff_kv = None
    for l in range(DEPTH):
        ffn_w = (mod[l], norm_g[l], ffn_w_gate, ffn_w_up, ffn_w_down, l)
        if l == 0:
            x = _ffn(x_prompt.reshape(N_CTX, D_MODEL), *ffn_w, 0, **ctx_rows)
            x = _ffn(x_sample.reshape(N_LAT, D_MODEL), *ffn_w, 0, **lat_rows,
                     out_off=CTX_TILES, o_prev=x)
        else:
            x = _ffn(x, *ffn_w, 0)
        proj = _proj_in(x, mod[l], norm_g[l], w_in, l)

        o_h, states = _hgrn(proj, hgrn_lb_raw, hgrn_onorm_g[l], l, s_prev=states)
        o_h = _hgrn(proj, hgrn_lb_raw, hgrn_onorm_g[l], l, state0=state_hgrn, o_prev=o_h)

        gqa_w = (proj, gqa_qnorm_g[l], gqa_knorm_g[l], l)
        o_g, *gqa_kv = _gqa(*gqa_w, kv_prev=gqa_kv)
        o_g = _gqa(*gqa_w, latent_args=(cache_gqa_k, cache_gqa_v, cos_g, sin_g, o_g))

        diff_w = (proj, diff_qnorm_g[l], diff_knorm_g[l], diff_subln_g[l], diff_lambda[l], l)
        o_d, *diff_kv = _diff(*diff_w, kv_prev=diff_kv)
        o_d = _diff(*diff_w, latent_args=(cache_dk, cache_diff_v, cos_d, sin_d, o_d))

        x = _proj_out(x, mod[l], o_h, o_g, o_d, w_out, l)
        if l < DEPTH - 1:
            x = _ffn(x, *ffn_w, 1)
        else:
            y_prompt = _ffn(x, *ffn_w, 1, **ctx_rows, out_rows=N_CTX)
            y_sample = _ffn(x, *ffn_w, 1, **lat_rows, in_off=CTX_TILES, out_rows=N_LAT)

    return (y_prompt.reshape(BATCH, SEQ, D_MODEL), y_sample.reshape(DEC_BATCH, DEC_SEQ, D_MODEL),
            gqa_kv[0].reshape(BATCH, DEPTH, SEQ, GQA_KV_HEADS, HEAD_DIM),
            gqa_kv[1].reshape(BATCH, DEPTH, SEQ, GQA_KV_HEADS, HEAD_DIM),
            diff_kv[0].reshape(BATCH, DEPTH, SEQ, DIFF_HEADS, 2, DIFF_QK_DIM),
            diff_kv[1].reshape(BATCH, DEPTH, SEQ, DIFF_HEADS, HEAD_DIM),
            states)
```

```python
import functools
import math

import jax
import jax.numpy as jnp
from jax import lax
from jax.experimental import pallas as pl
from jax.experimental.pallas import tpu as pltpu

F32 = jnp.float32
BF16 = jnp.bfloat16

D_MODEL = 2048
BATCH = 16
SEQ = 256
DEPTH = 2
DEC_BATCH = 2
DEC_SEQ = 1024
PAST_LEN = 256
GRID_W = 64
HEAD_DIM = 128
HGRN_HEADS = 4
GQA_Q_HEADS = 6
GQA_KV_HEADS = 2
GQA_GROUP = GQA_Q_HEADS // GQA_KV_HEADS
DIFF_HEADS = 6
DIFF_QK_DIM = 64
FFN_DIM = 5632
N_MOD = 9
IN_WIDTH = 6144
ROPE_BASE = 10000.0
EPS = 1e-6
LOG2E = math.log2(math.e)

N_CTX = BATCH * SEQ
N_LAT = DEC_BATCH * DEC_SEQ
N_ROWS = N_CTX + N_LAT
N_COND = 1 + DEC_BATCH
COND_PAD = 8

COL_HQ, COL_HI, COL_HG, COL_HFF, COL_HFB = 0, 4, 8, 12, 16
COL_GQ, COL_GK, COL_GV = 20, 26, 28
COL_DQ, COL_DK, COL_DV = 30, 36, 42

HGRN_W = HGRN_HEADS * HEAD_DIM
GQA_W = GQA_Q_HEADS * HEAD_DIM
GQA_KV_W = GQA_KV_HEADS * HEAD_DIM
DIFF_W = DIFF_HEADS * HEAD_DIM

VMEM_LIMIT = 60 * 1024 * 1024

TM = 1024
ROW_CHUNK = 128
FFN_TF = 256
FFN_TILES_PER_STEP = 2
FFN_NB = 512
IN_TN = 512
IN_TILES_PER_STEP = 2
OUT_TN = 512
OUT_TILES_PER_STEP = 2
MOD_TN = 2048
HGRN_CHUNK = 64
HGRN_SUB = 8
HGRN_HP = 4
HGRN_UNROLL_MAX_CHUNKS = 4
Q_BLOCK = 256
GQA_Q_BLOCK = 256
GQA_STACK_MAX_SEQ = 256
DIFF_HP_CTX = 6
DIFF_HP_LAT = 2

CTX_TILES = N_CTX // TM
LAT_TILES = N_LAT // TM
ALL_TILES = CTX_TILES + LAT_TILES


def _cond_of_tile(i):
    tiles_per_latent = DEC_SEQ // TM
    return jnp.where(i < CTX_TILES, 0, 1 + (i - CTX_TILES) // tiles_per_latent)


def _silu(x):
    return x * jax.nn.sigmoid(x)


def _dot(a, b):
    return jnp.dot(a, b, preferred_element_type=F32)


def _dot_nt(a, b):
    return lax.dot_general(a, b, (((1,), (1,)), ((), ())), preferred_element_type=F32)


def _dot_tn(a, b):
    return lax.dot_general(a, b, (((0,), (0,)), ((), ())), preferred_element_type=F32)


def _rms(x, g):
    ms = jnp.mean(x * x, axis=-1, keepdims=True)
    return (x * lax.rsqrt(ms + EPS)) * g


def _params(*semantics):
    return pltpu.CompilerParams(dimension_semantics=semantics, vmem_limit_bytes=VMEM_LIMIT)


_ANY = pl.BlockSpec(memory_space=pl.ANY)


def _mod_kernel(cond_ref, w_ref, b_ref, o_ref):
    a = _silu(cond_ref[...]).astype(BF16)
    o_ref[...] = _dot(a, w_ref[...].astype(BF16)) + b_ref[...]


def _modulation(cond, w_mod, b_mod):
    width = N_MOD * D_MODEL
    out = pl.pallas_call(
        _mod_kernel,
        out_shape=jax.ShapeDtypeStruct((DEPTH, COND_PAD, width), F32),
        grid=(DEPTH, width // MOD_TN),
        in_specs=[
            pl.BlockSpec((COND_PAD, D_MODEL), lambda l, j: (0, 0)),
            pl.BlockSpec((None, D_MODEL, MOD_TN), lambda l, j: (l, 0, j)),
            pl.BlockSpec((None, 1, MOD_TN), lambda l, j: (l, 0, j)),
        ],
        out_specs=pl.BlockSpec((None, COND_PAD, MOD_TN), lambda l, j: (l, 0, j)),
        compiler_params=_params("parallel", "parallel"),
        name="modulation",
    )(cond, w_mod, b_mod.reshape(DEPTH, 1, width))
    return out[:, :N_COND].reshape(DEPTH, N_COND, N_MOD, D_MODEL)


def _mod_norm_into(x_ref, mod_ref, g_ref, h_ref, sub, row0=0):
    shift = mod_ref[3 * sub:3 * sub + 1, :]
    gain = g_ref[sub:sub + 1, :] * (1.0 + mod_ref[3 * sub + 1:3 * sub + 2, :])

    def body(r, carry):
        rows = pl.ds(pl.multiple_of(row0 + r * ROW_CHUNK, ROW_CHUNK), ROW_CHUNK)
        h_ref[rows, :] = (_rms(x_ref[rows, :], gain) + shift).astype(BF16)
        return carry

    lax.fori_loop(0, TM // ROW_CHUNK, body, 0)


def _ffn_kernel(x_ref, mod_ref, g_ref, wg_ref, wu_ref, wd_ref, *rest, sub, tile0):
    o_ref, h_ref = rest[-2:]
    j = pl.program_id(1)
    conds = [_cond_of_tile(tile0 + pl.program_id(0) * FFN_TILES_PER_STEP + part)
             for part in range(FFN_TILES_PER_STEP)]

    def hidden_step(first):
        for part in range(FFN_TILES_PER_STEP):
            rows = slice(part * TM, (part + 1) * TM)
            h = h_ref[rows, :]
            gate_act = _dot(h, wg_ref[...].astype(BF16))
            up = _dot(h, wu_ref[...].astype(BF16))
            a = (_silu(gate_act) * up).astype(BF16)
            for n in range(0, D_MODEL, FFN_NB):
                contrib = _dot(a, wd_ref[:, n:n + FFN_NB].astype(BF16))
                if first:
                    o_ref[rows, n:n + FFN_NB] = contrib
                else:
                    o_ref[rows, n:n + FFN_NB] += contrib

    @pl.when(j == 0)
    def _():
        for part, cond in enumerate(conds):
            _mod_norm_into(x_ref, mod_ref.at[cond], g_ref, h_ref, sub, row0=part * TM)
        hidden_step(True)

    @pl.when(j > 0)
    def _():
        hidden_step(False)

    @pl.when(j == pl.num_programs(1) - 1)
    def _():
        for part, cond in enumerate(conds):
            gate = mod_ref[cond, 3 * sub + 2:3 * sub + 3, :]

            def body(r, carry):
                rows = pl.ds(pl.multiple_of(part * TM + r * ROW_CHUNK, ROW_CHUNK), ROW_CHUNK)
                o_ref[rows, :] = x_ref[rows, :] + gate * (0.5 * o_ref[rows, :])
                return carry

            lax.fori_loop(0, TM // ROW_CHUNK, body, 0)


def _ffn(x, mod_l, norm_g_l, w_gate, w_up, w_down, layer, which, *, n_tiles=ALL_TILES, in_off=0,
         out_off=0, tile0=0, out_rows=N_ROWS, o_prev=None):
    sub = 2 * which
    per = FFN_TILES_PER_STEP
    rows = per * TM
    assert n_tiles % per == 0 and in_off % per == 0 and out_off % per == 0
    in_specs = [
        pl.BlockSpec((rows, D_MODEL), lambda i, j: (in_off // per + i, 0), pipeline_mode=pl.Buffered(1)),
        pl.BlockSpec((N_COND, N_MOD, D_MODEL), lambda i, j: (0, 0, 0)),
        pl.BlockSpec((3, D_MODEL), lambda i, j: (0, 0)),
        pl.BlockSpec((None, None, D_MODEL, FFN_TF), lambda i, j: (layer, which, 0, j)),
        pl.BlockSpec((None, None, D_MODEL, FFN_TF), lambda i, j: (layer, which, 0, j)),
        pl.BlockSpec((None, None, FFN_TF, D_MODEL), lambda i, j: (layer, which, j, 0)),
    ]
    args = [x, mod_l, norm_g_l, w_gate, w_up, w_down]
    aliases = {}
    if o_prev is not None:
        in_specs.append(_ANY)
        args.append(o_prev)
        aliases = {len(args) - 1: 0}
    return pl.pallas_call(
        functools.partial(_ffn_kernel, sub=sub, tile0=tile0),
        out_shape=jax.ShapeDtypeStruct((out_rows, D_MODEL), F32),
        grid=(n_tiles // per, FFN_DIM // FFN_TF),
        in_specs=in_specs,
        out_specs=pl.BlockSpec((rows, D_MODEL), lambda i, j: (out_off // per + i, 0),
                               pipeline_mode=pl.Buffered(1)),
        scratch_shapes=[pltpu.VMEM((rows, D_MODEL), BF16)],
        input_output_aliases=aliases,
        compiler_params=_params("parallel", "arbitrary"),
        name=f"ffn_l{layer}_h{which}_t{tile0}n{n_tiles}",
    )(*args)


def _proj_in_kernel(x_ref, mod_ref, g_ref, w_ref, o_ref, h_ref):
    @pl.when(pl.program_id(1) == 0)
    def _():
        for part in range(IN_TILES_PER_STEP):
            cond = _cond_of_tile(pl.program_id(0) * IN_TILES_PER_STEP + part)
            _mod_norm_into(x_ref, mod_ref.at[cond], g_ref, h_ref, 1, row0=part * TM)

    w = w_ref[...].astype(BF16)
    for part in range(IN_TILES_PER_STEP):
        rows = slice(part * TM, (part + 1) * TM)
        o_ref[rows, :] = _dot(h_ref[rows, :], w)


def _proj_in(x, mod_l, norm_g_l, w_in, layer):
    rows = IN_TILES_PER_STEP * TM
    return pl.pallas_call(
        _proj_in_kernel,
        out_shape=jax.ShapeDtypeStruct((N_ROWS, IN_WIDTH), F32),
        grid=(N_ROWS // rows, IN_WIDTH // IN_TN),
        in_specs=[
            pl.BlockSpec((rows, D_MODEL), lambda i, j: (i, 0)),
            pl.BlockSpec((N_COND, N_MOD, D_MODEL), lambda i, j: (0, 0, 0)),
            pl.BlockSpec((3, D_MODEL), lambda i, j: (0, 0)),
            pl.BlockSpec((None, D_MODEL, IN_TN), lambda i, j: (layer, 0, j)),
        ],
        out_specs=pl.BlockSpec((rows, IN_TN), lambda i, j: (i, j)),
        scratch_shapes=[pltpu.VMEM((rows, D_MODEL), BF16)],
        compiler_params=_params("parallel", "arbitrary"),
        name=f"proj_in_l{layer}",
    )(x, mod_l, norm_g_l, w_in)


def _proj_out_kernel(x_ref, mod_ref, oh_ref, og_ref, od_ref, w_ref, o_ref):
    w_h = w_ref[0:HGRN_W, :].astype(BF16)
    w_g = w_ref[HGRN_W:HGRN_W + GQA_W, :].astype(BF16)
    w_d = w_ref[HGRN_W + GQA_W:, :].astype(BF16)
    for part in range(OUT_TILES_PER_STEP):
        rows = slice(part * TM, (part + 1) * TM)
        cond = _cond_of_tile(pl.program_id(0) * OUT_TILES_PER_STEP + part)
        m = _dot(oh_ref[rows, :], w_h) + _dot(og_ref[rows, :], w_g) + _dot(od_ref[rows, :], w_d)
        o_ref[rows, :] = x_ref[rows, :] + mod_ref[cond, 5:6, :] * m


def _proj_out(x, mod_l, o_h, o_g, o_d, w_out, layer):
    rows = OUT_TILES_PER_STEP * TM
    return pl.pallas_call(
        _proj_out_kernel,
        out_shape=jax.ShapeDtypeStruct((N_ROWS, D_MODEL), F32),
        grid=(N_ROWS // rows, D_MODEL // OUT_TN),
        in_specs=[
            pl.BlockSpec((rows, OUT_TN), lambda i, j: (i, j)),
            pl.BlockSpec((N_COND, N_MOD, OUT_TN), lambda i, j: (0, 0, j)),
            pl.BlockSpec((rows, HGRN_W), lambda i, j: (i, 0)),
            pl.BlockSpec((rows, GQA_W), lambda i, j: (i, 0)),
            pl.BlockSpec((rows, DIFF_W), lambda i, j: (i, 0)),
            pl.BlockSpec((None, D_MODEL, OUT_TN), lambda i, j: (layer, 0, j)),
        ],
        out_specs=pl.BlockSpec((rows, OUT_TN), lambda i, j: (i, j)),
        compiler_params=_params("parallel", "parallel"),
        name=f"proj_out_l{layer}",
    )(x, mod_l, o_h, o_g, o_d, w_out)


def _log2_forget_and_key(z, lb):
    z2 = z * LOG2E
    soft = jnp.log2(1.0 + jnp.exp2(-jnp.abs(z2)))
    log_1mlb = jnp.log1p(-lb) * LOG2E
    a = jnp.log2(lb)
    c = log_1mlb + (jnp.minimum(z2, 0.0) - soft)
    log_f = jnp.maximum(a, c) + jnp.log2(1.0 + jnp.exp2(-jnp.abs(a - c)))
    log_k = log_1mlb + (jnp.minimum(-z2, 0.0) - soft)
    return log_f, log_k


def _cumsum_rows(x, reverse):
    tile = HGRN_SUB
    row = lax.broadcasted_iota(jnp.int32, (tile, 1), 0)
    tiles = []
    for j in range(x.shape[0] // tile):
        y = x[j * tile:(j + 1) * tile, :]
        for sh in (1, 2, 4):
            if reverse:
                y = y + jnp.where(row < tile - sh, pltpu.roll(y, tile - sh, 0), 0.0)
            else:
                y = y + jnp.where(row >= sh, pltpu.roll(y, sh, 0), 0.0)
        tiles.append(y)
    order = range(len(tiles) - 1, -1, -1) if reverse else range(len(tiles))
    carry = None
    for j in order:
        if carry is not None:
            tiles[j] = tiles[j] + carry
        carry = tiles[j][0:1, :] if reverse else tiles[j][tile - 1:tile, :]
    return jnp.concatenate(tiles, axis=0)


def _hgrn_prepare(q, z, lb, reverse):
    lf2, lk2 = _log2_forget_and_key(z, lb)
    b = _cumsum_rows(lf2, reverse)
    return _silu(q), b, b - lk2


def _hgrn_chunk(qs, b, c, v, st, reverse):
    C, SUB = HGRN_CHUNK, HGRN_SUB
    n_sub = C // SUB
    b_tot = b[0:1, :] if reverse else b[C - 1:C, :]

    o = _dot_nt((qs * jnp.exp2(b)).astype(BF16), st.astype(BF16))

    terms = []
    for i in range(n_sub):
        lo = i * SUB
        qi, bi, ci = qs[lo:lo + SUB, :], b[lo:lo + SUB, :], c[lo:lo + SUB, :]
        for s in range(SUB):
            terms.append(qi * jnp.exp2(bi - ci[s:s + 1, :]))
    k_sums = _dot(jnp.concatenate(terms, axis=0).astype(BF16), jnp.ones((HEAD_DIM, C), BF16))

    row = lax.broadcasted_iota(jnp.int32, (SUB, C), 0)
    lane = lax.broadcasted_iota(jnp.int32, (SUB, C), 1)
    lane_s = lane % SUB
    causal = (lane_s >= row) if reverse else (lane_s <= row)
    blocks = []
    for i in range(n_sub):
        lo, hi = i * SUB, (i + 1) * SUB
        diag = k_sums[lo * SUB:(lo + 1) * SUB, :]
        for s in range(1, SUB):
            diag = jnp.where(lane_s == s, k_sums[(lo + s) * SUB:(lo + s + 1) * SUB, :], diag)
        has_off = (i < n_sub - 1) if reverse else (i > 0)
        if has_off:
            ref = b[hi:hi + 1, :] if reverse else b[lo - 1:lo, :]
            qt = (qs[lo:hi, :] * jnp.exp2(b[lo:hi, :] - ref)).astype(BF16)
            if reverse:
                kt = jnp.concatenate([jnp.zeros((hi, HEAD_DIM), F32), jnp.exp2(ref - c[hi:, :])], axis=0)
            else:
                kt = jnp.concatenate([jnp.exp2(ref - c[:lo, :]), jnp.zeros((C - lo, HEAD_DIM), F32)], axis=0)
            off = _dot_nt(qt, kt.astype(BF16))
        else:
            off = jnp.zeros((SUB, C), F32)
        blocks.append(jnp.where((lane // SUB == i) & causal, diag, off))
    a = jnp.concatenate(blocks, axis=0)
    o = o + _dot(a.astype(BF16), v.astype(BF16))

    st_new = jnp.exp2(b_tot) * st + _dot_tn(v.astype(BF16), jnp.exp2(b_tot - c).astype(BF16))
    return o, st_new


def _hgrn_kernel(*refs, layer, seq, latent, n_alias):
    raw_ref, q_ref, v_ref, g_ref, ff_ref, fb_ref, ong_ref = refs[:7]
    s0_ref = refs[7] if latent else None
    outs = refs[7 + (1 if latent else 0) + n_alias:]
    if latent:
        o_ref, of_ref, ob_ref, st_ref, pre_ref = outs
        s_ref = None
    else:
        o_ref, s_ref, of_ref, ob_ref, st_ref, pre_ref = outs
    C = HGRN_CHUNK
    n_chunks = seq // C

    def lower_bound(d, lanes):
        rows = [raw_ref[2 * l + d:2 * l + d + 1, lanes] for l in range(DEPTH)]
        m = functools.reduce(jnp.maximum, rows)
        e = [jnp.exp(r - m) for r in rows]
        tot = functools.reduce(lambda x, y: x + y, e)
        lb = jnp.zeros_like(m)
        for l in range(1, layer + 1):
            lb = lb + e[l] / tot
        return lb

    head_lanes = [slice(hp * HEAD_DIM, (hp + 1) * HEAD_DIM) for hp in range(HGRN_HP)]
    lbs = [[lower_bound(d, lanes) for d in range(2)] for lanes in head_lanes]

    for hp in range(HGRN_HP):
        for d in range(2):
            st_ref[hp, d] = s0_ref[d, hp].T if latent else jnp.zeros((HEAD_DIM, HEAD_DIM), F32)

    def chunk_rows(ci):
        if isinstance(ci, int):
            return pl.ds(ci * C, C), pl.ds((n_chunks - 1 - ci) * C, C)
        return (pl.ds(pl.multiple_of(ci * C, C), C),
                pl.ds(pl.multiple_of((n_chunks - 1 - ci) * C, C), C))

    def prepare(ci, slot):
        rows = chunk_rows(ci)
        for hp, lanes in enumerate(head_lanes):
            for d, f_ref in enumerate((ff_ref, fb_ref)):
                pre = _hgrn_prepare(q_ref[rows[d], lanes], f_ref[rows[d], lanes], lbs[hp][d], d == 1)
                for n, val in enumerate(pre):
                    pre_ref[slot, 2 * hp + d, n] = val

    prepare(0, 0)

    def consume(ci, slot):
        rows = chunk_rows(ci)
        for hp, lanes in enumerate(head_lanes):
            for d, acc_ref in enumerate((of_ref, ob_ref)):
                qs, b, c = (pre_ref[slot, 2 * hp + d, n] for n in range(3))
                o, st = _hgrn_chunk(qs, b, c, v_ref[rows[d], lanes], st_ref[hp, d], d == 1)
                acc_ref[rows[d], lanes] = o
                st_ref[hp, d] = st

    def step(ci, slot):
        consume(ci, slot)
        prepare(ci + 1, 1 - slot)

    n_steps = n_chunks - 1
    if n_chunks <= HGRN_UNROLL_MAX_CHUNKS:
        for ci in range(n_steps):
            step(ci, ci % 2)
    else:
        peeled = n_steps % 2
        for ci in range(peeled):
            step(ci, ci % 2)

        def pair(p, carry):
            ci = peeled + 2 * p
            step(ci, peeled % 2)
            step(ci + 1, 1 - peeled % 2)
            return carry

        lax.fori_loop(0, (n_steps - peeled) // 2, pair, 0)
    consume(n_chunks - 1, (n_chunks - 1) % 2)

    for hp, lanes in enumerate(head_lanes):
        if s_ref is not None:
            for d in range(2):
                s_ref[d, hp] = st_ref[hp, d].T
        o = of_ref[:, lanes] + ob_ref[:, lanes]
        o_ref[:, lanes] = (_rms(o, ong_ref[...]) * _silu(g_ref[:, lanes])).astype(o_ref.dtype)


def _hgrn(proj, lb_raw, onorm_g_l, layer, *, state0=None, o_prev=None, s_prev=None):
    latent = state0 is not None
    n_batch, seq, row0 = (DEC_BATCH, DEC_SEQ, N_CTX // DEC_SEQ) if latent else (BATCH, SEQ, 0)
    width = HGRN_HP * HEAD_DIM

    def col(block):
        return pl.BlockSpec((seq, width), lambda b, h: (row0 + b, block // HGRN_HP + h))

    in_specs = [
        pl.BlockSpec((2 * DEPTH, width), lambda b, h: (0, h)),
        col(COL_HQ), col(COL_HI), col(COL_HG), col(COL_HFF), col(COL_HFB),
        pl.BlockSpec((1, HEAD_DIM), lambda b, h: (0, 0)),
    ]
    args = [lb_raw.reshape(2 * DEPTH, HGRN_W), proj, proj, proj, proj, proj,
            onorm_g_l.reshape(1, HEAD_DIM)]
    o_shape = jax.ShapeDtypeStruct((N_ROWS, HGRN_W), BF16)
    o_spec = pl.BlockSpec((seq, width), lambda b, h: (row0 + b, h))
    aliases = {}
    if latent:
        in_specs += [pl.BlockSpec((None, None, 2, HGRN_HP, HEAD_DIM, HEAD_DIM),
                                  lambda b, h: (b, layer, 0, h, 0, 0)), _ANY]
        args += [state0, o_prev]
        aliases = {len(args) - 1: 0}
        out_shape, out_specs = o_shape, o_spec
    else:
        if s_prev is not None:
            in_specs.append(_ANY)
            args.append(s_prev)
            aliases = {len(args) - 1: 1}
        out_shape = (o_shape, jax.ShapeDtypeStruct(
            (BATCH, DEPTH, 2, HGRN_HEADS, HEAD_DIM, HEAD_DIM), F32))
        out_specs = (o_spec, pl.BlockSpec((None, None, 2, HGRN_HP, HEAD_DIM, HEAD_DIM),
                                          lambda b, h: (b, layer, 0, h, 0, 0)))
    return pl.pallas_call(
        functools.partial(_hgrn_kernel, layer=layer, seq=seq, latent=latent, n_alias=len(aliases)),
        out_shape=out_shape,
        grid=(n_batch, HGRN_HEADS // HGRN_HP),
        in_specs=in_specs,
        out_specs=out_specs,
        scratch_shapes=[pltpu.VMEM((seq, width), F32), pltpu.VMEM((seq, width), F32),
                        pltpu.VMEM((HGRN_HP, 2, HEAD_DIM, HEAD_DIM), F32),
                        pltpu.VMEM((2, 2 * HGRN_HP, 3, HGRN_CHUNK, HEAD_DIM), F32)],
        input_output_aliases=aliases,
        compiler_params=_params("parallel", "parallel"),
        name=f"hgrn_l{layer}_{'lat' if latent else 'ctx'}",
    )(*args)


def _swap_pairs(x, width):
    lanes = x.shape[-1]
    lane = lax.broadcasted_iota(jnp.int32, x.shape, x.ndim - 1)
    from_right = pltpu.roll(x, lanes - width, x.ndim - 1)
    from_left = pltpu.roll(x, width, x.ndim - 1)
    return jnp.where(lane % (2 * width) < width, from_right, from_left)


def _rope(x, cos, sin_signed, quarter):
    return x * cos + _swap_pairs(x, quarter) * sin_signed


def _rms_halves(x, g):
    half = x.shape[-1] // 2
    lane = lax.broadcasted_iota(jnp.int32, x.shape, x.ndim - 1)
    lo = lane < half
    sq = x * x
    ms_lo = jnp.sum(jnp.where(lo, sq, 0.0), axis=-1, keepdims=True) / half
    ms_hi = jnp.sum(jnp.where(lo, 0.0, sq), axis=-1, keepdims=True) / half
    ms = jnp.where(lo, ms_lo, ms_hi)
    return (x * lax.rsqrt(ms + EPS)) * g


def _gqa_kernel(*refs, seq, latent, n_alias):
    qg_ref, kg_ref = refs[:2]
    q_refs = refs[2:2 + GQA_GROUP]
    k_ref, v_ref = refs[2 + GQA_GROUP:4 + GQA_GROUP]
    rest = refs[4 + GQA_GROUP:]
    if latent:
        ck_ref, cv_ref, cos_ref, sin_ref = rest[:4]
        o_ref, kt_ref, vt_ref = rest[4 + n_alias:]
    else:
        o_ref, kn_ref, vo_ref, kt_ref, vt_ref = rest[n_alias:]
    past = PAST_LEN if latent else 0
    quarter = HEAD_DIM // 4

    kn = _rms(k_ref[...], kg_ref[...])
    v = v_ref[...]
    if latent:
        kn = _rope(kn, cos_ref[...], sin_ref[...], quarter)
        kt_ref[0:past, :] = ck_ref[...].astype(BF16)
        vt_ref[0:past, 0:HEAD_DIM] = cv_ref[...].astype(BF16)
    else:
        kn_ref[...] = kn
        vo_ref[...] = v
    kt_ref[past:past + seq, :] = kn.astype(BF16)
    vt_ref[past:past + seq, 0:HEAD_DIM] = v.astype(BF16)
    vt_ref[:, HEAD_DIM:] = jnp.ones((past + seq, HEAD_DIM), BF16)

    q_scale = HEAD_DIM ** -0.5 * LOG2E

    def attend(q_rows):
        s = _dot_nt(jnp.concatenate(q_rows, axis=0), kt_ref[...])
        p = jnp.exp2(s - jnp.max(s, axis=-1, keepdims=True))
        ov = _dot(p.astype(BF16), vt_ref[...])
        o = (ov[:, :HEAD_DIM] / ov[:, HEAD_DIM:]).astype(o_ref.dtype)
        n = q_rows[0].shape[0]
        return [o[i * n:(i + 1) * n, :] for i in range(len(q_rows))]

    def query(q_ref, rows):
        qn = _rms(q_ref[rows, :], qg_ref[...])
        if latent:
            qn = _rope(qn, cos_ref[rows, :], sin_ref[rows, :], quarter)
        return (qn * q_scale).astype(BF16)

    head_lanes = [slice(g * HEAD_DIM, (g + 1) * HEAD_DIM) for g in range(GQA_GROUP)]
    if seq <= GQA_STACK_MAX_SEQ:
        rows = slice(0, seq)
        outs = attend([query(q_ref, rows) for q_ref in q_refs])
        for lanes, o in zip(head_lanes, outs):
            o_ref[rows, lanes] = o
    else:
        for lanes, q_ref in zip(head_lanes, q_refs):
            for r in range(seq // GQA_Q_BLOCK):
                rows = slice(r * GQA_Q_BLOCK, (r + 1) * GQA_Q_BLOCK)
                o_ref[rows, lanes] = attend([query(q_ref, rows)])[0]


def _gqa(proj, qnorm_g, knorm_g, layer, *, latent_args=None, kv_prev=None):
    latent = latent_args is not None
    n_batch, seq, row0 = (DEC_BATCH, DEC_SEQ, N_CTX // DEC_SEQ) if latent else (BATCH, SEQ, 0)
    past = PAST_LEN if latent else 0

    def vec():
        return pl.BlockSpec((1, HEAD_DIM), lambda b, h: (0, 0))

    def q_spec(g):
        return pl.BlockSpec((seq, HEAD_DIM), lambda b, h: (row0 + b, COL_GQ + h * GQA_GROUP + g))

    in_specs = [vec(), vec()] + [q_spec(g) for g in range(GQA_GROUP)] + [
        pl.BlockSpec((seq, HEAD_DIM), lambda b, h: (row0 + b, COL_GK + h)),
        pl.BlockSpec((seq, HEAD_DIM), lambda b, h: (row0 + b, COL_GV + h)),
    ]
    args = ([qnorm_g.reshape(1, HEAD_DIM), knorm_g.reshape(1, HEAD_DIM)]
            + [proj] * (GQA_GROUP + 2))
    o_spec = pl.BlockSpec((seq, GQA_GROUP * HEAD_DIM), lambda b, h: (row0 + b, h))
    o_shape = jax.ShapeDtypeStruct((N_ROWS, GQA_W), BF16)
    cache_spec = pl.BlockSpec((None, None, SEQ, HEAD_DIM), lambda b, h: (b, layer, 0, h))
    aliases = {}
    if latent:
        cache_k, cache_v, cos, sin_signed, o_prev = latent_args
        table_spec = pl.BlockSpec((seq, HEAD_DIM), lambda b, h: (0, 0))
        in_specs += [cache_spec, cache_spec, table_spec, table_spec, _ANY]
        args += [cache_k.reshape(DEC_BATCH, DEPTH, PAST_LEN, GQA_KV_W),
                 cache_v.reshape(DEC_BATCH, DEPTH, PAST_LEN, GQA_KV_W), cos, sin_signed, o_prev]
        aliases = {len(args) - 1: 0}
        out_shape, out_specs = o_shape, o_spec
    else:
        if kv_prev is not None:
            in_specs += [_ANY, _ANY]
            args += list(kv_prev)
            aliases = {len(args) - 2: 1, len(args) - 1: 2}
        new_shape = jax.ShapeDtypeStruct((BATCH, DEPTH, SEQ, GQA_KV_W), F32)
        out_shape = (o_shape, new_shape, new_shape)
        out_specs = (o_spec, cache_spec, cache_spec)
    return pl.pallas_call(
        functools.partial(_gqa_kernel, seq=seq, latent=latent, n_alias=len(aliases)),
        out_shape=out_shape,
        grid=(n_batch, GQA_KV_HEADS),
        in_specs=in_specs,
        out_specs=out_specs,
        scratch_shapes=[pltpu.VMEM((past + seq, HEAD_DIM), BF16),
                        pltpu.VMEM((past + seq, 2 * HEAD_DIM), BF16)],
        input_output_aliases=aliases,
        compiler_params=_params("parallel", "parallel"),
        name=f"gqa_l{layer}_{'lat' if latent else 'ctx'}",
    )(*args)


def _diff_kernel(*refs, seq, latent, lam_init, n_alias, heads):
    qg_ref, kg_ref, sg_ref, lam_ref, q_ref, k_ref, v_ref = refs[:7]
    if latent:
        ck_ref, cv_ref, cos_ref, sin_ref = refs[7:11]
        o_ref, kt_ref, vt_ref = refs[11 + n_alias:]
    else:
        o_ref, kn_ref, vo_ref, kt_ref, vt_ref = refs[7 + n_alias:]
    past = PAST_LEN if latent else 0
    quarter = DIFF_QK_DIM // 4
    head_lanes = [slice(hd * HEAD_DIM, (hd + 1) * HEAD_DIM) for hd in range(heads)]

    for hd, lanes in enumerate(head_lanes):
        kn = _rms_halves(k_ref[:, lanes], kg_ref[...])
        v = v_ref[:, lanes]
        if latent:
            kn = _rope(kn, cos_ref[...], sin_ref[...], quarter)
            kt_ref[0:past, lanes] = ck_ref[:, lanes].astype(BF16)
            vt_ref[hd, 0:past, 0:HEAD_DIM] = cv_ref[:, lanes].astype(BF16)
        else:
            kn_ref[:, lanes] = kn
            vo_ref[:, lanes] = v
        kt_ref[past:past + seq, lanes] = kn.astype(BF16)
        vt_ref[hd, past:past + seq, 0:HEAD_DIM] = v.astype(BF16)
        vt_ref[hd, :, HEAD_DIM:] = jnp.ones((past + seq, HEAD_DIM), BF16)

    lv = lam_ref[...]
    lam = (jnp.exp(jnp.sum(lv[0:1, :] * lv[1:2, :], axis=-1, keepdims=True))
           - jnp.exp(jnp.sum(lv[2:3, :] * lv[3:4, :], axis=-1, keepdims=True)) + lam_init)
    q_scale = DIFF_QK_DIM ** -0.5 * LOG2E

    def attend(q_masked, kt, vt):
        s = _dot_nt(q_masked.astype(BF16), kt)
        p = jnp.exp2(s - jnp.max(s, axis=-1, keepdims=True))
        ov = _dot(p.astype(BF16), vt)
        return ov[:, :HEAD_DIM] / ov[:, HEAD_DIM:]

    q_block = min(seq, Q_BLOCK)
    for hd, lanes in enumerate(head_lanes):
        for r in range(seq // q_block):
            rows = slice(r * q_block, (r + 1) * q_block)
            qn = _rms_halves(q_ref[rows, lanes], qg_ref[...])
            if latent:
                qn = _rope(qn, cos_ref[rows, :], sin_ref[rows, :], quarter)
            qn = qn * q_scale
            lo = lax.broadcasted_iota(jnp.int32, qn.shape, 1) < DIFF_QK_DIM
            kt, vt = kt_ref[:, lanes], vt_ref[hd]
            o = attend(jnp.where(lo, qn, 0.0), kt, vt) - lam * attend(jnp.where(lo, 0.0, qn), kt, vt)
            o = _rms(o, sg_ref[...]) * (1.0 - lam_init)
            o_ref[rows, lanes] = o.astype(o_ref.dtype)


def _diff(proj, qnorm_g, knorm_g, subln_g, lam_params, layer, *, latent_args=None, kv_prev=None):
    latent = latent_args is not None
    n_batch, seq, row0 = (DEC_BATCH, DEC_SEQ, N_CTX // DEC_SEQ) if latent else (BATCH, SEQ, 0)
    past = PAST_LEN if latent else 0
    lam_init = 0.8 - 0.6 * math.exp(-0.3 * layer)
    heads = DIFF_HP_LAT if latent else DIFF_HP_CTX
    width = heads * HEAD_DIM

    def vec():
        return pl.BlockSpec((1, HEAD_DIM), lambda b, h: (0, 0))

    def col(block):
        return pl.BlockSpec((seq, width), lambda b, h: (row0 + b, block // heads + h))

    in_specs = [
        vec(), vec(), vec(),
        pl.BlockSpec((4, DIFF_QK_DIM), lambda b, h: (0, 0)),
        col(COL_DQ), col(COL_DK), col(COL_DV),
    ]
    args = [jnp.tile(qnorm_g, 2).reshape(1, HEAD_DIM), jnp.tile(knorm_g, 2).reshape(1, HEAD_DIM),
            subln_g.reshape(1, HEAD_DIM), lam_params, proj, proj, proj]
    o_spec = pl.BlockSpec((seq, width), lambda b, h: (row0 + b, h))
    o_shape = jax.ShapeDtypeStruct((N_ROWS, DIFF_W), BF16)
    cache_spec = pl.BlockSpec((None, None, SEQ, width), lambda b, h: (b, layer, 0, h))
    aliases = {}
    if latent:
        cache_k, cache_v, cos, sin_signed, o_prev = latent_args
        table_spec = pl.BlockSpec((seq, HEAD_DIM), lambda b, h: (0, 0))
        in_specs += [cache_spec, cache_spec, table_spec, table_spec, _ANY]
        args += [cache_k.reshape(DEC_BATCH, DEPTH, PAST_LEN, DIFF_W),
                 cache_v.reshape(DEC_BATCH, DEPTH, PAST_LEN, DIFF_W), cos, sin_signed, o_prev]
        aliases = {len(args) - 1: 0}
        out_shape, out_specs = o_shape, o_spec
    else:
        if kv_prev is not None:
            in_specs += [_ANY, _ANY]
            args += list(kv_prev)
            aliases = {len(args) - 2: 1, len(args) - 1: 2}
        new_shape = jax.ShapeDtypeStruct((BATCH, DEPTH, SEQ, DIFF_W), F32)
        out_shape = (o_shape, new_shape, new_shape)
        out_specs = (o_spec, cache_spec, cache_spec)
    return pl.pallas_call(
        functools.partial(_diff_kernel, seq=seq, latent=latent, lam_init=lam_init,
                          n_alias=len(aliases), heads=heads),
        out_shape=out_shape,
        grid=(n_batch, DIFF_HEADS // heads),
        in_specs=in_specs,
        out_specs=out_specs,
        scratch_shapes=[pltpu.VMEM((past + seq, width), BF16),
                        pltpu.VMEM((heads, past + seq, 2 * HEAD_DIM), BF16)],
        input_output_aliases=aliases,
        compiler_params=_params("parallel", "parallel"),
        name=f"diff_l{layer}_{'lat' if latent else 'ctx'}",
    )(*args)


def _rope_tables(n_tokens, dim, repeat):
    quarter = dim // 4
    t = jnp.arange(n_tokens)
    pos = jnp.stack([t // GRID_W, t % GRID_W], axis=-1).astype(F32)
    inv = ROPE_BASE ** (-jnp.arange(quarter, dtype=F32) / quarter)
    ang = pos[:, :, None] * inv
    cos, sin = jnp.cos(ang), jnp.sin(ang)
    cos_l = jnp.concatenate([cos, cos], axis=-1).reshape(n_tokens, dim)
    sin_l = jnp.concatenate([-sin, sin], axis=-1).reshape(n_tokens, dim)
    return jnp.tile(cos_l, (1, repeat)), jnp.tile(sin_l, (1, repeat))


def kernel(x_prompt, x_sample, c, cache_gqa_k, cache_gqa_v, cache_diff_k, cache_diff_v, state_hgrn,
           c_ctx, w_mod, b_mod, norm_g, ffn_w_gate, ffn_w_up, ffn_w_down, w_in, w_out, hgrn_lb_raw,
           hgrn_onorm_g, gqa_qnorm_g, gqa_knorm_g, diff_qnorm_g, diff_knorm_g, diff_lambda,
           diff_subln_g):
    cond = jnp.concatenate(
        [c_ctx[None, :], c, jnp.zeros((COND_PAD - N_COND, D_MODEL), F32)], axis=0)
    mod = _modulation(cond, w_mod, b_mod)

    cos_g, sin_g = _rope_tables(DEC_SEQ, HEAD_DIM, 1)
    cos_d, sin_d = _rope_tables(DEC_SEQ, DIFF_QK_DIM, 2)
    cache_dk = cache_diff_k.reshape(DEC_BATCH, DEPTH, PAST_LEN, DIFF_HEADS, 2 * DIFF_QK_DIM)
    ctx_rows = dict(n_tiles=CTX_TILES, tile0=0)
    lat_rows = dict(n_tiles=LAT_TILES, tile0=CTX_TILES)

    x = None
    states = gqa_kv = diff_kv = None
    for l in range(DEPTH):
        ffn_w = (mod[l], norm_g[l], ffn_w_gate, ffn_w_up, ffn_w_down, l)
        if l == 0:
            x = _ffn(x_prompt.reshape(N_CTX, D_MODEL), *ffn_w, 0, **ctx_rows)
            x = _ffn(x_sample.reshape(N_LAT, D_MODEL), *ffn_w, 0, **lat_rows,
                     out_off=CTX_TILES, o_prev=x)
        else:
            x = _ffn(x, *ffn_w, 0)
        proj = _proj_in(x, mod[l], norm_g[l], w_in, l)

        o_h, states = _hgrn(proj, hgrn_lb_raw, hgrn_onorm_g[l], l, s_prev=states)
        o_h = _hgrn(proj, hgrn_lb_raw, hgrn_onorm_g[l], l, state0=state_hgrn, o_prev=o_h)

        gqa_w = (proj, gqa_qnorm_g[l], gqa_knorm_g[l], l)
        o_g, *gqa_kv = _gqa(*gqa_w, kv_prev=gqa_kv)
        o_g = _gqa(*gqa_w, latent_args=(cache_gqa_k, cache_gqa_v, cos_g, sin_g, o_g))

        diff_w = (proj, diff_qnorm_g[l], diff_knorm_g[l], diff_subln_g[l], diff_lambda[l], l)
        o_d, *diff_kv = _diff(*diff_w, kv_prev=diff_kv)
        o_d = _diff(*diff_w, latent_args=(cache_dk, cache_diff_v, cos_d, sin_d, o_d))

        x = _proj_out(x, mod[l], o_h, o_g, o_d, w_out, l)
        if l < DEPTH - 1:
            x = _ffn(x, *ffn_w, 1)
        else:
            y_prompt = _ffn(x, *ffn_w, 1, **ctx_rows, out_rows=N_CTX)
            y_sample = _ffn(x, *ffn_w, 1, **lat_rows, in_off=CTX_TILES, out_rows=N_LAT)

    return (y_prompt.reshape(BATCH, SEQ, D_MODEL), y_sample.reshape(DEC_BATCH, DEC_SEQ, D_MODEL),
            gqa_kv[0].reshape(BATCH, DEPTH, SEQ, GQA_KV_HEADS, HEAD_DIM),
            gqa_kv[1].reshape(BATCH, DEPTH, SEQ, GQA_KV_HEADS, HEAD_DIM),
            diff_kv[0].reshape(BATCH, DEPTH, SEQ, DIFF_HEADS, 2, DIFF_QK_DIM),
            diff_kv[1].reshape(BATCH, DEPTH, SEQ, DIFF_HEADS, HEAD_DIM),
            states)
```
